```python
import math, functools
import jax, jax.numpy as jnp
from jax import lax
import numpy as np

D_MODEL = 2048
BATCH = 16
SEQ = 256
DEPTH = 2
DEC_BATCH = 8
DEC_SEQ = 1024
PAST_LEN = 512

GRID_W = 64
N_EVEN = (DEPTH + 1) // 2
N_ODD = DEPTH // 2
HEAD_DIM = 128
HA = 8
NOPE_DIM = 128
ROPE_DIM = 64
V_DIM = 128
QK_DIM = NOPE_DIM + ROPE_DIM
Q_LORA = 512
KV_LORA = 512
HB = 8
NA_KH = 8
NA_KW = 16
HC = 16
KVH_C = 4
GROUPS_C = HC // KVH_C
WINDOW = 128
WINDOW_BLOCK = 128
D_FF = 5632
N_MOD = 9
IN_EVEN = Q_LORA + KV_LORA + ROPE_DIM + 3 * HB * HEAD_DIM
MIX_EVEN = HA * V_DIM + HB * HEAD_DIM
IN_ODD = HC * HEAD_DIM + 2 * KVH_C * HEAD_DIM
MIX_ODD = HC * HEAD_DIM
ROPE_THETA = 10000.0
EPS = 1e-6
NEG_INF = -1e30
QBLOCK = 128

kernel_name = "hybrid_dit_mla_natten_swa_step"


def rms_norm(x, g):
    xf = x.astype(jnp.float32)
    xf = xf * lax.rsqrt(jnp.mean(xf * xf, axis=-1, keepdims=True) + EPS)
    return (xf * g.astype(jnp.float32)).astype(x.dtype)


def modulate(x, g, shift, scale):
    return rms_norm(x, g) * (1 + scale) + shift


def modulation(cond, ada_w, ada_b):
    m = jax.nn.silu(cond) @ ada_w + ada_b
    return jnp.split(m[:, None, :], N_MOD, axis=-1)


def swiglu(h, w_in, w_out):
    gate, up = jnp.split(h @ w_in, 2, axis=-1)
    return (jax.nn.silu(gate) * up) @ w_out


def axial_rope(x):
    S, R = x.shape[1], x.shape[-1]
    half = R // 2
    nf = half // 2
    t = jnp.arange(S)
    inv_freq = ROPE_THETA ** (-jnp.arange(nf, dtype=jnp.float32) / nf)

    def rot(xa, pos):
        ang = pos.astype(jnp.float32)[:, None] * inv_freq[None, :]
        cos = jnp.cos(ang)[None, :, None, :]
        sin = jnp.sin(ang)[None, :, None, :]
        xa = xa.astype(jnp.float32)
        x1, x2 = xa[..., :nf], xa[..., nf:]
        return jnp.concatenate([x1 * cos - x2 * sin, x1 * sin + x2 * cos], axis=-1)

    out = jnp.concatenate([rot(x[..., :half], t // GRID_W), rot(x[..., half:], t % GRID_W)], axis=-1)
    return out.astype(x.dtype)


def rope_tail(x, n_rot):
    return jnp.concatenate([x[..., :-n_rot], axial_rope(x[..., -n_rot:])], axis=-1)


def blocked_attention(q, k, v, scale, sink=None):
    B, S = q.shape[:2]
    nq = S // QBLOCK
    qb = q.reshape(B, nq, QBLOCK, *q.shape[2:]).swapaxes(0, 1)

    def one(qblk):
        s = jnp.einsum('bqhgd,bthd->bhgqt', qblk, k).astype(jnp.float32) * scale
        if sink is not None:
            s_sink = jnp.broadcast_to(sink[None, :, :, None, None].astype(jnp.float32), s.shape[:-1] + (1,))
            s = jnp.concatenate([s, s_sink], axis=-1)
        p = jax.nn.softmax(s, axis=-1)
        if sink is not None:
            p = p[..., :-1]
        return jnp.einsum('bhgqt,bthd->bqhgd', p.astype(v.dtype), v)

    out = lax.map(one, qb)
    return out.swapaxes(0, 1).reshape(B, S, *out.shape[3:])


def neighborhood_attention(q, k, v, k_ctx, v_ctx, rpb):
    B, S, H, Dh = q.shape
    rows = S // GRID_W
    kh = min(NA_KH, rows)
    kw = NA_KW
    ncb = GRID_W // kw
    cbw = 2 * kw
    scale = 1.0 / math.sqrt(Dh)
    qg = q.reshape(B, rows, GRID_W, H, Dh)
    kg = k.reshape(B, rows, GRID_W, H, Dh)
    vg = v.reshape(B, rows, GRID_W, H, Dh)
    band_start = np.clip(np.arange(ncb) * kw - kw // 2, 0, GRID_W - cbw)
    key_col = band_start[:, None] + np.arange(cbw)[None, :]
    q_col = np.arange(ncb)[:, None] * kw + np.arange(kw)[None, :]
    win_start = np.clip(q_col - kw // 2, 0, GRID_W - kw)
    kc = key_col[:, None, :]
    col_valid = jnp.asarray((kc >= win_start[..., None]) & (kc < win_start[..., None] + kw))
    col_off = np.clip(kc - q_col[..., None], -(NA_KW - 1), NA_KW - 1) + (NA_KW - 1)
    col_bias = rpb[:, :, col_off]

    def one_row(r):
        rs = jnp.clip(r - kh // 2, 0, rows - kh)
        k_band = lax.dynamic_slice_in_dim(kg, rs, kh, axis=1)[:, :, key_col]
        v_band = lax.dynamic_slice_in_dim(vg, rs, kh, axis=1)[:, :, key_col]
        q_r = lax.dynamic_index_in_dim(qg, r, axis=1, keepdims=False).reshape(B, ncb, kw, H, Dh)
        s_loc = jnp.einsum('bjqhd,bajchd->bjqhac', q_r, k_band).astype(jnp.float32) * scale
        row_off = rs + jnp.arange(kh) - r + (NA_KH - 1)
        bias = col_bias[:, row_off].transpose(2, 3, 0, 1, 4).astype(jnp.float32)
        s_loc = jnp.where(col_valid[:, :, None, None, :], s_loc + bias, NEG_INF)
        s_loc = s_loc.reshape(B, ncb, kw, H, kh * cbw)
        s_ctx = jnp.einsum('bjqhd,blhd->bjqhl', q_r, k_ctx).astype(jnp.float32) * scale
        p = jax.nn.softmax(jnp.concatenate([s_loc, s_ctx], axis=-1), axis=-1).astype(v.dtype)
        p_loc = p[..., :kh * cbw].reshape(B, ncb, kw, H, kh, cbw)
        p_ctx = p[..., kh * cbw:]
        out = (jnp.einsum('bjqhac,bajchd->bjqhd', p_loc, v_band)
               + jnp.einsum('bjqhl,blhd->bjqhd', p_ctx, v_ctx))
        return out.reshape(B, GRID_W, H, Dh)

    out = lax.map(one_row, jnp.arange(rows))
    return out.transpose(1, 0, 2, 3, 4).reshape(B, S, H, Dh)


def windowed_attention(q, k, v, k_ctx, v_ctx, sink):
    B, S, KVH, G, Dh = q.shape
    wb = WINDOW_BLOCK
    nb = S // wb
    scale = 1.0 / math.sqrt(Dh)
    pad = ((0, 0), (wb, wb), (0, 0), (0, 0))
    kp = jnp.pad(k, pad).reshape(B, nb + 2, wb, KVH, Dh)
    vp = jnp.pad(v, pad).reshape(B, nb + 2, wb, KVH, Dh)
    k_band = jnp.concatenate([kp[:, :-2], kp[:, 1:-1], kp[:, 2:]], axis=2)
    v_band = jnp.concatenate([vp[:, :-2], vp[:, 1:-1], vp[:, 2:]], axis=2)
    qb = q.reshape(B, nb, wb, KVH, G, Dh)

    def one_block(args):
        qblk, kblk, vblk, n = args
        s_loc = jnp.einsum('bqhgd,bthd->bhgqt', qblk, kblk).astype(jnp.float32) * scale
        kabs = n * wb - wb + jnp.arange(3 * wb)
        qabs = n * wb + jnp.arange(wb)
        valid = (jnp.abs(qabs[:, None] - kabs[None, :]) <= WINDOW) & (kabs[None, :] >= 0) & (kabs[None, :] < S)
        s_loc = jnp.where(valid, s_loc, NEG_INF)
        s_ctx = jnp.einsum('bqhgd,blhd->bhgql', qblk, k_ctx).astype(jnp.float32) * scale
        s_sink = jnp.broadcast_to(sink[None, :, :, None, None].astype(jnp.float32), (B, KVH, G, wb, 1))
        p = jax.nn.softmax(jnp.concatenate([s_loc, s_ctx, s_sink], axis=-1), axis=-1).astype(v.dtype)
        return (jnp.einsum('bhgqt,bthd->bqhgd', p[..., :3 * wb], vblk)
                + jnp.einsum('bhgql,blhd->bqhgd', p[..., 3 * wb:-1], v_ctx))

    out = lax.map(one_block, (qb.swapaxes(0, 1), k_band.swapaxes(0, 1), v_band.swapaxes(0, 1), jnp.arange(nb)))
    return out.swapaxes(0, 1).reshape(B, S, KVH, G, Dh)


def even_projections(h, w_in, q_norm, w_q_up, kv_norm, mla_qk_norm, na_qk_norm):
    B, T, _ = h.shape
    i0 = Q_LORA
    i1 = i0 + KV_LORA
    i2 = i1 + ROPE_DIM
    i3 = i2 + HB * HEAD_DIM
    i4 = i3 + HB * HEAD_DIM
    cq, ckv, k_rope, qn, kn, vn = jnp.split(h @ w_in, [i0, i1, i2, i3, i4], axis=-1)
    q_mla = (rms_norm(cq, q_norm) @ w_q_up).reshape(B, T, HA, QK_DIM)
    q_mla = rms_norm(q_mla, mla_qk_norm[0])
    ckv = rms_norm(ckv, kv_norm)
    q_na = rms_norm(qn.reshape(B, T, HB, HEAD_DIM), na_qk_norm[0])
    k_na = rms_norm(kn.reshape(B, T, HB, HEAD_DIM), na_qk_norm[1])
    v_na = vn.reshape(B, T, HB, HEAD_DIM)
    return q_mla, ckv, k_rope, q_na, k_na, v_na


def mla_keys_values(ckv, k_rope, w_kv_up, k_norm):
    B, T, _ = ckv.shape
    kv = (ckv @ w_kv_up).reshape(B, T, HA, NOPE_DIM + V_DIM)
    k = jnp.concatenate([kv[..., :NOPE_DIM], jnp.broadcast_to(k_rope[:, :, None, :], (B, T, HA, ROPE_DIM))], axis=-1)
    return rms_norm(k, k_norm), kv[..., NOPE_DIM:]


def even_mixer_context(w_in, w_out, q_norm, w_q_up, kv_norm, w_kv_up, mla_qk_norm, na_qk_norm, h):
    B, T, _ = h.shape
    q_mla, ckv, k_rope, q_na, k_na, v_na = even_projections(h, w_in, q_norm, w_q_up, kv_norm, mla_qk_norm, na_qk_norm)
    k_mla, v_mla = mla_keys_values(ckv, k_rope, w_kv_up, mla_qk_norm[1])
    o_mla = blocked_attention(q_mla[:, :, :, None], k_mla, v_mla, 1.0 / math.sqrt(QK_DIM))
    o_na = blocked_attention(q_na[:, :, :, None], k_na, v_na, 1.0 / math.sqrt(HEAD_DIM))
    y = jnp.concatenate([o_mla.reshape(B, T, HA * V_DIM), o_na.reshape(B, T, HB * HEAD_DIM)], axis=-1) @ w_out
    return y, (ckv, k_rope, k_na, v_na)


def even_mixer_latent(cache_ckv, cache_krope, cache_k, cache_v, rpb, w_in, w_out, q_norm, w_q_up, kv_norm, w_kv_up,
                      mla_qk_norm, na_qk_norm, h):
    B, S, _ = h.shape
    q_mla, ckv, k_rope, q_na, k_na, v_na = even_projections(h, w_in, q_norm, w_q_up, kv_norm, mla_qk_norm, na_qk_norm)
    k_lat, v_lat = mla_keys_values(ckv, k_rope, w_kv_up, mla_qk_norm[1])
    q_mla = rope_tail(q_mla, ROPE_DIM)
    k_lat = rope_tail(k_lat, ROPE_DIM)
    k_ctx, v_ctx = mla_keys_values(cache_ckv, cache_krope, w_kv_up, mla_qk_norm[1])
    o_mla = blocked_attention(q_mla[:, :, :, None], jnp.concatenate([k_lat, k_ctx], axis=1),
                              jnp.concatenate([v_lat, v_ctx], axis=1), 1.0 / math.sqrt(QK_DIM))
    o_na = neighborhood_attention(q_na, k_na, v_na, cache_k, cache_v, rpb)
    y = jnp.concatenate([o_mla.reshape(B, S, HA * V_DIM), o_na.reshape(B, S, HB * HEAD_DIM)], axis=-1) @ w_out
    return y, ()


def odd_projections(h, w_in, qk_norm):
    B, T, _ = h.shape
    q, k, v = jnp.split(h @ w_in, [HC * HEAD_DIM, (HC + KVH_C) * HEAD_DIM], axis=-1)
    q = rms_norm(q.reshape(B, T, HC, HEAD_DIM), qk_norm[0])
    k = rms_norm(k.reshape(B, T, KVH_C, HEAD_DIM), qk_norm[1])
    return q, k, v.reshape(B, T, KVH_C, HEAD_DIM)


def odd_mixer_context(w_in, w_out, qk_norm, sink, h):
    B, T, _ = h.shape
    q, k, v = odd_projections(h, w_in, qk_norm)
    o = blocked_attention(q.reshape(B, T, KVH_C, GROUPS_C, HEAD_DIM), k, v, 1.0 / math.sqrt(HEAD_DIM),
                          sink=sink.reshape(KVH_C, GROUPS_C))
    return o.reshape(B, T, MIX_ODD) @ w_out, (k, v)


def odd_mixer_latent(cache_k, cache_v, w_in, w_out, qk_norm, sink, h):
    B, S, _ = h.shape
    q, k, v = odd_projections(h, w_in, qk_norm)
    q = axial_rope(q)
    k = axial_rope(k)
    o = windowed_attention(q.reshape(B, S, KVH_C, GROUPS_C, HEAD_DIM), k, v, cache_k, cache_v,
                           sink.reshape(KVH_C, GROUPS_C))
    return o.reshape(B, S, MIX_ODD) @ w_out, ()


def macaron_layer(x, mods, norm_g, ffn_w_in, ffn_w_out, mixer):
    sh1, sc1, g1, sh2, sc2, g2, sh3, sc3, g3 = mods
    x = x + 0.5 * g1 * swiglu(modulate(x, norm_g[0], sh1, sc1), ffn_w_in[0], ffn_w_out[0])
    y, extras = mixer(modulate(x, norm_g[1], sh2, sc2))
    x = x + g2 * y
    x = x + 0.5 * g3 * swiglu(modulate(x, norm_g[2], sh3, sc3), ffn_w_in[1], ffn_w_out[1])
    return x, extras


def setup_inputs(seed: int = 0) -> dict:
    key = jax.random.key(seed)
    ks = iter(jax.random.split(key, 32))

    def normal(shape, scale=1.0):
        return jax.random.normal(next(ks), shape, jnp.float32) * scale

    def gain(shape):
        return 1.0 + normal(shape, 0.02)

    D = D_MODEL
    return {
        "x_prompt": normal((BATCH, SEQ, D)),
        "x_sample": normal((DEC_BATCH, DEC_SEQ, D)),
        "cache_mla_ckv": normal((DEC_BATCH, N_EVEN, PAST_LEN, KV_LORA)),
        "cache_mla_krope": normal((DEC_BATCH, N_EVEN, PAST_LEN, ROPE_DIM)),
        "cache_na_k": normal((DEC_BATCH, N_EVEN, PAST_LEN, HB, HEAD_DIM)),
        "cache_na_v": normal((DEC_BATCH, N_EVEN, PAST_LEN, HB, HEAD_DIM)),
        "cache_gqa_k": normal((DEC_BATCH, N_ODD, PAST_LEN, KVH_C, HEAD_DIM)),
        "cache_gqa_v": normal((DEC_BATCH, N_ODD, PAST_LEN, KVH_C, HEAD_DIM)),
        "c": normal((DEC_BATCH, D)),
        "c_ctx": normal((D,)),
        "ada_w": normal((DEPTH, D, N_MOD * D), 0.5 * D ** -0.5),
        "ada_b": normal((DEPTH, N_MOD * D), 0.02),
        "norm_g": gain((DEPTH, 3, D)),
        "ffn_w_in": normal((DEPTH, 2, D, 2 * D_FF), D ** -0.5),
        "ffn_w_out": normal((DEPTH, 2, D_FF, D), D_FF ** -0.5),
        "even_w_in": normal((N_EVEN, D, IN_EVEN), D ** -0.5),
        "even_w_out": normal((N_EVEN, MIX_EVEN, D), MIX_EVEN ** -0.5),
        "mla_q_norm": gain((N_EVEN, Q_LORA)),
        "mla_w_q_up": normal((N_EVEN, Q_LORA, HA * QK_DIM), Q_LORA ** -0.5),
        "mla_kv_norm": gain((N_EVEN, KV_LORA)),
        "mla_w_kv_up": normal((N_EVEN, KV_LORA, HA * (NOPE_DIM + V_DIM)), KV_LORA ** -0.5),
        "mla_qk_norm": gain((N_EVEN, 2, QK_DIM)),
        "na_qk_norm": gain((N_EVEN, 2, HEAD_DIM)),
        "na_rpb": normal((N_EVEN, HB, 2 * NA_KH - 1, 2 * NA_KW - 1), 0.1),
        "odd_w_in": normal((N_ODD, D, IN_ODD), D ** -0.5),
        "odd_w_out": normal((N_ODD, MIX_ODD, D), MIX_ODD ** -0.5),
        "gqa_qk_norm": gain((N_ODD, 2, HEAD_DIM)),
        "gqa_sink": normal((N_ODD, HC), 0.5),
    }


def reference(x_prompt, x_sample, cache_mla_ckv, cache_mla_krope, cache_na_k, cache_na_v, cache_gqa_k, cache_gqa_v,
              c, c_ctx, ada_w, ada_b, norm_g, ffn_w_in, ffn_w_out, even_w_in, even_w_out, mla_q_norm, mla_w_q_up,
              mla_kv_norm, mla_w_kv_up, mla_qk_norm, na_qk_norm, na_rpb, odd_w_in, odd_w_out, gqa_qk_norm, gqa_sink):
    xp, xs = x_prompt, x_sample
    ctx_cond = c_ctx[None, :]
    ckv_l, kr_l, nak_l, nav_l, gk_l, gv_l = [], [], [], [], [], []
    for layer in range(DEPTH):
        mods_ctx = modulation(ctx_cond, ada_w[layer], ada_b[layer])
        mods_lat = modulation(c, ada_w[layer], ada_b[layer])
        common = (norm_g[layer], ffn_w_in[layer], ffn_w_out[layer])
        e = layer // 2
        if layer % 2 == 0:
            mix_w = (even_w_in[e], even_w_out[e], mla_q_norm[e], mla_w_q_up[e], mla_kv_norm[e], mla_w_kv_up[e],
                     mla_qk_norm[e], na_qk_norm[e])
            xp, (ckv, kr, nak, nav) = macaron_layer(xp, mods_ctx, *common,
                                                    functools.partial(even_mixer_context, *mix_w))
            xs, _ = macaron_layer(xs, mods_lat, *common,
                                  functools.partial(even_mixer_latent, cache_mla_ckv[:, e], cache_mla_krope[:, e],
                                                    cache_na_k[:, e], cache_na_v[:, e], na_rpb[e], *mix_w))
            ckv_l.append(ckv)
            kr_l.append(kr)
            nak_l.append(nak)
            nav_l.append(nav)
        else:
            mix_w = (odd_w_in[e], odd_w_out[e], gqa_qk_norm[e], gqa_sink[e])
            xp, (gk, gv) = macaron_layer(xp, mods_ctx, *common, functools.partial(odd_mixer_context, *mix_w))
            xs, _ = macaron_layer(xs, mods_lat, *common,
                                  functools.partial(odd_mixer_latent, cache_gqa_k[:, e], cache_gqa_v[:, e], *mix_w))
            gk_l.append(gk)
            gv_l.append(gv)
    return (xp, xs, jnp.stack(ckv_l, axis=1), jnp.stack(kr_l, axis=1), jnp.stack(nak_l, axis=1),
            jnp.stack(nav_l, axis=1), jnp.stack(gk_l, axis=1), jnp.stack(gv_l, axis=1))
```

```python
import functools
import math

import numpy as np
import jax
import jax.numpy as jnp
from jax import lax
from jax.experimental import pallas as pl
from jax.experimental.pallas import tpu as pltpu

F32 = jnp.float32
BF16 = jnp.bfloat16

D = 2048
D_FF = 5632
N_MOD = 9
CTX_B, CTX_S = 16, 256
LAT_B, LAT_S = 8, 1024
PAST = 512
T_CTX = CTX_B * CTX_S
T_LAT = LAT_B * LAT_S
T = T_CTX + T_LAT
GRID_W = 64
N_GROUPS = 1 + LAT_B
HEAD = 128
HA = 8
NOPE, ROPE = 128, 64
QK = NOPE + ROPE
QK_PAD = 256
Q_LORA = 512
KV_LORA = 512
HB = 8
NA_KH, NA_KW = 8, 16
HC, KVH_C = 16, 4
GROUPS_C = HC // KVH_C
WINDOW = 128
IN_EVEN_PAD = 4224
IN_ODD = 3072
ROPE_THETA = 10000.0
EPS = 1e-6
NEG_INF = -1e30

VMEM_LIMIT = 56 * 1024 * 1024
ROW_CHUNK = 128


def _cparams(n_axes):
    return pltpu.CompilerParams(dimension_semantics=("arbitrary",) * n_axes,
                                vmem_limit_bytes=VMEM_LIMIT)


def _group_of_tile(i, tm):
    n_ctx = T_CTX // tm
    per = LAT_S // tm
    return jnp.where(i < n_ctx, 0, (i - n_ctx) // per + 1)


def _rope_block_of_tile(i, tm):
    n_ctx = T_CTX // tm
    per = LAT_S // tm
    return jnp.where(i < n_ctx, 0, per + (i - n_ctx) % per)


def _ada_kernel(c_ref, w_ref, b_ref, o_ref):
    c = c_ref[...]
    a = (c * jax.nn.sigmoid(c)).astype(BF16)
    o_ref[0] = jnp.dot(a, w_ref[0].astype(BF16), preferred_element_type=F32) + b_ref[0]


def _ada_modulation(cond, ada_w, ada_b, tn=1024):
    depth = ada_w.shape[0]
    n = N_MOD * D
    rows = cond.shape[0]
    out = pl.pallas_call(
        _ada_kernel,
        grid=(depth, n // tn),
        in_specs=[
            pl.BlockSpec((rows, D), lambda l, j: (0, 0)),
            pl.BlockSpec((1, D, tn), lambda l, j: (l, 0, j)),
            pl.BlockSpec((1, 1, tn), lambda l, j: (l, 0, j)),
        ],
        out_specs=pl.BlockSpec((1, rows, tn), lambda l, j: (l, 0, j)),
        out_shape=jax.ShapeDtypeStruct((depth, rows, n), F32),
        compiler_params=_cparams(2),
        name="ada_modulation",
    )(cond, ada_w, ada_b.reshape(depth, 1, n))
    return out


def _modulate_into(h_ref, x_ref, mod_ref, g_ref, which, tm):
    shift = mod_ref[0, 3 * which:3 * which + 1, :]
    scale = mod_ref[0, 3 * which + 1:3 * which + 2, :]
    gain = g_ref[...] * (1.0 + scale)

    def body(r, carry):
        rows = pl.ds(pl.multiple_of(r * ROW_CHUNK, ROW_CHUNK), ROW_CHUNK)
        x = x_ref[rows, :]
        inv = lax.rsqrt(jnp.mean(x * x, axis=-1, keepdims=True) + EPS)
        h_ref[rows, :] = ((x * inv) * gain + shift).astype(BF16)
        return carry

    lax.fori_loop(0, tm // ROW_CHUNK, body, 0)


def _ffn_in_kernel(x_ref, mod_ref, g_ref, wg_ref, wu_ref, o_ref, h_ref, *, which, tm):
    @pl.when(pl.program_id(1) == 0)
    def _():
        _modulate_into(h_ref, x_ref, mod_ref, g_ref, which, tm)

    h = h_ref[...]
    gate = jnp.dot(h, wg_ref[...], preferred_element_type=F32)
    up = jnp.dot(h, wu_ref[...], preferred_element_type=F32)
    o_ref[...] = (gate * jax.nn.sigmoid(gate) * up).astype(BF16)


def _ffn_in(x, mods, g, w_in, which, tm=1024, tn=512):
    nj = D_FF // tn
    return pl.pallas_call(
        functools.partial(_ffn_in_kernel, which=which, tm=tm),
        grid=(T // tm, nj),
        in_specs=[
            pl.BlockSpec((tm, D), lambda i, j: (i, 0)),
            pl.BlockSpec((1, N_MOD, D), lambda i, j: (_group_of_tile(i, tm), 0, 0)),
            pl.BlockSpec((1, D), lambda i, j: (0, 0)),
            pl.BlockSpec((D, tn), lambda i, j: (0, j)),
            pl.BlockSpec((D, tn), lambda i, j: (0, j + nj)),
        ],
        out_specs=pl.BlockSpec((tm, tn), lambda i, j: (i, j)),
        out_shape=jax.ShapeDtypeStruct((T, D_FF), BF16),
        scratch_shapes=[pltpu.VMEM((tm, D), BF16)],
        compiler_params=_cparams(2),
        name="ffn_in",
    )(x, mods, g, w_in, w_in)


def _mod_proj_kernel(x_ref, mod_ref, g_ref, w_ref, o_ref, h_ref, *, which, tm):
    @pl.when(pl.program_id(1) == 0)
    def _():
        _modulate_into(h_ref, x_ref, mod_ref, g_ref, which, tm)

    o_ref[...] = jnp.dot(h_ref[...], w_ref[...], preferred_element_type=F32)


def _mod_proj(x, mods, g, w, which, tm, tn):
    n = w.shape[1]
    return pl.pallas_call(
        functools.partial(_mod_proj_kernel, which=which, tm=tm),
        grid=(T // tm, n // tn),
        in_specs=[
            pl.BlockSpec((tm, D), lambda i, j: (i, 0)),
            pl.BlockSpec((1, N_MOD, D), lambda i, j: (_group_of_tile(i, tm), 0, 0)),
            pl.BlockSpec((1, D), lambda i, j: (0, 0)),
            pl.BlockSpec((D, tn), lambda i, j: (0, j)),
        ],
        out_specs=pl.BlockSpec((tm, tn), lambda i, j: (i, j)),
        out_shape=jax.ShapeDtypeStruct((T, n), F32),
        scratch_shapes=[pltpu.VMEM((tm, D), BF16)],
        compiler_params=_cparams(2),
        name="mixer_in_proj",
    )(x, mods, g, w)


def _mm_residual_kernel(a_ref, w_ref, x_ref, mod_ref, o_ref, *, gate_row, coef):
    acc = jnp.dot(a_ref[...], w_ref[...], preferred_element_type=F32)
    gate = mod_ref[0, gate_row:gate_row + 1, :]
    o_ref[...] = x_ref[...] + (coef * gate) * acc


def _mm_residual(a, w, x, mods, gate_row, coef, tm, tn, name):
    k = a.shape[1]
    return pl.pallas_call(
        functools.partial(_mm_residual_kernel, gate_row=gate_row, coef=coef),
        grid=(T // tm, D // tn),
        in_specs=[
            pl.BlockSpec((tm, k), lambda i, j: (i, 0)),
            pl.BlockSpec((k, tn), lambda i, j: (0, j)),
            pl.BlockSpec((tm, tn), lambda i, j: (i, j)),
            pl.BlockSpec((1, N_MOD, tn), lambda i, j: (_group_of_tile(i, tm), 0, j)),
        ],
        out_specs=pl.BlockSpec((tm, tn), lambda i, j: (i, j)),
        out_shape=jax.ShapeDtypeStruct((T, D), F32),
        compiler_params=_cparams(2),
        name=name,
    )(a, w, x, mods)


def _rope(x, cos, sin, nf):
    lane = lax.broadcasted_iota(jnp.int32, x.shape, 1)
    first = (lane & (2 * nf - 1)) < nf
    partner = jnp.where(first, pltpu.roll(x, 128 - nf, 1), pltpu.roll(x, nf, 1))
    return x * cos + partner * sin


def _rms(x, g, n):
    inv = lax.rsqrt(jnp.sum(x * x, axis=-1, keepdims=True) * (1.0 / n) + EPS)
    return x * inv * g


def _mla_keys_values(ckvn, kr, wk_ref, wv_ref, gkn_ref, gkr_ref, cos, sin, km_ref, vm_ref):
    c16 = ckvn.astype(BF16)
    kn = jnp.dot(c16, wk_ref[...], preferred_element_type=F32)
    vm_ref[...] = jnp.dot(c16, wv_ref[...], preferred_element_type=F32).astype(BF16)
    kr_ss = jnp.sum(kr * kr, axis=-1, keepdims=True)
    for h in range(HA):
        x = kn[:, h * NOPE:(h + 1) * NOPE]
        inv = lax.rsqrt((jnp.sum(x * x, axis=-1, keepdims=True) + kr_ss) * (1.0 / QK) + EPS)
        km_ref[:, h * QK_PAD:h * QK_PAD + NOPE] = (x * inv * gkn_ref[...]).astype(BF16)
        r = kr * inv * gkr_ref[...]
        if cos is not None:
            r = _rope(r, cos, sin, ROPE // 4)
        km_ref[:, h * QK_PAD + NOPE:(h + 1) * QK_PAD] = r.astype(BF16)


def _even_prep_kernel(p_ref, wq_ref, wk_ref, wv_ref, qn_ref, kvn_ref, gq_ref, gkn_ref, gkr_ref,
                      gnaq_ref, gnak_ref, cos_ref, sin_ref,
                      qm_ref, km_ref, vm_ref, qna_ref, kna_ref, vna_ref, ckv_ref, knaf_ref):
    cos = cos_ref[...]
    sin = sin_ref[...]
    cq = _rms(p_ref[:, 0:Q_LORA], qn_ref[...], Q_LORA)
    q = jnp.dot(cq.astype(BF16), wq_ref[...], preferred_element_type=F32)
    for h in range(HA):
        y = _rms(q[:, h * QK_PAD:(h + 1) * QK_PAD], gq_ref[...], QK)
        qm_ref[:, h * QK_PAD:h * QK_PAD + NOPE] = y[:, :NOPE].astype(BF16)
        qm_ref[:, h * QK_PAD + NOPE:(h + 1) * QK_PAD] = _rope(y[:, NOPE:], cos, sin, ROPE // 4).astype(BF16)
    ckvn = _rms(p_ref[:, Q_LORA:Q_LORA + KV_LORA], kvn_ref[...], KV_LORA)
    ckv_ref[...] = ckvn
    kr = p_ref[:, IN_EVEN_PAD - 128:IN_EVEN_PAD]
    _mla_keys_values(ckvn, kr, wk_ref, wv_ref, gkn_ref, gkr_ref, cos, sin, km_ref, vm_ref)
    base = Q_LORA + KV_LORA
    for h in range(HB):
        qh = _rms(p_ref[:, base + h * HEAD:base + (h + 1) * HEAD], gnaq_ref[...], HEAD)
        qna_ref[:, h * HEAD:(h + 1) * HEAD] = qh.astype(BF16)
        kh = _rms(p_ref[:, base + (HB + h) * HEAD:base + (HB + h + 1) * HEAD], gnak_ref[...], HEAD)
        knaf_ref[:, h * HEAD:(h + 1) * HEAD] = kh
        kna_ref[:, h * HEAD:(h + 1) * HEAD] = kh.astype(BF16)
    vna_ref[...] = p_ref[:, base + 2 * HB * HEAD:base + 3 * HB * HEAD].astype(BF16)


def _even_prep(p, wq, wk, wv, qn, kvn, gq, gkn, gkr, gnaq, gnak, cos, sin, tm=256):
    full = lambda shape: pl.BlockSpec(shape, lambda i: (0, 0))
    rows = lambda w: pl.BlockSpec((tm, w), lambda i: (i, 0))
    table = pl.BlockSpec((tm, 128), lambda i: (_rope_block_of_tile(i, tm), 0))
    return pl.pallas_call(
        _even_prep_kernel,
        grid=(T // tm,),
        in_specs=[rows(IN_EVEN_PAD), full(wq.shape), full(wk.shape), full(wv.shape), full(qn.shape),
                  full(kvn.shape), full(gq.shape), full(gkn.shape), full(gkr.shape), full(gnaq.shape),
                  full(gnak.shape), table, table],
        out_specs=[rows(HA * QK_PAD), rows(HA * QK_PAD), rows(HA * HEAD), rows(HB * HEAD), rows(HB * HEAD),
                   rows(HB * HEAD), rows(KV_LORA), rows(HB * HEAD)],
        out_shape=[jax.ShapeDtypeStruct((T, HA * QK_PAD), BF16), jax.ShapeDtypeStruct((T, HA * QK_PAD), BF16),
                   jax.ShapeDtypeStruct((T, HA * HEAD), BF16), jax.ShapeDtypeStruct((T, HB * HEAD), BF16),
                   jax.ShapeDtypeStruct((T, HB * HEAD), BF16), jax.ShapeDtypeStruct((T, HB * HEAD), BF16),
                   jax.ShapeDtypeStruct((T, KV_LORA), F32), jax.ShapeDtypeStruct((T, HB * HEAD), F32)],
        compiler_params=_cparams(1),
        name="even_prep",
    )(p, wq, wk, wv, qn, kvn, gq, gkn, gkr, gnaq, gnak, cos, sin)


def _cache_kv_kernel(ckv_ref, kr_ref, wk_ref, wv_ref, gkn_ref, gkr_ref, km_ref, vm_ref):
    _mla_keys_values(ckv_ref[...], kr_ref[...], wk_ref, wv_ref, gkn_ref, gkr_ref, None, None, km_ref, vm_ref)


def _cache_kv(ckv, kr, wk, wv, gkn, gkr, tm=512):
    n = ckv.shape[0]
    full = lambda shape: pl.BlockSpec(shape, lambda i: (0, 0))
    rows = lambda w: pl.BlockSpec((tm, w), lambda i: (i, 0))
    return pl.pallas_call(
        _cache_kv_kernel,
        grid=(n // tm,),
        in_specs=[rows(KV_LORA), rows(128), full(wk.shape), full(wv.shape), full(gkn.shape), full(gkr.shape)],
        out_specs=[rows(HA * QK_PAD), rows(HA * HEAD)],
        out_shape=[jax.ShapeDtypeStruct((n, HA * QK_PAD), BF16), jax.ShapeDtypeStruct((n, HA * HEAD), BF16)],
        compiler_params=_cparams(1),
        name="mla_cache_kv",
    )(ckv, kr, wk, wv, gkn, gkr)


def _odd_prep_kernel(p_ref, gq_ref, gk_ref, cos_ref, sin_ref, q_ref, k_ref, v_ref, kf_ref):
    cos = cos_ref[...]
    sin = sin_ref[...]
    for h in range(HC):
        y = _rms(p_ref[:, h * HEAD:(h + 1) * HEAD], gq_ref[...], HEAD)
        q_ref[:, h * HEAD:(h + 1) * HEAD] = _rope(y, cos, sin, HEAD // 4).astype(BF16)
    base = HC * HEAD
    for h in range(KVH_C):
        y = _rms(p_ref[:, base + h * HEAD:base + (h + 1) * HEAD], gk_ref[...], HEAD)
        kf_ref[:, h * HEAD:(h + 1) * HEAD] = y
        k_ref[:, h * HEAD:(h + 1) * HEAD] = _rope(y, cos, sin, HEAD // 4).astype(BF16)
    v_ref[...] = p_ref[:, base + KVH_C * HEAD:base + 2 * KVH_C * HEAD].astype(BF16)


def _odd_prep(p, gq, gk, cos, sin, tm=512):
    full = lambda shape: pl.BlockSpec(shape, lambda i: (0, 0))
    rows = lambda w: pl.BlockSpec((tm, w), lambda i: (i, 0))
    table = pl.BlockSpec((tm, 128), lambda i: (_rope_block_of_tile(i, tm), 0))
    return pl.pallas_call(
        _odd_prep_kernel,
        grid=(T // tm,),
        in_specs=[rows(IN_ODD), full(gq.shape), full(gk.shape), table, table],
        out_specs=[rows(HC * HEAD), rows(KVH_C * HEAD), rows(KVH_C * HEAD), rows(KVH_C * HEAD)],
        out_shape=[jax.ShapeDtypeStruct((T, HC * HEAD), BF16), jax.ShapeDtypeStruct((T, KVH_C * HEAD), BF16),
                   jax.ShapeDtypeStruct((T, KVH_C * HEAD), BF16), jax.ShapeDtypeStruct((T, KVH_C * HEAD), F32)],
        compiler_params=_cparams(1),
        name="odd_prep",
    )(p, gq, gk, cos, sin)


def _attn_kernel(*refs, n_kv, has_bias, window, has_sink, has_prev, hps, groups, dk, dv, sq, qb, scale):
    it = iter(refs)
    q_ref = next(it)
    kv_refs = [(next(it), next(it)) for _ in range(n_kv)]
    bias_ref = next(it) if has_bias else None
    sink_ref = next(it) if has_sink else None
    if has_prev:
        next(it)
    o_ref = next(it)
    hblk = pl.program_id(0)

    for hh in range(hps):
        for g in range(groups):
            qcol = (hh * groups + g) * dk
            ocol = (hh * groups + g) * dv

            def q_block(i, carry, hh=hh, g=g, qcol=qcol, ocol=ocol):
                q0 = pl.multiple_of(i * qb, qb)
                q = q_ref[pl.ds(q0, qb), qcol:qcol + dk]
                scores = []
                for idx, (k_ref, _) in enumerate(kv_refs):
                    k = k_ref[:, hh * dk:(hh + 1) * dk].astype(BF16)
                    s = lax.dot_general(q, k, (((1,), (1,)), ((), ())), preferred_element_type=F32) * scale
                    if idx == 0 and has_bias:
                        s = s + bias_ref[0, pl.ds(q0, qb), :]
                    if idx == 0 and window is not None:
                        q_abs = q0 + lax.broadcasted_iota(jnp.int32, s.shape, 0)
                        k_abs = lax.broadcasted_iota(jnp.int32, s.shape, 1)
                        s = jnp.where(jnp.abs(q_abs - k_abs) <= window, s, NEG_INF)
                    scores.append(s)
                m = jnp.max(scores[0], axis=-1, keepdims=True)
                for s in scores[1:]:
                    m = jnp.maximum(m, jnp.max(s, axis=-1, keepdims=True))
                if has_sink:
                    sink = sink_ref[(hblk * hps + hh) * groups + g]
                    m = jnp.maximum(m, sink)
                denom = None
                acc = None
                for s, (_, v_ref) in zip(scores, kv_refs):
                    p = jnp.exp(s - m)
                    part = jnp.sum(p, axis=-1, keepdims=True)
                    v = v_ref[:, hh * dv:(hh + 1) * dv].astype(BF16)
                    pv = jnp.dot(p.astype(BF16), v, preferred_element_type=F32)
                    denom = part if denom is None else denom + part
                    acc = pv if acc is None else acc + pv
                if has_sink:
                    denom = denom + jnp.exp(sink - m)
                o_ref[pl.ds(q0, qb), ocol:ocol + dv] = (acc / denom).astype(o_ref.dtype)
                return carry

            lax.fori_loop(0, sq // qb, q_block, 0)


def _attention(q, kvs, o_prev, *, nb, sq, q_rowblk0, n_kv_heads, hps, groups, dk, dv, scale,
               o_colblk0, bias=None, window=None, sink=None, qb=256, name="attention"):
    grid = (n_kv_heads // hps, nb)
    in_specs = [pl.BlockSpec((sq, hps * groups * dk), lambda h, b: (q_rowblk0 + b, h))]
    args = [q]
    for (k, v, sk, r0) in kvs:
        in_specs.append(pl.BlockSpec((sk, hps * dk), lambda h, b, r0=r0: (r0 + b, h)))
        in_specs.append(pl.BlockSpec((sk, hps * dv), lambda h, b, r0=r0: (r0 + b, h)))
        args += [k, v]
    if bias is not None:
        in_specs.append(pl.BlockSpec((1,) + bias.shape[1:], lambda h, b: (h, 0, 0)))
        args.append(bias)
    if sink is not None:
        in_specs.append(pl.BlockSpec(memory_space=pltpu.SMEM))
        args.append(sink)
    aliases = {}
    if o_prev is not None:
        aliases = {len(args): 0}
        in_specs.append(pl.BlockSpec(memory_space=pl.ANY))
        args.append(o_prev)
    body = functools.partial(
        _attn_kernel, n_kv=len(kvs), has_bias=bias is not None, window=window, has_sink=sink is not None,
        has_prev=o_prev is not None, hps=hps, groups=groups, dk=dk, dv=dv, sq=sq, qb=min(qb, sq), scale=scale)
    return pl.pallas_call(
        body,
        grid=grid,
        in_specs=in_specs,
        out_specs=pl.BlockSpec((sq, hps * groups * dv), lambda h, b: (q_rowblk0 + b, o_colblk0 + h)),
        out_shape=jax.ShapeDtypeStruct((T, D), BF16),
        input_output_aliases=aliases,
        compiler_params=_cparams(2),
        name=name,
    )(*args)


def _rope_tables(rot_dim):
    half = rot_dim // 2
    nf = half // 2
    t = np.arange(LAT_S)
    inv_freq = ROPE_THETA ** (-np.arange(nf, dtype=np.float64) / nf)
    cos = np.zeros((2 * LAT_S, 128), np.float64)
    sin = np.zeros((2 * LAT_S, 128), np.float64)
    cos[:LAT_S, :rot_dim] = 1.0
    for part, pos in enumerate((t // GRID_W, t % GRID_W)):
        ang = pos[:, None].astype(np.float64) * inv_freq[None, :]
        lo = part * half
        cos[LAT_S:, lo:lo + nf] = np.cos(ang)
        cos[LAT_S:, lo + nf:lo + half] = np.cos(ang)
        sin[LAT_S:, lo:lo + nf] = -np.sin(ang)
        sin[LAT_S:, lo + nf:lo + half] = np.sin(ang)
    return jnp.asarray(cos, F32), jnp.asarray(sin, F32)


def _na_bias(rpb):
    rows = LAT_S // GRID_W
    r = np.arange(rows)
    rs = np.clip(r - NA_KH // 2, 0, rows - NA_KH)
    row_valid = (r[None, :] >= rs[:, None]) & (r[None, :] < rs[:, None] + NA_KH)
    row_off = np.clip(r[None, :] - r[:, None] + NA_KH - 1, 0, 2 * NA_KH - 2)
    c = np.arange(GRID_W)
    ws = np.clip(c - NA_KW // 2, 0, GRID_W - NA_KW)
    col_valid = (c[None, :] >= ws[:, None]) & (c[None, :] < ws[:, None] + NA_KW)
    col_off = np.clip(c[None, :] - c[:, None] + NA_KW - 1, 0, 2 * NA_KW - 2)
    t = rpb[:, row_off, :]
    t = t[:, :, :, col_off]
    valid = row_valid[:, :, None, None] & col_valid[None, None, :, :]
    t = jnp.where(jnp.asarray(valid)[None], t, NEG_INF)
    return t.transpose(0, 1, 3, 2, 4).reshape(HB, LAT_S, LAT_S)


def _macaron_ffn(x, mods, g, w_in, w_out, which):
    act = _ffn_in(x, mods, g, w_in, which)
    return _mm_residual(act, w_out, x, mods, 3 * which + 2, 0.5, tm=1024, tn=512, name="ffn_out")


def _even_mixer(x, mods, g, w, caches):
    cache_ckv, cache_krope, cache_nak, cache_nav = caches
    p = _mod_proj(x, mods, g, w["w_in"], 1, tm=1024, tn=IN_EVEN_PAD // 3)
    cos, sin = _rope_tables(ROPE)
    qm, km, vm, qna, kna, vna, ckvn, knaf = _even_prep(
        p, w["wq"], w["wk"], w["wv"], w["q_norm"], w["kv_norm"], w["gq"], w["gkn"], w["gkr"],
        w["gnaq"], w["gnak"], cos, sin)
    km_ctx, vm_ctx = _cache_kv(cache_ckv, cache_krope, w["wk"], w["wv"], w["gkn"], w["gkr"])
    s_mla = 1.0 / math.sqrt(QK)
    s_na = 1.0 / math.sqrt(HEAD)
    lat0 = T_CTX // LAT_S
    o = _attention(qm, [(km, vm, CTX_S, 0)], None, nb=CTX_B, sq=CTX_S, q_rowblk0=0, n_kv_heads=HA, hps=HA,
                   groups=1, dk=QK_PAD, dv=HEAD, scale=s_mla, o_colblk0=0, name="mla_ctx")
    o = _attention(qna, [(kna, vna, CTX_S, 0)], o, nb=CTX_B, sq=CTX_S, q_rowblk0=0, n_kv_heads=HB, hps=HB,
                   groups=1, dk=HEAD, dv=HEAD, scale=s_na, o_colblk0=1, name="na_ctx")
    o = _attention(qm, [(km, vm, LAT_S, lat0), (km_ctx, vm_ctx, PAST, 0)], o, nb=LAT_B, sq=LAT_S,
                   q_rowblk0=lat0, n_kv_heads=HA, hps=1, groups=1, dk=QK_PAD, dv=HEAD, scale=s_mla,
                   o_colblk0=0, name="mla_lat")
    o = _attention(qna, [(kna, vna, LAT_S, lat0), (cache_nak, cache_nav, PAST, 0)], o, nb=LAT_B, sq=LAT_S,
                   q_rowblk0=lat0, n_kv_heads=HB, hps=1, groups=1, dk=HEAD, dv=HEAD, scale=s_na,
                   o_colblk0=HA, bias=w["na_bias"], name="na_lat")
    x = _mm_residual(o, w["w_out"], x, mods, 5, 1.0, tm=1024, tn=1024, name="mixer_out_proj")
    new_ckv = ckvn[:T_CTX].reshape(CTX_B, CTX_S, KV_LORA)
    new_kr = p[:T_CTX, IN_EVEN_PAD - 128:IN_EVEN_PAD - 128 + ROPE].reshape(CTX_B, CTX_S, ROPE)
    new_nak = knaf[:T_CTX].reshape(CTX_B, CTX_S, HB, HEAD)
    vcol = Q_LORA + KV_LORA + 2 * HB * HEAD
    new_nav = p[:T_CTX, vcol:vcol + HB * HEAD].reshape(CTX_B, CTX_S, HB, HEAD)
    return x, (new_ckv, new_kr, new_nak, new_nav)


def _odd_mixer(x, mods, g, w, caches):
    cache_k, cache_v = caches
    p = _mod_proj(x, mods, g, w["w_in"], 1, tm=1024, tn=1024)
    cos, sin = _rope_tables(HEAD)
    q, k, v, kf = _odd_prep(p, w["gq"], w["gk"], cos, sin)
    scale = 1.0 / math.sqrt(HEAD)
    lat0 = T_CTX // LAT_S
    o = _attention(q, [(k, v, CTX_S, 0)], None, nb=CTX_B, sq=CTX_S, q_rowblk0=0, n_kv_heads=KVH_C, hps=KVH_C,
                   groups=GROUPS_C, dk=HEAD, dv=HEAD, scale=scale, o_colblk0=0, sink=w["sink"], name="gqa_ctx")
    o = _attention(q, [(k, v, LAT_S, lat0), (cache_k, cache_v, PAST, 0)], o, nb=LAT_B, sq=LAT_S,
                   q_rowblk0=lat0, n_kv_heads=KVH_C, hps=1, groups=GROUPS_C, dk=HEAD, dv=HEAD, scale=scale,
                   o_colblk0=0, window=WINDOW, sink=w["sink"], name="gqa_lat")
    x = _mm_residual(o, w["w_out"], x, mods, 5, 1.0, tm=1024, tn=1024, name="mixer_out_proj")
    new_k = kf[:T_CTX].reshape(CTX_B, CTX_S, KVH_C, HEAD)
    vcol = (HC + KVH_C) * HEAD
    new_v = p[:T_CTX, vcol:vcol + KVH_C * HEAD].reshape(CTX_B, CTX_S, KVH_C, HEAD)
    return x, (new_k, new_v)


def _even_weights(e, even_w_in, even_w_out, mla_q_norm, mla_w_q_up, mla_kv_norm, mla_w_kv_up, mla_qk_norm,
                  na_qk_norm, na_rpb):
    w_in = even_w_in[e]
    i0, i1, i2 = Q_LORA, Q_LORA + KV_LORA, Q_LORA + KV_LORA + ROPE
    w_in = jnp.concatenate([w_in[:, :i1], w_in[:, i2:], w_in[:, i1:i2], jnp.zeros((D, 128 - ROPE), F32)], axis=1)
    wq = jnp.pad(mla_w_q_up[e].reshape(Q_LORA, HA, QK), ((0, 0), (0, 0), (0, QK_PAD - QK)))
    wkv = mla_w_kv_up[e].reshape(KV_LORA, HA, NOPE + HEAD)
    qk = mla_qk_norm[e]
    return {
        "w_in": w_in.astype(BF16),
        "w_out": even_w_out[e].astype(BF16),
        "wq": wq.reshape(Q_LORA, HA * QK_PAD).astype(BF16),
        "wk": wkv[:, :, :NOPE].reshape(KV_LORA, HA * NOPE).astype(BF16),
        "wv": wkv[:, :, NOPE:].reshape(KV_LORA, HA * HEAD).astype(BF16),
        "q_norm": mla_q_norm[e][None, :],
        "kv_norm": mla_kv_norm[e][None, :],
        "gq": jnp.pad(qk[0], (0, QK_PAD - QK))[None, :],
        "gkn": qk[1, :NOPE][None, :],
        "gkr": jnp.pad(qk[1, NOPE:], (0, 128 - ROPE))[None, :],
        "gnaq": na_qk_norm[e, 0][None, :],
        "gnak": na_qk_norm[e, 1][None, :],
        "na_bias": _na_bias(na_rpb[e]),
    }


def kernel(x_prompt, x_sample, cache_mla_ckv, cache_mla_krope, cache_na_k, cache_na_v, cache_gqa_k, cache_gqa_v, c, c_ctx, ada_w, ada_b, norm_g, ffn_w_in, ffn_w_out, even_w_in, even_w_out, mla_q_norm, mla_w_q_up, mla_kv_norm, mla_w_kv_up, mla_qk_norm, na_qk_norm, na_rpb, odd_w_in, odd_w_out, gqa_qk_norm, gqa_sink):
    depth = ada_w.shape[0]
    x = jnp.concatenate([x_prompt.reshape(T_CTX, D), x_sample.reshape(T_LAT, D)], axis=0)
    cond = jnp.concatenate([c_ctx[None, :], c, jnp.zeros((16 - N_GROUPS, D), F32)], axis=0)
    mods_all = _ada_modulation(cond, ada_w, ada_b)

    ckv_l, kr_l, nak_l, nav_l, gk_l, gv_l = [], [], [], [], [], []
    for layer in range(depth):
        mods = mods_all[layer, :N_GROUPS].reshape(N_GROUPS, N_MOD, D)
        g = norm_g[layer]
        w_in = ffn_w_in[layer].astype(BF16)
        w_out = ffn_w_out[layer].astype(BF16)
        e = layer // 2
        x = _macaron_ffn(x, mods, g[0:1], w_in[0], w_out[0], 0)
        if layer % 2 == 0:
            w = _even_weights(e, even_w_in, even_w_out, mla_q_norm, mla_w_q_up, mla_kv_norm, mla_w_kv_up,
                              mla_qk_norm, na_qk_norm, na_rpb)
            caches = (cache_mla_ckv[:, e].reshape(LAT_B * PAST, KV_LORA),
                      jnp.pad(cache_mla_krope[:, e].reshape(LAT_B * PAST, ROPE), ((0, 0), (0, 128 - ROPE))),
                      cache_na_k[:, e].reshape(LAT_B * PAST, HB * HEAD),
                      cache_na_v[:, e].reshape(LAT_B * PAST, HB * HEAD))
            x, (ckv, kr, nak, nav) = _even_mixer(x, mods, g[1:2], w, caches)
            ckv_l.append(ckv)
            kr_l.append(kr)
            nak_l.append(nak)
            nav_l.append(nav)
        else:
            w = {"w_in": odd_w_in[e].astype(BF16), "w_out": odd_w_out[e].astype(BF16),
                 "gq": gqa_qk_norm[e, 0][None, :], "gk": gqa_qk_norm[e, 1][None, :], "sink": gqa_sink[e]}
            caches = (cache_gqa_k[:, e].reshape(LAT_B * PAST, KVH_C * HEAD),
                      cache_gqa_v[:, e].reshape(LAT_B * PAST, KVH_C * HEAD))
            x, (gk, gv) = _odd_mixer(x, mods, g[1:2], w, caches)
            gk_l.append(gk)
            gv_l.append(gv)
        x = _macaron_ffn(x, mods, g[2:3], w_in[1], w_out[1], 2)

    y_prompt = x[:T_CTX].reshape(CTX_B, CTX_S, D)
    y_sample = x[T_CTX:].reshape(LAT_B, LAT_S, D)
    return (y_prompt, y_sample, jnp.stack(ckv_l, axis=1), jnp.stack(kr_l, axis=1), jnp.stack(nak_l, axis=1),
            jnp.stack(nav_l, axis=1), jnp.stack(gk_l, axis=1), jnp.stack(gv_l, axis=1))
```

```python
import functools
import math

import numpy as np
import jax
import jax.numpy as jnp
from jax import lax
from jax.experimental import pallas as pl
from jax.experimental.pallas import tpu as pltpu

F32 = jnp.float32
BF16 = jnp.bfloat16

D = 2048
D_FF = 5632
N_MOD = 9
CTX_B, CTX_S = 16, 256
LAT_B, LAT_S = 8, 1024
PAST = 512
T_CTX = CTX_B * CTX_S
T_LAT = LAT_B * LAT_S
T = T_CTX + T_LAT
GRID_W = 64
GRID_H = LAT_S // GRID_W
N_GROUPS = 1 + LAT_B
HEAD = 128
HA = 8
NOPE, ROPE = 128, 64
QK = NOPE + ROPE
QK_PAD = 256
Q_LORA = 512
KV_LORA = 512
HB = 8
NA_KH, NA_KW = 8, 16
HC, KVH_C = 16, 4
GROUPS_C = HC // KVH_C
WINDOW = 128
IN_EVEN_PAD = 4224
IN_ODD = 3072
ROPE_THETA = 10000.0
EPS = 1e-6
NEG_INF = -1e30
LOG2E = math.log2(math.e)

VMEM_LIMIT = 56 * 1024 * 1024
ROW_CHUNK = 128
Q_BLOCK = 256

_NT = (((1,), (1,)), ((), ()))


def _cparams(n_axes):
    return pltpu.CompilerParams(dimension_semantics=("arbitrary",) * n_axes,
                                vmem_limit_bytes=VMEM_LIMIT)


def _group_of_tile(i, tm):
    n_ctx = T_CTX // tm
    per = LAT_S // tm
    return jnp.where(i < n_ctx, 0, (i - n_ctx) // per + 1)


def _rope_block_of_tile(i, tm):
    n_ctx = T_CTX // tm
    per = LAT_S // tm
    return jnp.where(i < n_ctx, 0, per + (i - n_ctx) % per)


def _weight_spec(lead, rows, tn):
    return pl.BlockSpec((None,) * len(lead) + (rows, tn), lambda i, j: lead + (0, j))


def _ada_kernel(c_ref, w_ref, b_ref, o_ref):
    c = c_ref[...]
    a = (c * jax.nn.sigmoid(c)).astype(BF16)
    o_ref[0] = jnp.dot(a, w_ref[0].astype(BF16), preferred_element_type=F32) + b_ref[0]


def _ada_modulation(cond, ada_w, ada_b, tn=1024):
    depth = ada_w.shape[0]
    n = N_MOD * D
    rows = cond.shape[0]
    return pl.pallas_call(
        _ada_kernel,
        grid=(depth, n // tn),
        in_specs=[
            pl.BlockSpec((rows, D), lambda l, j: (0, 0)),
            pl.BlockSpec((1, D, tn), lambda l, j: (l, 0, j)),
            pl.BlockSpec((1, 1, tn), lambda l, j: (l, 0, j)),
        ],
        out_specs=pl.BlockSpec((1, rows, tn), lambda l, j: (l, 0, j)),
        out_shape=jax.ShapeDtypeStruct((depth, rows, n), F32),
        compiler_params=_cparams(2),
        name="ada_modulation",
    )(cond, ada_w, ada_b.reshape(depth, 1, n))


def _modulate_into(h_ref, x_ref, mod_ref, g_ref, which, tm):
    shift = mod_ref[0, 3 * which:3 * which + 1, :]
    scale = mod_ref[0, 3 * which + 1:3 * which + 2, :]
    gain = g_ref[...] * (1.0 + scale)

    def body(r, carry):
        rows = pl.ds(pl.multiple_of(r * ROW_CHUNK, ROW_CHUNK), ROW_CHUNK)
        x = x_ref[rows, :]
        inv = lax.rsqrt(jnp.mean(x * x, axis=-1, keepdims=True) + EPS)
        h_ref[rows, :] = ((x * inv) * gain + shift).astype(BF16)
        return carry

    lax.fori_loop(0, tm // ROW_CHUNK, body, 0)


def _ffn_in_kernel(x_ref, mod_ref, g_ref, wg_ref, wu_ref, o_ref, h_ref, *, which, tm):
    @pl.when(pl.program_id(1) == 0)
    def _():
        _modulate_into(h_ref, x_ref, mod_ref, g_ref, which, tm)

    h = h_ref[...]
    gate = jnp.dot(h, wg_ref[...].astype(BF16), preferred_element_type=F32)
    up = jnp.dot(h, wu_ref[...].astype(BF16), preferred_element_type=F32)
    o_ref[...] = (gate * jax.nn.sigmoid(gate) * up).astype(BF16)


def _ffn_in(x, mods, g, w_in, lead, which, tm=1024, tn=512):
    nj = D_FF // tn
    n_lead = len(lead)
    return pl.pallas_call(
        functools.partial(_ffn_in_kernel, which=which, tm=tm),
        grid=(T // tm, nj),
        in_specs=[
            pl.BlockSpec((tm, D), lambda i, j: (i, 0)),
            pl.BlockSpec((1, N_MOD, D), lambda i, j: (_group_of_tile(i, tm), 0, 0)),
            pl.BlockSpec((1, D), lambda i, j: (0, 0)),
            pl.BlockSpec((None,) * n_lead + (D, tn), lambda i, j: lead + (0, j)),
            pl.BlockSpec((None,) * n_lead + (D, tn), lambda i, j: lead + (0, j + nj)),
        ],
        out_specs=pl.BlockSpec((tm, tn), lambda i, j: (i, j)),
        out_shape=jax.ShapeDtypeStruct((T, D_FF), BF16),
        scratch_shapes=[pltpu.VMEM((tm, D), BF16)],
        compiler_params=_cparams(2),
        name="ffn_in",
    )(x, mods, g, w_in, w_in)


def _mod_proj_kernel(x_ref, mod_ref, g_ref, w_ref, o_ref, h_ref, *, which, tm):
    @pl.when(pl.program_id(1) == 0)
    def _():
        _modulate_into(h_ref, x_ref, mod_ref, g_ref, which, tm)

    o_ref[...] = jnp.dot(h_ref[...], w_ref[...], preferred_element_type=F32)


def _mod_proj(x, mods, g, w, which, tm, tn):
    n = w.shape[1]
    return pl.pallas_call(
        functools.partial(_mod_proj_kernel, which=which, tm=tm),
        grid=(T // tm, n // tn),
        in_specs=[
            pl.BlockSpec((tm, D), lambda i, j: (i, 0)),
            pl.BlockSpec((1, N_MOD, D), lambda i, j: (_group_of_tile(i, tm), 0, 0)),
            pl.BlockSpec((1, D), lambda i, j: (0, 0)),
            pl.BlockSpec((D, tn), lambda i, j: (0, j)),
        ],
        out_specs=pl.BlockSpec((tm, tn), lambda i, j: (i, j)),
        out_shape=jax.ShapeDtypeStruct((T, n), F32),
        scratch_shapes=[pltpu.VMEM((tm, D), BF16)],
        compiler_params=_cparams(2),
        name="mixer_in_proj",
    )(x, mods, g, w)


def _mm_residual_kernel(a_ref, w_ref, x_ref, mod_ref, o_ref, *, gate_row, coef):
    acc = jnp.dot(a_ref[...], w_ref[...], preferred_element_type=F32)
    gate = mod_ref[0, gate_row:gate_row + 1, :]
    o_ref[...] = x_ref[...] + (coef * gate) * acc


def _mm_residual(a, w, lead, x, mods, gate_row, coef, tm, tn, name, tile0=0, n_tiles=T // 1024):
    k = a.shape[1]
    return pl.pallas_call(
        functools.partial(_mm_residual_kernel, gate_row=gate_row, coef=coef),
        grid=(n_tiles, D // tn),
        in_specs=[
            pl.BlockSpec((tm, k), lambda i, j: (tile0 + i, 0)),
            _weight_spec(lead, k, tn),
            pl.BlockSpec((tm, tn), lambda i, j: (tile0 + i, j)),
            pl.BlockSpec((1, N_MOD, tn), lambda i, j: (_group_of_tile(tile0 + i, tm), 0, j)),
        ],
        out_specs=pl.BlockSpec((tm, tn), lambda i, j: (i, j)),
        out_shape=jax.ShapeDtypeStruct((n_tiles * tm, D), F32),
        compiler_params=_cparams(2),
        name=name,
    )(a, w, x, mods)


def _rope(x, cos, sin, nf):
    lane = lax.broadcasted_iota(jnp.int32, x.shape, 1)
    first = (lane & (2 * nf - 1)) < nf
    partner = jnp.where(first, pltpu.roll(x, 128 - nf, 1), pltpu.roll(x, nf, 1))
    return x * cos + partner * sin


def _rms(x, g, n):
    inv = lax.rsqrt(jnp.sum(x * x, axis=-1, keepdims=True) * (1.0 / n) + EPS)
    return x * inv * g


def _mla_keys_values(ckvn, kr, wk_ref, wv_ref, gkn_ref, gkr_ref, cos, sin, km_ref, vm_ref):
    c16 = ckvn.astype(BF16)
    kn = jnp.dot(c16, wk_ref[...], preferred_element_type=F32)
    vm_ref[...] = jnp.dot(c16, wv_ref[...], preferred_element_type=F32).astype(BF16)
    kr_ss = jnp.sum(kr * kr, axis=-1, keepdims=True)
    for h in range(HA):
        x = kn[:, h * NOPE:(h + 1) * NOPE]
        inv = lax.rsqrt((jnp.sum(x * x, axis=-1, keepdims=True) + kr_ss) * (1.0 / QK) + EPS)
        km_ref[:, h * QK_PAD:h * QK_PAD + NOPE] = (x * inv * gkn_ref[...]).astype(BF16)
        r = kr * inv * gkr_ref[...]
        if cos is not None:
            r = _rope(r, cos, sin, ROPE // 4)
        km_ref[:, h * QK_PAD + NOPE:(h + 1) * QK_PAD] = r.astype(BF16)


def _even_prep_kernel(p_ref, wq_ref, wk_ref, wv_ref, qn_ref, kvn_ref, gq_ref, gkn_ref, gkr_ref,
                      gnaq_ref, gnak_ref, cos_ref, sin_ref,
                      qm_ref, km_ref, vm_ref, qna_ref, kna_ref, vna_ref,
                      ckv_ref, kr_ref, knaf_ref, vnaf_ref, *, n_ctx_tiles):
    is_ctx = pl.program_id(0) < n_ctx_tiles
    cos = cos_ref[...]
    sin = sin_ref[...]
    cq = _rms(p_ref[:, 0:Q_LORA], qn_ref[...], Q_LORA)
    q = jnp.dot(cq.astype(BF16), wq_ref[...], preferred_element_type=F32)
    for h in range(HA):
        y = _rms(q[:, h * QK_PAD:(h + 1) * QK_PAD], gq_ref[...], QK)
        qm_ref[:, h * QK_PAD:h * QK_PAD + NOPE] = y[:, :NOPE].astype(BF16)
        qm_ref[:, h * QK_PAD + NOPE:(h + 1) * QK_PAD] = _rope(y[:, NOPE:], cos, sin, ROPE // 4).astype(BF16)
    ckvn = _rms(p_ref[:, Q_LORA:Q_LORA + KV_LORA], kvn_ref[...], KV_LORA)
    kr = p_ref[:, IN_EVEN_PAD - 128:IN_EVEN_PAD]
    _mla_keys_values(ckvn, kr, wk_ref, wv_ref, gkn_ref, gkr_ref, cos, sin, km_ref, vm_ref)
    base = Q_LORA + KV_LORA
    vn = p_ref[:, base + 2 * HB * HEAD:base + 3 * HB * HEAD]
    vna_ref[...] = vn.astype(BF16)

    @pl.when(is_ctx)
    def _():
        ckv_ref[...] = ckvn
        kr_ref[...] = kr[:, :ROPE]
        vnaf_ref[...] = vn

    for h in range(HB):
        qh = _rms(p_ref[:, base + h * HEAD:base + (h + 1) * HEAD], gnaq_ref[...], HEAD)
        qna_ref[:, h * HEAD:(h + 1) * HEAD] = qh.astype(BF16)
        kh = _rms(p_ref[:, base + (HB + h) * HEAD:base + (HB + h + 1) * HEAD], gnak_ref[...], HEAD)
        kna_ref[:, h * HEAD:(h + 1) * HEAD] = kh.astype(BF16)

        @pl.when(is_ctx)
        def _(kh=kh, h=h):
            knaf_ref[:, h * HEAD:(h + 1) * HEAD] = kh


def _even_prep(p, wq, wk, wv, qn, kvn, gq, gkn, gkr, gnaq, gnak, cos, sin, tm=256):
    n_ctx = T_CTX // tm
    full = lambda shape: pl.BlockSpec(shape, lambda i: (0, 0))
    rows = lambda w: pl.BlockSpec((tm, w), lambda i: (i, 0))
    ctx_rows = lambda w: pl.BlockSpec((tm, w), lambda i: (jnp.minimum(i, n_ctx - 1), 0))
    table = pl.BlockSpec((tm, 128), lambda i: (_rope_block_of_tile(i, tm), 0))
    tok = lambda w, dt: jax.ShapeDtypeStruct((T, w), dt)
    ctx = lambda w: jax.ShapeDtypeStruct((T_CTX, w), F32)
    return pl.pallas_call(
        functools.partial(_even_prep_kernel, n_ctx_tiles=n_ctx),
        grid=(T // tm,),
        in_specs=[rows(IN_EVEN_PAD), full(wq.shape), full(wk.shape), full(wv.shape), full(qn.shape),
                  full(kvn.shape), full(gq.shape), full(gkn.shape), full(gkr.shape), full(gnaq.shape),
                  full(gnak.shape), table, table],
        out_specs=[rows(HA * QK_PAD), rows(HA * QK_PAD), rows(HA * HEAD), rows(HB * HEAD), rows(HB * HEAD),
                   rows(HB * HEAD), ctx_rows(KV_LORA), ctx_rows(ROPE), ctx_rows(HB * HEAD), ctx_rows(HB * HEAD)],
        out_shape=[tok(HA * QK_PAD, BF16), tok(HA * QK_PAD, BF16), tok(HA * HEAD, BF16), tok(HB * HEAD, BF16),
                   tok(HB * HEAD, BF16), tok(HB * HEAD, BF16), ctx(KV_LORA), ctx(ROPE), ctx(HB * HEAD),
                   ctx(HB * HEAD)],
        compiler_params=_cparams(1),
        name="even_prep",
    )(p, wq, wk, wv, qn, kvn, gq, gkn, gkr, gnaq, gnak, cos, sin)


def _cache_kv_kernel(ckv_ref, kr_ref, wk_ref, wv_ref, gkn_ref, gkr_ref, km_ref, vm_ref):
    _mla_keys_values(ckv_ref[...], kr_ref[...], wk_ref, wv_ref, gkn_ref, gkr_ref, None, None, km_ref, vm_ref)


def _cache_kv(ckv, kr, wk, wv, gkn, gkr, tm=512):
    n = ckv.shape[0]
    full = lambda shape: pl.BlockSpec(shape, lambda i: (0, 0))
    rows = lambda w: pl.BlockSpec((tm, w), lambda i: (i, 0))
    return pl.pallas_call(
        _cache_kv_kernel,
        grid=(n // tm,),
        in_specs=[rows(KV_LORA), rows(128), full(wk.shape), full(wv.shape), full(gkn.shape), full(gkr.shape)],
        out_specs=[rows(HA * QK_PAD), rows(HA * HEAD)],
        out_shape=[jax.ShapeDtypeStruct((n, HA * QK_PAD), BF16), jax.ShapeDtypeStruct((n, HA * HEAD), BF16)],
        compiler_params=_cparams(1),
        name="mla_cache_kv",
    )(ckv, kr, wk, wv, gkn, gkr)


def _odd_prep_kernel(p_ref, gq_ref, gk_ref, cos_ref, sin_ref, q_ref, k_ref, v_ref, kf_ref, vf_ref, *, n_ctx_tiles):
    is_ctx = pl.program_id(0) < n_ctx_tiles
    cos = cos_ref[...]
    sin = sin_ref[...]
    for h in range(HC):
        y = _rms(p_ref[:, h * HEAD:(h + 1) * HEAD], gq_ref[...], HEAD)
        q_ref[:, h * HEAD:(h + 1) * HEAD] = _rope(y, cos, sin, HEAD // 4).astype(BF16)
    base = HC * HEAD
    for h in range(KVH_C):
        y = _rms(p_ref[:, base + h * HEAD:base + (h + 1) * HEAD], gk_ref[...], HEAD)
        k_ref[:, h * HEAD:(h + 1) * HEAD] = _rope(y, cos, sin, HEAD // 4).astype(BF16)

        @pl.when(is_ctx)
        def _(y=y, h=h):
            kf_ref[:, h * HEAD:(h + 1) * HEAD] = y

    v = p_ref[:, base + KVH_C * HEAD:base + 2 * KVH_C * HEAD]
    v_ref[...] = v.astype(BF16)

    @pl.when(is_ctx)
    def _():
        vf_ref[...] = v


def _odd_prep(p, gq, gk, cos, sin, tm=512):
    n_ctx = T_CTX // tm
    full = lambda shape: pl.BlockSpec(shape, lambda i: (0, 0))
    rows = lambda w: pl.BlockSpec((tm, w), lambda i: (i, 0))
    ctx_rows = lambda w: pl.BlockSpec((tm, w), lambda i: (jnp.minimum(i, n_ctx - 1), 0))
    table = pl.BlockSpec((tm, 128), lambda i: (_rope_block_of_tile(i, tm), 0))
    tok = lambda w, dt: jax.ShapeDtypeStruct((T, w), dt)
    ctx = lambda w: jax.ShapeDtypeStruct((T_CTX, w), F32)
    return pl.pallas_call(
        functools.partial(_odd_prep_kernel, n_ctx_tiles=n_ctx),
        grid=(T // tm,),
        in_specs=[rows(IN_ODD), full(gq.shape), full(gk.shape), table, table],
        out_specs=[rows(HC * HEAD), rows(KVH_C * HEAD), rows(KVH_C * HEAD), ctx_rows(KVH_C * HEAD),
                   ctx_rows(KVH_C * HEAD)],
        out_shape=[tok(HC * HEAD, BF16), tok(KVH_C * HEAD, BF16), tok(KVH_C * HEAD, BF16), ctx(KVH_C * HEAD),
                   ctx(KVH_C * HEAD)],
        compiler_params=_cparams(1),
        name="odd_prep",
    )(p, gq, gk, cos, sin)


def _build_na_bias(toep_ref, bias_scr):
    neg = jnp.full((GRID_W, GRID_W), NEG_INF, F32)
    for r in range(GRID_H):
        rs = min(max(r - NA_KH // 2, 0), GRID_H - NA_KH)
        for kr in range(GRID_H):
            tile = toep_ref[0, kr - r + NA_KH - 1] if rs <= kr < rs + NA_KH else neg
            bias_scr[r * GRID_W:(r + 1) * GRID_W, kr * GRID_W:(kr + 1) * GRID_W] = tile


def _attn_kernel(*refs, plan, n_kv, bias_mode, has_sink, has_prev, hps, groups, dk, dv, qb, c):
    it = iter(refs)
    q_ref = next(it)
    kv_refs = [(next(it), next(it)) for _ in range(n_kv)]
    bias_ref = next(it) if bias_mode is not None else None
    sink_ref = next(it) if has_sink else None
    if has_prev:
        next(it)
    o_ref = next(it)
    bias_scr = next(it) if bias_mode == "na" else None
    hblk = pl.program_id(0)

    if bias_mode == "na":
        @pl.when(pl.program_id(1) == 0)
        def _():
            _build_na_bias(bias_ref, bias_scr)

    for hh in range(hps):
        for g in range(groups):
            head = hh * groups + g
            sink = sink_ref[(hblk * hps + hh) * groups + g] if has_sink else None
            for qi, segs in enumerate(plan):
                q0 = qi * qb
                q = q_ref[q0:q0 + qb, head * dk:(head + 1) * dk]
                scores = []
                for (si, start, length, biased) in segs:
                    k = kv_refs[si][0][start:start + length, hh * dk:(hh + 1) * dk].astype(BF16)
                    t = lax.dot_general(q, k, _NT, preferred_element_type=F32) * c
                    if biased and bias_mode == "na":
                        t = t + bias_scr[q0:q0 + qb, start:start + length]
                    elif biased:
                        t = t + bias_ref[qi, :, :length]
                    scores.append(t)
                m = jnp.max(scores[0], axis=-1, keepdims=True)
                for t in scores[1:]:
                    m = jnp.maximum(m, jnp.max(t, axis=-1, keepdims=True))
                if has_sink:
                    m = jnp.maximum(m, sink)
                denom = jnp.exp2(sink - m) if has_sink else None
                acc = None
                for t, (si, start, length, _) in zip(scores, segs):
                    p = jnp.exp2(t - m)
                    part = jnp.sum(p, axis=-1, keepdims=True)
                    v = kv_refs[si][1][start:start + length, hh * dv:(hh + 1) * dv].astype(BF16)
                    pv = jnp.dot(p.astype(BF16), v, preferred_element_type=F32)
                    denom = part if denom is None else denom + part
                    acc = pv if acc is None else acc + pv
                o_ref[q0:q0 + qb, head * dv:(head + 1) * dv] = (acc / denom).astype(o_ref.dtype)


def _attention(q, kvs, o_prev, plan, *, nb, sq, q_rowblk0, n_kv_heads, hps, groups, dk, dv, scale,
               o_colblk0, bias=None, bias_mode=None, sink=None, name="attention"):
    grid = (n_kv_heads // hps, nb)
    in_specs = [pl.BlockSpec((sq, hps * groups * dk), lambda h, b: (q_rowblk0 + b, h))]
    args = [q]
    for (k, v, sk, r0) in kvs:
        in_specs.append(pl.BlockSpec((sk, hps * dk), lambda h, b, r0=r0: (r0 + b, h)))
        in_specs.append(pl.BlockSpec((sk, hps * dv), lambda h, b, r0=r0: (r0 + b, h)))
        args += [k, v]
    scratch = []
    if bias_mode == "na":
        in_specs.append(pl.BlockSpec((1,) + bias.shape[1:], lambda h, b: (h, 0, 0, 0)))
        args.append(bias)
        scratch.append(pltpu.VMEM((LAT_S, LAT_S), F32))
    elif bias_mode == "table":
        in_specs.append(pl.BlockSpec(bias.shape, lambda h, b: (0, 0, 0)))
        args.append(bias)
    if sink is not None:
        in_specs.append(pl.BlockSpec(memory_space=pltpu.SMEM))
        args.append(sink)
    aliases = {}
    if o_prev is not None:
        aliases = {len(args): 0}
        in_specs.append(pl.BlockSpec(memory_space=pl.ANY))
        args.append(o_prev)
    body = functools.partial(
        _attn_kernel, plan=plan, n_kv=len(kvs), bias_mode=bias_mode, has_sink=sink is not None,
        has_prev=o_prev is not None, hps=hps, groups=groups, dk=dk, dv=dv, qb=min(Q_BLOCK, sq), c=scale * LOG2E)
    return pl.pallas_call(
        body,
        grid=grid,
        in_specs=in_specs,
        out_specs=pl.BlockSpec((sq, hps * groups * dv), lambda h, b: (q_rowblk0 + b, o_colblk0 + h)),
        out_shape=jax.ShapeDtypeStruct((T, D), BF16),
        scratch_shapes=scratch,
        input_output_aliases=aliases,
        compiler_params=_cparams(2),
        name=name,
    )(*args)


_CTX_PLAN = [[(0, 0, CTX_S, False)]]
_N_QB = LAT_S // Q_BLOCK
_MLA_PLAN = [[(0, 0, LAT_S, False), (1, 0, PAST, False)] for _ in range(_N_QB)]
_NA_ROWS = [(0, 8), (0, 12), (4, 16), (8, 16)]
_NA_PLAN = [[(0, lo * GRID_W, (hi - lo) * GRID_W, True), (1, 0, PAST, False)] for lo, hi in _NA_ROWS]
_WIN_SPANS = [(max(i * Q_BLOCK - WINDOW, 0), min((i + 1) * Q_BLOCK + WINDOW, LAT_S)) for i in range(_N_QB)]
_WIN_PLAN = [[(0, lo, hi - lo, True), (1, 0, PAST, False)] for lo, hi in _WIN_SPANS]


def _window_bias():
    width = max(hi - lo for lo, hi in _WIN_SPANS)
    out = np.full((_N_QB, Q_BLOCK, width), NEG_INF, np.float32)
    for i, (lo, hi) in enumerate(_WIN_SPANS):
        q_abs = i * Q_BLOCK + np.arange(Q_BLOCK)[:, None]
        k_abs = lo + np.arange(hi - lo)[None, :]
        out[i, :, :hi - lo] = np.where(np.abs(q_abs - k_abs) <= WINDOW, 0.0, NEG_INF)
    return jnp.asarray(out)


def _rope_tables(rot_dim):
    half = rot_dim // 2
    nf = half // 2
    t = np.arange(LAT_S)
    inv_freq = ROPE_THETA ** (-np.arange(nf, dtype=np.float64) / nf)
    cos = np.zeros((2 * LAT_S, 128), np.float64)
    sin = np.zeros((2 * LAT_S, 128), np.float64)
    cos[:LAT_S, :rot_dim] = 1.0
    for part, pos in enumerate((t // GRID_W, t % GRID_W)):
        ang = pos[:, None].astype(np.float64) * inv_freq[None, :]
        lo = part * half
        cos[LAT_S:, lo:lo + nf] = np.cos(ang)
        cos[LAT_S:, lo + nf:lo + half] = np.cos(ang)
        sin[LAT_S:, lo:lo + nf] = -np.sin(ang)
        sin[LAT_S:, lo + nf:lo + half] = np.sin(ang)
    return jnp.asarray(cos, F32), jnp.asarray(sin, F32)


def _na_toeplitz(rpb):
    c = np.arange(GRID_W)
    ws = np.clip(c - NA_KW // 2, 0, GRID_W - NA_KW)
    col_valid = (c[None, :] >= ws[:, None]) & (c[None, :] < ws[:, None] + NA_KW)
    col_off = np.clip(c[None, :] - c[:, None] + NA_KW - 1, 0, 2 * NA_KW - 2)
    t = rpb[:, :, col_off] * LOG2E
    return jnp.where(jnp.asarray(col_valid)[None, None], t, NEG_INF)


def _even_mixer(x, mods, g, w, caches):
    cache_ckv, cache_krope, cache_nak, cache_nav = caches
    p = _mod_proj(x, mods, g, w["w_in"], 1, tm=1024, tn=IN_EVEN_PAD // 3)
    cos, sin = _rope_tables(ROPE)
    qm, km, vm, qna, kna, vna, new_ckv, new_kr, new_nak, new_nav = _even_prep(
        p, w["wq"], w["wk"], w["wv"], w["q_norm"], w["kv_norm"], w["gq"], w["gkn"], w["gkr"],
        w["gnaq"], w["gnak"], cos, sin)
    km_ctx, vm_ctx = _cache_kv(cache_ckv, cache_krope, w["wk"], w["wv"], w["gkn"], w["gkr"])
    s_mla = 1.0 / math.sqrt(QK)
    s_na = 1.0 / math.sqrt(HEAD)
    lat0 = T_CTX // LAT_S
    o = _attention(qm, [(km, vm, CTX_S, 0)], None, _CTX_PLAN, nb=CTX_B, sq=CTX_S, q_rowblk0=0, n_kv_heads=HA,
                   hps=HA, groups=1, dk=QK_PAD, dv=HEAD, scale=s_mla, o_colblk0=0, name="mla_ctx")
    o = _attention(qna, [(kna, vna, CTX_S, 0)], o, _CTX_PLAN, nb=CTX_B, sq=CTX_S, q_rowblk0=0, n_kv_heads=HB,
                   hps=HB, groups=1, dk=HEAD, dv=HEAD, scale=s_na, o_colblk0=1, name="na_ctx")
    o = _attention(qm, [(km, vm, LAT_S, lat0), (km_ctx, vm_ctx, PAST, 0)], o, _MLA_PLAN, nb=LAT_B, sq=LAT_S,
                   q_rowblk0=lat0, n_kv_heads=HA, hps=1, groups=1, dk=QK_PAD, dv=HEAD, scale=s_mla,
                   o_colblk0=0, name="mla_lat")
    o = _attention(qna, [(kna, vna, LAT_S, lat0), (cache_nak, cache_nav, PAST, 0)], o, _NA_PLAN, nb=LAT_B,
                   sq=LAT_S, q_rowblk0=lat0, n_kv_heads=HB, hps=1, groups=1, dk=HEAD, dv=HEAD, scale=s_na,
                   o_colblk0=HA, bias=w["na_toep"], bias_mode="na", name="na_lat")
    x = _mm_residual(o, w["w_out"], w["w_out_lead"], x, mods, 5, 1.0, tm=1024, tn=1024, name="mixer_out_proj")
    return x, (new_ckv.reshape(CTX_B, CTX_S, KV_LORA), new_kr.reshape(CTX_B, CTX_S, ROPE),
               new_nak.reshape(CTX_B, CTX_S, HB, HEAD), new_nav.reshape(CTX_B, CTX_S, HB, HEAD))


def _odd_mixer(x, mods, g, w, caches):
    cache_k, cache_v = caches
    p = _mod_proj(x, mods, g, w["w_in"], 1, tm=1024, tn=1024)
    cos, sin = _rope_tables(HEAD)
    q, k, v, new_k, new_v = _odd_prep(p, w["gq"], w["gk"], cos, sin)
    scale = 1.0 / math.sqrt(HEAD)
    lat0 = T_CTX // LAT_S
    o = _attention(q, [(k, v, CTX_S, 0)], None, _CTX_PLAN, nb=CTX_B, sq=CTX_S, q_rowblk0=0, n_kv_heads=KVH_C,
                   hps=KVH_C, groups=GROUPS_C, dk=HEAD, dv=HEAD, scale=scale, o_colblk0=0, sink=w["sink"],
                   name="gqa_ctx")
    o = _attention(q, [(k, v, LAT_S, lat0), (cache_k, cache_v, PAST, 0)], o, _WIN_PLAN, nb=LAT_B, sq=LAT_S,
                   q_rowblk0=lat0, n_kv_heads=KVH_C, hps=1, groups=GROUPS_C, dk=HEAD, dv=HEAD, scale=scale,
                   o_colblk0=0, bias=_window_bias(), bias_mode="table", sink=w["sink"], name="gqa_lat")
    x = _mm_residual(o, w["w_out"], w["w_out_lead"], x, mods, 5, 1.0, tm=1024, tn=1024, name="mixer_out_proj")
    return x, (new_k.reshape(CTX_B, CTX_S, KVH_C, HEAD), new_v.reshape(CTX_B, CTX_S, KVH_C, HEAD))


def _even_weights(e, even_w_in, w_out_bf16, mla_q_norm, mla_w_q_up, mla_kv_norm, mla_w_kv_up, mla_qk_norm,
                  na_qk_norm, na_rpb):
    w_in = even_w_in[e]
    i0, i1, i2 = Q_LORA, Q_LORA + KV_LORA, Q_LORA + KV_LORA + ROPE
    w_in = jnp.concatenate([w_in[:, :i1], w_in[:, i2:], w_in[:, i1:i2], jnp.zeros((D, 128 - ROPE), F32)], axis=1)
    wq = jnp.pad(mla_w_q_up[e].reshape(Q_LORA, HA, QK), ((0, 0), (0, 0), (0, QK_PAD - QK)))
    wkv = mla_w_kv_up[e].reshape(KV_LORA, HA, NOPE + HEAD)
    qk = mla_qk_norm[e]
    return {
        "w_in": w_in.astype(BF16),
        "w_out": w_out_bf16,
        "w_out_lead": (e,),
        "wq": wq.reshape(Q_LORA, HA * QK_PAD).astype(BF16),
        "wk": wkv[:, :, :NOPE].reshape(KV_LORA, HA * NOPE).astype(BF16),
        "wv": wkv[:, :, NOPE:].reshape(KV_LORA, HA * HEAD).astype(BF16),
        "q_norm": mla_q_norm[e][None, :],
        "kv_norm": mla_kv_norm[e][None, :],
        "gq": jnp.pad(qk[0], (0, QK_PAD - QK))[None, :],
        "gkn": qk[1, :NOPE][None, :],
        "gkr": jnp.pad(qk[1, NOPE:], (0, 128 - ROPE))[None, :],
        "gnaq": na_qk_norm[e, 0][None, :],
        "gnak": na_qk_norm[e, 1][None, :],
        "na_toep": _na_toeplitz(na_rpb[e]),
    }


def kernel(x_prompt, x_sample, cache_mla_ckv, cache_mla_krope, cache_na_k, cache_na_v, cache_gqa_k, cache_gqa_v, c, c_ctx, ada_w, ada_b, norm_g, ffn_w_in, ffn_w_out, even_w_in, even_w_out, mla_q_norm, mla_w_q_up, mla_kv_norm, mla_w_kv_up, mla_qk_norm, na_qk_norm, na_rpb, odd_w_in, odd_w_out, gqa_qk_norm, gqa_sink):
    depth = ada_w.shape[0]
    x = jnp.concatenate([x_prompt.reshape(T_CTX, D), x_sample.reshape(T_LAT, D)], axis=0)
    cond = jnp.concatenate([c_ctx[None, :], c, jnp.zeros((16 - N_GROUPS, D), F32)], axis=0)
    mods_all = _ada_modulation(cond, ada_w, ada_b)
    ffn_w_out16 = ffn_w_out.astype(BF16)
    even_w_out16 = even_w_out.astype(BF16)
    odd_w_out16 = odd_w_out.astype(BF16)
    n_ctx_tiles = T_CTX // 1024

    ckv_l, kr_l, nak_l, nav_l, gk_l, gv_l = [], [], [], [], [], []
    y_prompt = y_sample = None
    for layer in range(depth):
        mods = mods_all[layer, :N_GROUPS].reshape(N_GROUPS, N_MOD, D)
        g = norm_g[layer]
        e = layer // 2
        act = _ffn_in(x, mods, g[0:1], ffn_w_in, (layer, 0), 0)
        x = _mm_residual(act, ffn_w_out16, (layer, 0), x, mods, 2, 0.5, tm=1024, tn=512, name="ffn_out")
        if layer % 2 == 0:
            w = _even_weights(e, even_w_in, even_w_out16, mla_q_norm, mla_w_q_up, mla_kv_norm, mla_w_kv_up,
                              mla_qk_norm, na_qk_norm, na_rpb)
            caches = (cache_mla_ckv[:, e].reshape(LAT_B * PAST, KV_LORA),
                      jnp.pad(cache_mla_krope[:, e].reshape(LAT_B * PAST, ROPE), ((0, 0), (0, 128 - ROPE))),
                      cache_na_k[:, e].reshape(LAT_B * PAST, HB * HEAD),
                      cache_na_v[:, e].reshape(LAT_B * PAST, HB * HEAD))
            x, (ckv, kr, nak, nav) = _even_mixer(x, mods, g[1:2], w, caches)
            ckv_l.append(ckv)
            kr_l.append(kr)
            nak_l.append(nak)
            nav_l.append(nav)
        else:
            w = {"w_in": odd_w_in[e].astype(BF16), "w_out": odd_w_out16, "w_out_lead": (e,),
                 "gq": gqa_qk_norm[e, 0][None, :], "gk": gqa_qk_norm[e, 1][None, :],
                 "sink": gqa_sink[e] * LOG2E}
            caches = (cache_gqa_k[:, e].reshape(LAT_B * PAST, KVH_C * HEAD),
                      cache_gqa_v[:, e].reshape(LAT_B * PAST, KVH_C * HEAD))
            x, (gk, gv) = _odd_mixer(x, mods, g[1:2], w, caches)
            gk_l.append(gk)
            gv_l.append(gv)
        act = _ffn_in(x, mods, g[2:3], ffn_w_in, (layer, 1), 2)
        ffn_out = functools.partial(_mm_residual, act, ffn_w_out16, (layer, 1), x, mods, 8, 0.5, tm=1024, tn=512,
                                    name="ffn_out")
        if layer + 1 < depth:
            x = ffn_out()
        else:
            y_prompt = ffn_out(tile0=0, n_tiles=n_ctx_tiles).reshape(CTX_B, CTX_S, D)
            y_sample = ffn_out(tile0=n_ctx_tiles, n_tiles=T // 1024 - n_ctx_tiles).reshape(LAT_B, LAT_S, D)

    return (y_prompt, y_sample, jnp.stack(ckv_l, axis=1), jnp.stack(kr_l, axis=1), jnp.stack(nak_l, axis=1),
            jnp.stack(nav_l, axis=1), jnp.stack(gk_l, axis=1), jnp.stack(gv_l, axis=1))
```

```python
import functools
import math

import numpy as np
import jax
import jax.numpy as jnp
from jax import lax
from jax.experimental import pallas as pl
from jax.experimental.pallas import tpu as pltpu

F32 = jnp.float32
BF16 = jnp.bfloat16

D = 2048
D_FF = 5632
N_MOD = 9
CTX_B, CTX_S = 16, 256
LAT_B, LAT_S = 8, 1024
PAST = 512
T_CTX = CTX_B * CTX_S
T_LAT = LAT_B * LAT_S
T = T_CTX + T_LAT
GRID_W = 64
GRID_H = LAT_S // GRID_W
N_GROUPS = 1 + LAT_B
HEAD = 128
HA = 8
NOPE, ROPE = 128, 64
QK = NOPE + ROPE
QK_PAD = 256
Q_LORA = 512
KV_LORA = 512
HB = 8
NA_KH, NA_KW = 8, 16
HC, KVH_C = 16, 4
GROUPS_C = HC // KVH_C
WINDOW = 128
IN_EVEN_PAD = 4224
IN_ODD = 3072
ROPE_THETA = 10000.0
EPS = 1e-6
NEG_INF = -1e30
LOG2E = math.log2(math.e)

VMEM_LIMIT = 56 * 1024 * 1024
Q_BLOCK = 256
M_CHUNK = 256

_NT = (((1,), (1,)), ((), ()))


def _cparams(n_axes):
    return pltpu.CompilerParams(dimension_semantics=("arbitrary",) * n_axes,
                                vmem_limit_bytes=VMEM_LIMIT)


def _group_of_tile(i, tm):
    n_ctx = T_CTX // tm
    per = LAT_S // tm
    return jnp.where(i < n_ctx, 0, (i - n_ctx) // per + 1)


def _rope_block_of_tile(i, tm):
    n_ctx = T_CTX // tm
    per = LAT_S // tm
    return jnp.where(i < n_ctx, 0, per + (i - n_ctx) % per)


def _weight_spec(lead, rows, tn):
    return pl.BlockSpec((None,) * len(lead) + (rows, tn), lambda i, j: lead + (0, j))


def _ada_kernel(c_ref, w_ref, b_ref, o_ref):
    c = c_ref[...]
    a = (c * jax.nn.sigmoid(c)).astype(BF16)
    o_ref[0] = jnp.dot(a, w_ref[0].astype(BF16), preferred_element_type=F32) + b_ref[0]


def _ada_modulation(cond, ada_w, ada_b, tn=1024):
    depth = ada_w.shape[0]
    n = N_MOD * D
    rows = cond.shape[0]
    return pl.pallas_call(
        _ada_kernel,
        grid=(depth, n // tn),
        in_specs=[
            pl.BlockSpec((rows, D), lambda l, j: (0, 0)),
            pl.BlockSpec((1, D, tn), lambda l, j: (l, 0, j)),
            pl.BlockSpec((1, 1, tn), lambda l, j: (l, 0, j)),
        ],
        out_specs=pl.BlockSpec((1, rows, tn), lambda l, j: (l, 0, j)),
        out_shape=jax.ShapeDtypeStruct((depth, rows, n), F32),
        compiler_params=_cparams(2),
        name="ada_modulation",
    )(cond, ada_w, ada_b.reshape(depth, 1, n))


def _mod_rows(mod_ref, g_ref, which):
    shift = mod_ref[0, 3 * which:3 * which + 1, :]
    scale = mod_ref[0, 3 * which + 1:3 * which + 2, :]
    return g_ref[...] * (1.0 + scale), shift


def _modulate(x, gain, shift):
    inv = lax.rsqrt(jnp.mean(x * x, axis=-1, keepdims=True) + EPS)
    return ((x * inv) * gain + shift).astype(BF16)


def _ffn_in_kernel(x_ref, mod_ref, g_ref, wg_ref, wu_ref, o_ref, h_ref, *, which, tm):
    def chunks(first):
        wg = wg_ref[...].astype(BF16)
        wu = wu_ref[...].astype(BF16)
        if first:
            gain, shift = _mod_rows(mod_ref, g_ref, which)
        for r in range(0, tm, M_CHUNK):
            if first:
                h = _modulate(x_ref[r:r + M_CHUNK, :], gain, shift)
                h_ref[r:r + M_CHUNK, :] = h
            else:
                h = h_ref[r:r + M_CHUNK, :]
            gate = jnp.dot(h, wg, preferred_element_type=F32)
            up = jnp.dot(h, wu, preferred_element_type=F32)
            o_ref[r:r + M_CHUNK, :] = (gate * jax.nn.sigmoid(gate) * up).astype(BF16)

    @pl.when(pl.program_id(1) == 0)
    def _():
        chunks(True)

    @pl.when(pl.program_id(1) != 0)
    def _():
        chunks(False)


def _ffn_in(x, mods, g, w_in, lead, which, tm=1024, tn=512):
    nj = D_FF // tn
    n_lead = len(lead)
    return pl.pallas_call(
        functools.partial(_ffn_in_kernel, which=which, tm=tm),
        grid=(T // tm, nj),
        in_specs=[
            pl.BlockSpec((tm, D), lambda i, j: (i, 0)),
            pl.BlockSpec((1, N_MOD, D), lambda i, j: (_group_of_tile(i, tm), 0, 0)),
            pl.BlockSpec((1, D), lambda i, j: (0, 0)),
            pl.BlockSpec((None,) * n_lead + (D, tn), lambda i, j: lead + (0, j)),
            pl.BlockSpec((None,) * n_lead + (D, tn), lambda i, j: lead + (0, j + nj)),
        ],
        out_specs=pl.BlockSpec((tm, tn), lambda i, j: (i, j)),
        out_shape=jax.ShapeDtypeStruct((T, D_FF), BF16),
        scratch_shapes=[pltpu.VMEM((tm, D), BF16)],
        compiler_params=_cparams(2),
        name="ffn_in",
    )(x, mods, g, w_in, w_in)


def _mm_residual_kernel(a_ref, w_ref, x_ref, mod_ref, o_ref, *, gate_row, coef, tm):
    gate = coef * mod_ref[0, gate_row:gate_row + 1, :]
    for r in range(0, tm, M_CHUNK):
        acc = jnp.dot(a_ref[r:r + M_CHUNK, :], w_ref[...], preferred_element_type=F32)
        o_ref[r:r + M_CHUNK, :] = x_ref[r:r + M_CHUNK, :] + gate * acc


def _mm_residual(a, w, lead, x, mods, gate_row, coef, tm, tn, name, tile0=0, n_tiles=T // 1024):
    k = a.shape[1]
    return pl.pallas_call(
        functools.partial(_mm_residual_kernel, gate_row=gate_row, coef=coef, tm=tm),
        grid=(n_tiles, D // tn),
        in_specs=[
            pl.BlockSpec((tm, k), lambda i, j: (tile0 + i, 0)),
            _weight_spec(lead, k, tn),
            pl.BlockSpec((tm, tn), lambda i, j: (tile0 + i, j)),
            pl.BlockSpec((1, N_MOD, tn), lambda i, j: (_group_of_tile(tile0 + i, tm), 0, j)),
        ],
        out_specs=pl.BlockSpec((tm, tn), lambda i, j: (i, j)),
        out_shape=jax.ShapeDtypeStruct((n_tiles * tm, D), F32),
        compiler_params=_cparams(2),
        name=name,
    )(a, w, x, mods)


def _out_proj_kernel(*refs, widths, n_ctx_tiles, tm):
    n_a = len(widths)
    ctx_refs, lat_refs = refs[:n_a], refs[n_a:2 * n_a]
    w_ref, x_ref, mod_ref, o_ref = refs[2 * n_a:]
    gate = mod_ref[0, 5:6, :]

    def run(a_refs):
        for r in range(0, tm, M_CHUNK):
            acc = None
            k0 = 0
            for a_ref, kw in zip(a_refs, widths):
                part = jnp.dot(a_ref[r:r + M_CHUNK, :], w_ref[k0:k0 + kw, :], preferred_element_type=F32)
                acc = part if acc is None else acc + part
                k0 += kw
            o_ref[r:r + M_CHUNK, :] = x_ref[r:r + M_CHUNK, :] + gate * acc

    @pl.when(pl.program_id(0) < n_ctx_tiles)
    def _():
        run(ctx_refs)

    @pl.when(pl.program_id(0) >= n_ctx_tiles)
    def _():
        run(lat_refs)


def _out_proj(a_ctx, a_lat, w, lead, x, mods, tm=512):
    widths = tuple(a.shape[1] for a in a_ctx)
    n_ctx = T_CTX // tm
    specs = [pl.BlockSpec((tm, kw), lambda i: (jnp.minimum(i, n_ctx - 1), 0)) for kw in widths]
    specs += [pl.BlockSpec((tm, kw), lambda i: (jnp.maximum(i - n_ctx, 0), 0)) for kw in widths]
    return pl.pallas_call(
        functools.partial(_out_proj_kernel, widths=widths, n_ctx_tiles=n_ctx, tm=tm),
        grid=(T // tm,),
        in_specs=specs + [
            pl.BlockSpec((None,) * len(lead) + (D, D), lambda i: lead + (0, 0), pipeline_mode=pl.Buffered(1)),
            pl.BlockSpec((tm, D), lambda i: (i, 0)),
            pl.BlockSpec((1, N_MOD, D), lambda i: (_group_of_tile(i, tm), 0, 0)),
        ],
        out_specs=pl.BlockSpec((tm, D), lambda i: (i, 0)),
        out_shape=jax.ShapeDtypeStruct((T, D), F32),
        compiler_params=_cparams(1),
        name="mixer_out_proj",
    )(*a_ctx, *a_lat, w, x, mods)


def _rope(x, cos, sin, nf):
    lane = lax.broadcasted_iota(jnp.int32, x.shape, 1)
    first = (lane & (2 * nf - 1)) < nf
    partner = jnp.where(first, pltpu.roll(x, 128 - nf, 1), pltpu.roll(x, nf, 1))
    return x * cos + partner * sin


def _rms(x, g, n):
    inv = lax.rsqrt(jnp.sum(x * x, axis=-1, keepdims=True) * (1.0 / n) + EPS)
    return x * inv * g


def _mla_keys_values(ckvn, kr, wk_ref, wv_ref, gkn_ref, gkr_ref, cos, sin, km_ref, vm_ref):
    c16 = ckvn.astype(BF16)
    kn = jnp.dot(c16, wk_ref[...], preferred_element_type=F32)
    vm_ref[...] = jnp.dot(c16, wv_ref[...], preferred_element_type=F32).astype(BF16)
    kr_ss = jnp.sum(kr * kr, axis=-1, keepdims=True)
    for h in range(HA):
        x = kn[:, h * NOPE:(h + 1) * NOPE]
        inv = lax.rsqrt((jnp.sum(x * x, axis=-1, keepdims=True) + kr_ss) * (1.0 / QK) + EPS)
        km_ref[:, h * QK_PAD:h * QK_PAD + NOPE] = (x * inv * gkn_ref[...]).astype(BF16)
        r = kr * inv * gkr_ref[...]
        if cos is not None:
            r = _rope(r, cos, sin, ROPE // 4)
        km_ref[:, h * QK_PAD + NOPE:(h + 1) * QK_PAD] = r.astype(BF16)


def _even_in_kernel(x_ref, mod_ref, g_ref, w_ref, wq_ref, wk_ref, wv_ref, qn_ref, kvn_ref, gq_ref, gkn_ref,
                    gkr_ref, gnaq_ref, gnak_ref, cos_ref, sin_ref,
                    qm_ref, km_ref, vm_ref, qna_ref, kna_ref, vna_ref,
                    ckv_ref, kr_ref, knaf_ref, vnaf_ref, *, n_ctx_tiles):
    is_ctx = pl.program_id(0) < n_ctx_tiles
    cos = cos_ref[...]
    sin = sin_ref[...]
    gain, shift = _mod_rows(mod_ref, g_ref, 1)
    h = _modulate(x_ref[...], gain, shift)
    proj = lambda lo, hi: jnp.dot(h, w_ref[:, lo:hi], preferred_element_type=F32)
    base = Q_LORA + KV_LORA

    low = proj(0, base)
    kr = proj(IN_EVEN_PAD - 128, IN_EVEN_PAD)
    cq = _rms(low[:, :Q_LORA], qn_ref[...], Q_LORA)
    q = jnp.dot(cq.astype(BF16), wq_ref[...], preferred_element_type=F32)
    for hd in range(HA):
        y = _rms(q[:, hd * QK_PAD:(hd + 1) * QK_PAD], gq_ref[...], QK)
        qm_ref[:, hd * QK_PAD:hd * QK_PAD + NOPE] = y[:, :NOPE].astype(BF16)
        qm_ref[:, hd * QK_PAD + NOPE:(hd + 1) * QK_PAD] = _rope(y[:, NOPE:], cos, sin, ROPE // 4).astype(BF16)
    ckvn = _rms(low[:, Q_LORA:], kvn_ref[...], KV_LORA)
    _mla_keys_values(ckvn, kr, wk_ref, wv_ref, gkn_ref, gkr_ref, cos, sin, km_ref, vm_ref)

    qn = proj(base, base + HB * HEAD)
    for hd in range(HB):
        qna_ref[:, hd * HEAD:(hd + 1) * HEAD] = _rms(qn[:, hd * HEAD:(hd + 1) * HEAD], gnaq_ref[...],
                                                     HEAD).astype(BF16)
    kn = proj(base + HB * HEAD, base + 2 * HB * HEAD)
    k_heads = []
    for hd in range(HB):
        kh = _rms(kn[:, hd * HEAD:(hd + 1) * HEAD], gnak_ref[...], HEAD)
        kna_ref[:, hd * HEAD:(hd + 1) * HEAD] = kh.astype(BF16)
        k_heads.append(kh)
    vn = proj(base + 2 * HB * HEAD, base + 3 * HB * HEAD)
    vna_ref[...] = vn.astype(BF16)

    @pl.when(is_ctx)
    def _():
        ckv_ref[...] = ckvn
        kr_ref[...] = kr[:, :ROPE]
        vnaf_ref[...] = vn
        for hd, kh in enumerate(k_heads):
            knaf_ref[:, hd * HEAD:(hd + 1) * HEAD] = kh


def _even_in(x, mods, g, w, tm=256):
    n_ctx = T_CTX // tm
    resident = lambda a: pl.BlockSpec(a.shape, lambda i: (0, 0), pipeline_mode=pl.Buffered(1))
    small = lambda a: pl.BlockSpec(a.shape, lambda i: (0, 0))
    rows = lambda n: pl.BlockSpec((tm, n), lambda i: (i, 0))
    ctx_rows = lambda n: pl.BlockSpec((tm, n), lambda i: (jnp.minimum(i, n_ctx - 1), 0))
    table = pl.BlockSpec((tm, 128), lambda i: (_rope_block_of_tile(i, tm), 0))
    tok = lambda n, dt: jax.ShapeDtypeStruct((T, n), dt)
    ctx = lambda n: jax.ShapeDtypeStruct((T_CTX, n), F32)
    cos, sin = _rope_tables(ROPE)
    norms = [w[k] for k in ("q_norm", "kv_norm", "gq", "gkn", "gkr", "gnaq", "gnak")]
    return pl.pallas_call(
        functools.partial(_even_in_kernel, n_ctx_tiles=n_ctx),
        grid=(T // tm,),
        in_specs=[rows(D), pl.BlockSpec((1, N_MOD, D), lambda i: (_group_of_tile(i, tm), 0, 0)), small(g),
                  resident(w["w_in"]), resident(w["wq"]), resident(w["wk"]), resident(w["wv"])]
                 + [small(a) for a in norms] + [table, table],
        out_specs=[rows(HA * QK_PAD), rows(HA * QK_PAD), rows(HA * HEAD), rows(HB * HEAD), rows(HB * HEAD),
                   rows(HB * HEAD), ctx_rows(KV_LORA), ctx_rows(ROPE), ctx_rows(HB * HEAD), ctx_rows(HB * HEAD)],
        out_shape=[tok(HA * QK_PAD, BF16), tok(HA * QK_PAD, BF16), tok(HA * HEAD, BF16), tok(HB * HEAD, BF16),
                   tok(HB * HEAD, BF16), tok(HB * HEAD, BF16), ctx(KV_LORA), ctx(ROPE), ctx(HB * HEAD),
                   ctx(HB * HEAD)],
        compiler_params=_cparams(1),
        name="even_in",
    )(x, mods, g, w["w_in"], w["wq"], w["wk"], w["wv"], *norms, cos, sin)


def _cache_kv_kernel(ckv_ref, kr_ref, wk_ref, wv_ref, gkn_ref, gkr_ref, km_ref, vm_ref):
    _mla_keys_values(ckv_ref[...], kr_ref[...], wk_ref, wv_ref, gkn_ref, gkr_ref, None, None, km_ref, vm_ref)


def _cache_kv(ckv, kr, wk, wv, gkn, gkr, tm=512):
    n = ckv.shape[0]
    full = lambda shape: pl.BlockSpec(shape, lambda i: (0, 0))
    rows = lambda w: pl.BlockSpec((tm, w), lambda i: (i, 0))
    return pl.pallas_call(
        _cache_kv_kernel,
        grid=(n // tm,),
        in_specs=[rows(KV_LORA), rows(128), full(wk.shape), full(wv.shape), full(gkn.shape), full(gkr.shape)],
        out_specs=[rows(HA * QK_PAD), rows(HA * HEAD)],
        out_shape=[jax.ShapeDtypeStruct((n, HA * QK_PAD), BF16), jax.ShapeDtypeStruct((n, HA * HEAD), BF16)],
        compiler_params=_cparams(1),
        name="mla_cache_kv",
    )(ckv, kr, wk, wv, gkn, gkr)


def _odd_in_kernel(x_ref, mod_ref, g_ref, w_ref, gq_ref, gk_ref, cos_ref, sin_ref,
                   q_ref, k_ref, v_ref, kf_ref, vf_ref, *, n_ctx_tiles, tm):
    is_ctx = pl.program_id(0) < n_ctx_tiles
    gain, shift = _mod_rows(mod_ref, g_ref, 1)
    k0 = HC * HEAD
    v0 = k0 + KVH_C * HEAD
    cache_rows = []
    for r in range(0, tm, M_CHUNK):
        rows = slice(r, r + M_CHUNK)
        cos = cos_ref[rows, :]
        sin = sin_ref[rows, :]
        h = _modulate(x_ref[rows, :], gain, shift)
        q = jnp.dot(h, w_ref[:, :k0], preferred_element_type=F32)
        for hd in range(HC):
            y = _rms(q[:, hd * HEAD:(hd + 1) * HEAD], gq_ref[...], HEAD)
            q_ref[rows, hd * HEAD:(hd + 1) * HEAD] = _rope(y, cos, sin, HEAD // 4).astype(BF16)
        k = jnp.dot(h, w_ref[:, k0:v0], preferred_element_type=F32)
        k_heads = []
        for hd in range(KVH_C):
            y = _rms(k[:, hd * HEAD:(hd + 1) * HEAD], gk_ref[...], HEAD)
            k_ref[rows, hd * HEAD:(hd + 1) * HEAD] = _rope(y, cos, sin, HEAD // 4).astype(BF16)
            k_heads.append(y)
        v = jnp.dot(h, w_ref[:, v0:], preferred_element_type=F32)
        v_ref[rows, :] = v.astype(BF16)
        cache_rows.append((rows, k_heads, v))

    @pl.when(is_ctx)
    def _():
        for rows, k_heads, v in cache_rows:
            vf_ref[rows, :] = v
            for hd, y in enumerate(k_heads):
                kf_ref[rows, hd * HEAD:(hd + 1) * HEAD] = y


def _odd_in(x, mods, g, w, tm=512):
    n_ctx = T_CTX // tm
    small = lambda a: pl.BlockSpec(a.shape, lambda i: (0, 0))
    rows = lambda n: pl.BlockSpec((tm, n), lambda i: (i, 0))
    ctx_rows = lambda n: pl.BlockSpec((tm, n), lambda i: (jnp.minimum(i, n_ctx - 1), 0))
    table = pl.BlockSpec((tm, 128), lambda i: (_rope_block_of_tile(i, tm), 0))
    tok = lambda n, dt: jax.ShapeDtypeStruct((T, n), dt)
    ctx = lambda n: jax.ShapeDtypeStruct((T_CTX, n), F32)
    cos, sin = _rope_tables(HEAD)
    return pl.pallas_call(
        functools.partial(_odd_in_kernel, n_ctx_tiles=n_ctx, tm=tm),
        grid=(T // tm,),
        in_specs=[rows(D), pl.BlockSpec((1, N_MOD, D), lambda i: (_group_of_tile(i, tm), 0, 0)), small(g),
                  pl.BlockSpec(w["w_in"].shape, lambda i: (0, 0), pipeline_mode=pl.Buffered(1)),
                  small(w["gq"]), small(w["gk"]), table, table],
        out_specs=[rows(HC * HEAD), rows(KVH_C * HEAD), rows(KVH_C * HEAD), ctx_rows(KVH_C * HEAD),
                   ctx_rows(KVH_C * HEAD)],
        out_shape=[tok(HC * HEAD, BF16), tok(KVH_C * HEAD, BF16), tok(KVH_C * HEAD, BF16), ctx(KVH_C * HEAD),
                   ctx(KVH_C * HEAD)],
        compiler_params=_cparams(1),
        name="odd_in",
    )(x, mods, g, w["w_in"], w["gq"], w["gk"], cos, sin)


def _build_na_bias(toep_ref, bias_scr):
    neg = jnp.full((GRID_W, GRID_W), NEG_INF, F32)
    for r in range(GRID_H):
        rs = min(max(r - NA_KH // 2, 0), GRID_H - NA_KH)
        for kr in range(GRID_H):
            tile = toep_ref[0, kr - r + NA_KH - 1] if rs <= kr < rs + NA_KH else neg
            bias_scr[r * GRID_W:(r + 1) * GRID_W, kr * GRID_W:(kr + 1) * GRID_W] = tile


def _attn_kernel(*refs, plan, n_kv, bias_mode, has_sink, hps, groups, dk, dv, qb, c):
    it = iter(refs)
    q_ref = next(it)
    kv_refs = [(next(it), next(it)) for _ in range(n_kv)]
    bias_ref = next(it) if bias_mode is not None else None
    sink_ref = next(it) if has_sink else None
    o_ref = next(it)
    bias_scr = next(it) if bias_mode == "na" else None
    hblk = pl.program_id(0)

    if bias_mode == "na":
        @pl.when(pl.program_id(1) == 0)
        def _():
            _build_na_bias(bias_ref, bias_scr)

    for hh in range(hps):
        for g in range(groups):
            head = hh * groups + g
            sink = sink_ref[(hblk * hps + hh) * groups + g] if has_sink else None
            for qi, segs in enumerate(plan):
                q0 = qi * qb
                q = q_ref[q0:q0 + qb, head * dk:(head + 1) * dk]
                scores = []
                for (si, start, length, biased) in segs:
                    k = kv_refs[si][0][start:start + length, hh * dk:(hh + 1) * dk].astype(BF16)
                    t = lax.dot_general(q, k, _NT, preferred_element_type=F32) * c
                    if biased and bias_mode == "na":
                        t = t + bias_scr[q0:q0 + qb, start:start + length]
                    elif biased:
                        t = t + bias_ref[qi, :, :length]
                    scores.append(t)
                m = jnp.max(scores[0], axis=-1, keepdims=True)
                for t in scores[1:]:
                    m = jnp.maximum(m, jnp.max(t, axis=-1, keepdims=True))
                if has_sink:
                    m = jnp.maximum(m, sink)
                denom = jnp.exp2(sink - m) if has_sink else None
                acc = None
                for t, (si, start, length, _) in zip(scores, segs):
                    p = jnp.exp2(t - m)
                    part = jnp.sum(p, axis=-1, keepdims=True)
                    v = kv_refs[si][1][start:start + length, hh * dv:(hh + 1) * dv].astype(BF16)
                    pv = jnp.dot(p.astype(BF16), v, preferred_element_type=F32)
                    denom = part if denom is None else denom + part
                    acc = pv if acc is None else acc + pv
                o_ref[q0:q0 + qb, head * dv:(head + 1) * dv] = (acc / denom).astype(o_ref.dtype)


def _attention(q, kvs, plan, *, nb, sq, q_rowblk0, n_kv_heads, hps, groups, dk, dv, scale,
               bias=None, bias_mode=None, sink=None, name="attention"):
    grid = (n_kv_heads // hps, nb)
    in_specs = [pl.BlockSpec((sq, hps * groups * dk), lambda h, b: (q_rowblk0 + b, h))]
    args = [q]
    for (k, v, sk, r0) in kvs:
        in_specs.append(pl.BlockSpec((sk, hps * dk), lambda h, b, r0=r0: (r0 + b, h)))
        in_specs.append(pl.BlockSpec((sk, hps * dv), lambda h, b, r0=r0: (r0 + b, h)))
        args += [k, v]
    scratch = []
    if bias_mode == "na":
        in_specs.append(pl.BlockSpec((1,) + bias.shape[1:], lambda h, b: (h, 0, 0, 0)))
        args.append(bias)
        scratch.append(pltpu.VMEM((LAT_S, LAT_S), F32))
    elif bias_mode == "table":
        in_specs.append(pl.BlockSpec(bias.shape, lambda h, b: (0, 0, 0)))
        args.append(bias)
    if sink is not None:
        in_specs.append(pl.BlockSpec(memory_space=pltpu.SMEM))
        args.append(sink)
    body = functools.partial(
        _attn_kernel, plan=plan, n_kv=len(kvs), bias_mode=bias_mode, has_sink=sink is not None,
        hps=hps, groups=groups, dk=dk, dv=dv, qb=min(Q_BLOCK, sq), c=scale * LOG2E)
    return pl.pallas_call(
        body,
        grid=grid,
        in_specs=in_specs,
        out_specs=pl.BlockSpec((sq, hps * groups * dv), lambda h, b: (b, h)),
        out_shape=jax.ShapeDtypeStruct((nb * sq, n_kv_heads * groups * dv), BF16),
        scratch_shapes=scratch,
        compiler_params=_cparams(2),
        name=name,
    )(*args)


_CTX_PLAN = [[(0, 0, CTX_S, False)]]
_N_QB = LAT_S // Q_BLOCK
_MLA_PLAN = [[(0, 0, LAT_S, False), (1, 0, PAST, False)] for _ in range(_N_QB)]
_NA_ROWS = [(0, 8), (0, 12), (4, 16), (8, 16)]
_NA_PLAN = [[(0, lo * GRID_W, (hi - lo) * GRID_W, True), (1, 0, PAST, False)] for lo, hi in _NA_ROWS]
_WIN_SPANS = [(max(i * Q_BLOCK - WINDOW, 0), min((i + 1) * Q_BLOCK + WINDOW, LAT_S)) for i in range(_N_QB)]
_WIN_PLAN = [[(0, lo, hi - lo, True), (1, 0, PAST, False)] for lo, hi in _WIN_SPANS]


def _window_bias():
    width = max(hi - lo for lo, hi in _WIN_SPANS)
    out = np.full((_N_QB, Q_BLOCK, width), NEG_INF, np.float32)
    for i, (lo, hi) in enumerate(_WIN_SPANS):
        q_abs = i * Q_BLOCK + np.arange(Q_BLOCK)[:, None]
        k_abs = lo + np.arange(hi - lo)[None, :]
        out[i, :, :hi - lo] = np.where(np.abs(q_abs - k_abs) <= WINDOW, 0.0, NEG_INF)
    return jnp.asarray(out)


def _rope_tables(rot_dim):
    half = rot_dim // 2
    nf = half // 2
    t = np.arange(LAT_S)
    inv_freq = ROPE_THETA ** (-np.arange(nf, dtype=np.float64) / nf)
    cos = np.zeros((2 * LAT_S, 128), np.float64)
    sin = np.zeros((2 * LAT_S, 128), np.float64)
    cos[:LAT_S, :rot_dim] = 1.0
    for part, pos in enumerate((t // GRID_W, t % GRID_W)):
        ang = pos[:, None].astype(np.float64) * inv_freq[None, :]
        lo = part * half
        cos[LAT_S:, lo:lo + nf] = np.cos(ang)
        cos[LAT_S:, lo + nf:lo + half] = np.cos(ang)
        sin[LAT_S:, lo:lo + nf] = -np.sin(ang)
        sin[LAT_S:, lo + nf:lo + half] = np.sin(ang)
    return jnp.asarray(cos, F32), jnp.asarray(sin, F32)


def _na_toeplitz(rpb):
    c = np.arange(GRID_W)
    ws = np.clip(c - NA_KW // 2, 0, GRID_W - NA_KW)
    col_valid = (c[None, :] >= ws[:, None]) & (c[None, :] < ws[:, None] + NA_KW)
    col_off = np.clip(c[None, :] - c[:, None] + NA_KW - 1, 0, 2 * NA_KW - 2)
    t = rpb[:, :, col_off] * LOG2E
    return jnp.where(jnp.asarray(col_valid)[None, None], t, NEG_INF)


def _even_mixer(x, mods, g, w, caches):
    cache_ckv, cache_krope, cache_nak, cache_nav = caches
    qm, km, vm, qna, kna, vna, new_ckv, new_kr, new_nak, new_nav = _even_in(x, mods, g, w)
    km_ctx, vm_ctx = _cache_kv(cache_ckv, cache_krope, w["wk"], w["wv"], w["gkn"], w["gkr"])
    s_mla = 1.0 / math.sqrt(QK)
    s_na = 1.0 / math.sqrt(HEAD)
    lat0 = T_CTX // LAT_S
    o_mla_ctx = _attention(qm, [(km, vm, CTX_S, 0)], _CTX_PLAN, nb=CTX_B, sq=CTX_S, q_rowblk0=0, n_kv_heads=HA,
                           hps=HA, groups=1, dk=QK_PAD, dv=HEAD, scale=s_mla, name="mla_ctx")
    o_na_ctx = _attention(qna, [(kna, vna, CTX_S, 0)], _CTX_PLAN, nb=CTX_B, sq=CTX_S, q_rowblk0=0, n_kv_heads=HB,
                          hps=HB, groups=1, dk=HEAD, dv=HEAD, scale=s_na, name="na_ctx")
    o_mla_lat = _attention(qm, [(km, vm, LAT_S, lat0), (km_ctx, vm_ctx, PAST, 0)], _MLA_PLAN, nb=LAT_B, sq=LAT_S,
                           q_rowblk0=lat0, n_kv_heads=HA, hps=1, groups=1, dk=QK_PAD, dv=HEAD, scale=s_mla,
                           name="mla_lat")
    o_na_lat = _attention(qna, [(kna, vna, LAT_S, lat0), (cache_nak, cache_nav, PAST, 0)], _NA_PLAN, nb=LAT_B,
                          sq=LAT_S, q_rowblk0=lat0, n_kv_heads=HB, hps=1, groups=1, dk=HEAD, dv=HEAD, scale=s_na,
                          bias=w["na_toep"], bias_mode="na", name="na_lat")
    x = _out_proj([o_mla_ctx, o_na_ctx], [o_mla_lat, o_na_lat], w["w_out"], w["w_out_lead"], x, mods)
    return x, (new_ckv.reshape(CTX_B, CTX_S, KV_LORA), new_kr.reshape(CTX_B, CTX_S, ROPE),
               new_nak.reshape(CTX_B, CTX_S, HB, HEAD), new_nav.reshape(CTX_B, CTX_S, HB, HEAD))


def _odd_mixer(x, mods, g, w, caches):
    cache_k, cache_v = caches
    q, k, v, new_k, new_v = _odd_in(x, mods, g, w)
    scale = 1.0 / math.sqrt(HEAD)
    lat0 = T_CTX // LAT_S
    o_ctx = _attention(q, [(k, v, CTX_S, 0)], _CTX_PLAN, nb=CTX_B, sq=CTX_S, q_rowblk0=0, n_kv_heads=KVH_C,
                       hps=KVH_C, groups=GROUPS_C, dk=HEAD, dv=HEAD, scale=scale, sink=w["sink"], name="gqa_ctx")
    o_lat = _attention(q, [(k, v, LAT_S, lat0), (cache_k, cache_v, PAST, 0)], _WIN_PLAN, nb=LAT_B, sq=LAT_S,
                       q_rowblk0=lat0, n_kv_heads=KVH_C, hps=1, groups=GROUPS_C, dk=HEAD, dv=HEAD, scale=scale,
                       bias=_window_bias(), bias_mode="table", sink=w["sink"], name="gqa_lat")
    x = _out_proj([o_ctx], [o_lat], w["w_out"], w["w_out_lead"], x, mods)
    return x, (new_k.reshape(CTX_B, CTX_S, KVH_C, HEAD), new_v.reshape(CTX_B, CTX_S, KVH_C, HEAD))


def _even_weights(e, even_w_in, w_out_bf16, mla_q_norm, mla_w_q_up, mla_kv_norm, mla_w_kv_up, mla_qk_norm,
                  na_qk_norm, na_rpb):
    w_in = even_w_in[e]
    i0, i1, i2 = Q_LORA, Q_LORA + KV_LORA, Q_LORA + KV_LORA + ROPE
    w_in = jnp.concatenate([w_in[:, :i1], w_in[:, i2:], w_in[:, i1:i2], jnp.zeros((D, 128 - ROPE), F32)], axis=1)
    wq = jnp.pad(mla_w_q_up[e].reshape(Q_LORA, HA, QK), ((0, 0), (0, 0), (0, QK_PAD - QK)))
    wkv = mla_w_kv_up[e].reshape(KV_LORA, HA, NOPE + HEAD)
    qk = mla_qk_norm[e]
    return {
        "w_in": w_in.astype(BF16),
        "w_out": w_out_bf16,
        "w_out_lead": (e,),
        "wq": wq.reshape(Q_LORA, HA * QK_PAD).astype(BF16),
        "wk": wkv[:, :, :NOPE].reshape(KV_LORA, HA * NOPE).astype(BF16),
        "wv": wkv[:, :, NOPE:].reshape(KV_LORA, HA * HEAD).astype(BF16),
        "q_norm": mla_q_norm[e][None, :],
        "kv_norm": mla_kv_norm[e][None, :],
        "gq": jnp.pad(qk[0], (0, QK_PAD - QK))[None, :],
        "gkn": qk[1, :NOPE][None, :],
        "gkr": jnp.pad(qk[1, NOPE:], (0, 128 - ROPE))[None, :],
        "gnaq": na_qk_norm[e, 0][None, :],
        "gnak": na_qk_norm[e, 1][None, :],
        "na_toep": _na_toeplitz(na_rpb[e]),
    }


def kernel(x_prompt, x_sample, cache_mla_ckv, cache_mla_krope, cache_na_k, cache_na_v, cache_gqa_k, cache_gqa_v, c, c_ctx, ada_w, ada_b, norm_g, ffn_w_in, ffn_w_out, even_w_in, even_w_out, mla_q_norm, mla_w_q_up, mla_kv_norm, mla_w_kv_up, mla_qk_norm, na_qk_norm, na_rpb, odd_w_in, odd_w_out, gqa_qk_norm, gqa_sink):
    depth = ada_w.shape[0]
    x = jnp.concatenate([x_prompt.reshape(T_CTX, D), x_sample.reshape(T_LAT, D)], axis=0)
    cond = jnp.concatenate([c_ctx[None, :], c, jnp.zeros((16 - N_GROUPS, D), F32)], axis=0)
    mods_all = _ada_modulation(cond, ada_w, ada_b)
    ffn_w_out16 = ffn_w_out.astype(BF16)
    even_w_out16 = even_w_out.astype(BF16)
    odd_w_out16 = odd_w_out.astype(BF16)
    n_ctx_tiles = T_CTX // 1024

    ckv_l, kr_l, nak_l, nav_l, gk_l, gv_l = [], [], [], [], [], []
    y_prompt = y_sample = None
    for layer in range(depth):
        mods = mods_all[layer, :N_GROUPS].reshape(N_GROUPS, N_MOD, D)
        g = norm_g[layer]
        e = layer // 2
        act = _ffn_in(x, mods, g[0:1], ffn_w_in, (layer, 0), 0)
        x = _mm_residual(act, ffn_w_out16, (layer, 0), x, mods, 2, 0.5, tm=1024, tn=512, name="ffn_out")
        if layer % 2 == 0:
            w = _even_weights(e, even_w_in, even_w_out16, mla_q_norm, mla_w_q_up, mla_kv_norm, mla_w_kv_up,
                              mla_qk_norm, na_qk_norm, na_rpb)
            caches = (cache_mla_ckv[:, e].reshape(LAT_B * PAST, KV_LORA),
                      jnp.pad(cache_mla_krope[:, e].reshape(LAT_B * PAST, ROPE), ((0, 0), (0, 128 - ROPE))),
                      cache_na_k[:, e].reshape(LAT_B * PAST, HB * HEAD),
                      cache_na_v[:, e].reshape(LAT_B * PAST, HB * HEAD))
            x, (ckv, kr, nak, nav) = _even_mixer(x, mods, g[1:2], w, caches)
            ckv_l.append(ckv)
            kr_l.append(kr)
            nak_l.append(nak)
            nav_l.append(nav)
        else:
            w = {"w_in": odd_w_in[e].astype(BF16), "w_out": odd_w_out16, "w_out_lead": (e,),
                 "gq": gqa_qk_norm[e, 0][None, :], "gk": gqa_qk_norm[e, 1][None, :],
                 "sink": gqa_sink[e] * LOG2E}
            caches = (cache_gqa_k[:, e].reshape(LAT_B * PAST, KVH_C * HEAD),
                      cache_gqa_v[:, e].reshape(LAT_B * PAST, KVH_C * HEAD))
            x, (gk, gv) = _odd_mixer(x, mods, g[1:2], w, caches)
            gk_l.append(gk)
            gv_l.append(gv)
        act = _ffn_in(x, mods, g[2:3], ffn_w_in, (layer, 1), 2)
        ffn_out = functools.partial(_mm_residual, act, ffn_w_out16, (layer, 1), x, mods, 8, 0.5, tm=1024, tn=512,
                                    name="ffn_out")
        if layer + 1 < depth:
            x = ffn_out()
        else:
            y_prompt = ffn_out(tile0=0, n_tiles=n_ctx_tiles).reshape(CTX_B, CTX_S, D)
            y_sample = ffn_out(tile0=n_ctx_tiles, n_tiles=T // 1024 - n_ctx_tiles).reshape(LAT_B, LAT_S, D)

    return (y_prompt, y_sample, jnp.stack(ckv_l, axis=1), jnp.stack(kr_l, axis=1), jnp.stack(nak_l, axis=1),
            jnp.stack(nav_l, axis=1), jnp.stack(gk_l, axis=1), jnp.stack(gv_l, axis=1))
```

```python
import functools
import math

import numpy as np
import jax
import jax.numpy as jnp
from jax import lax
from jax.experimental import pallas as pl
from jax.experimental.pallas import tpu as pltpu

F32 = jnp.float32
BF16 = jnp.bfloat16

D = 2048
D_FF = 5632
N_MOD = 9
CTX_B, CTX_S = 16, 256
LAT_B, LAT_S = 8, 1024
PAST = 512
T_CTX = CTX_B * CTX_S
T_LAT = LAT_B * LAT_S
T = T_CTX + T_LAT
GRID_W = 64
GRID_H = LAT_S // GRID_W
N_GROUPS = 1 + LAT_B
HEAD = 128
HA = 8
NOPE, ROPE = 128, 64
QK = NOPE + ROPE
QK_PAD = 256
Q_LORA = 512
KV_LORA = 512
HB = 8
NA_KH, NA_KW = 8, 16
HC, KVH_C = 16, 4
GROUPS_C = HC // KVH_C
WINDOW = 128
IN_EVEN_PAD = 4224
IN_ODD = 3072
ROPE_THETA = 10000.0
EPS = 1e-6
NEG_INF = -1e30
LOG2E = math.log2(math.e)

VMEM_LIMIT = 56 * 1024 * 1024
Q_BLOCK = 256
M_CHUNK = 256
WO_COL_TILES = 8

_NT = (((1,), (1,)), ((), ()))


def _cparams(n_axes):
    return pltpu.CompilerParams(dimension_semantics=("arbitrary",) * n_axes,
                                vmem_limit_bytes=VMEM_LIMIT)


def _group_of_tile(i, tm):
    n_ctx = T_CTX // tm
    per = LAT_S // tm
    return jnp.where(i < n_ctx, 0, (i - n_ctx) // per + 1)


def _rope_block_of_tile(i, tm):
    n_ctx = T_CTX // tm
    per = LAT_S // tm
    return jnp.where(i < n_ctx, 0, per + (i - n_ctx) % per)


def _weight_spec(lead, rows, tn):
    return pl.BlockSpec((None,) * len(lead) + (rows, tn), lambda i, j: lead + (0, j))


def _ada_kernel(c_ref, w_ref, b_ref, o_ref):
    c = c_ref[...]
    a = (c * jax.nn.sigmoid(c)).astype(BF16)
    o_ref[0] = jnp.dot(a, w_ref[0].astype(BF16), preferred_element_type=F32) + b_ref[0]


def _ada_modulation(cond, ada_w, ada_b, tn=1024):
    depth = ada_w.shape[0]
    n = N_MOD * D
    rows = cond.shape[0]
    return pl.pallas_call(
        _ada_kernel,
        grid=(depth, n // tn),
        in_specs=[
            pl.BlockSpec((rows, D), lambda l, j: (0, 0)),
            pl.BlockSpec((1, D, tn), lambda l, j: (l, 0, j)),
            pl.BlockSpec((1, 1, tn), lambda l, j: (l, 0, j)),
        ],
        out_specs=pl.BlockSpec((1, rows, tn), lambda l, j: (l, 0, j)),
        out_shape=jax.ShapeDtypeStruct((depth, rows, n), F32),
        compiler_params=_cparams(2),
        name="ada_modulation",
    )(cond, ada_w, ada_b.reshape(depth, 1, n))


def _mod_rows(mod_ref, g_ref, which):
    shift = mod_ref[0, 3 * which:3 * which + 1, :]
    scale = mod_ref[0, 3 * which + 1:3 * which + 2, :]
    return g_ref[...] * (1.0 + scale), shift


def _modulate(x, gain, shift):
    inv = lax.rsqrt(jnp.mean(x * x, axis=-1, keepdims=True) + EPS)
    return ((x * inv) * gain + shift).astype(BF16)


def _ffn_in_kernel(x_ref, mod_ref, g_ref, wg_ref, wu_ref, wo_ref, o_ref, wo16_ref, h_ref, *, which, tm):
    @pl.when(pl.program_id(0) < WO_COL_TILES)
    def _():
        wo16_ref[...] = wo_ref[...].astype(BF16)

    def chunks(first):
        wg = wg_ref[...].astype(BF16)
        wu = wu_ref[...].astype(BF16)
        if first:
            gain, shift = _mod_rows(mod_ref, g_ref, which)
        for r in range(0, tm, M_CHUNK):
            if first:
                h = _modulate(x_ref[r:r + M_CHUNK, :], gain, shift)
                h_ref[r:r + M_CHUNK, :] = h
            else:
                h = h_ref[r:r + M_CHUNK, :]
            gate = jnp.dot(h, wg, preferred_element_type=F32)
            up = jnp.dot(h, wu, preferred_element_type=F32)
            o_ref[r:r + M_CHUNK, :] = (gate * jax.nn.sigmoid(gate) * up).astype(BF16)

    @pl.when(pl.program_id(1) == 0)
    def _():
        chunks(True)

    @pl.when(pl.program_id(1) != 0)
    def _():
        chunks(False)


def _ffn_in(x, mods, g, w_in, w_out, lead, which, tm=1024, tn=512):
    nj = D_FF // tn
    n_lead = len(lead)
    wo_cols = D // WO_COL_TILES
    assert nj * tn == D_FF and T // tm >= WO_COL_TILES

    def wo_tile(i, j):
        parked = i >= WO_COL_TILES
        return jnp.where(parked, nj - 1, j), jnp.minimum(i, WO_COL_TILES - 1)

    return pl.pallas_call(
        functools.partial(_ffn_in_kernel, which=which, tm=tm),
        grid=(T // tm, nj),
        in_specs=[
            pl.BlockSpec((tm, D), lambda i, j: (i, 0)),
            pl.BlockSpec((1, N_MOD, D), lambda i, j: (_group_of_tile(i, tm), 0, 0)),
            pl.BlockSpec((1, D), lambda i, j: (0, 0)),
            pl.BlockSpec((None,) * n_lead + (D, tn), lambda i, j: lead + (0, j)),
            pl.BlockSpec((None,) * n_lead + (D, tn), lambda i, j: lead + (0, j + nj)),
            pl.BlockSpec((None,) * n_lead + (tn, wo_cols), lambda i, j: lead + wo_tile(i, j)),
        ],
        out_specs=[pl.BlockSpec((tm, tn), lambda i, j: (i, j)),
                   pl.BlockSpec((tn, wo_cols), wo_tile)],
        out_shape=[jax.ShapeDtypeStruct((T, D_FF), BF16), jax.ShapeDtypeStruct((D_FF, D), BF16)],
        scratch_shapes=[pltpu.VMEM((tm, D), BF16)],
        compiler_params=_cparams(2),
        name="ffn_in",
    )(x, mods, g, w_in, w_in, w_out)


def _mm_residual_kernel(a_ref, w_ref, x_ref, mod_ref, o_ref, *, gate_row, coef, tm):
    gate = coef * mod_ref[0, gate_row:gate_row + 1, :]
    for r in range(0, tm, M_CHUNK):
        acc = jnp.dot(a_ref[r:r + M_CHUNK, :], w_ref[...], preferred_element_type=F32)
        o_ref[r:r + M_CHUNK, :] = x_ref[r:r + M_CHUNK, :] + gate * acc


def _mm_residual(a, w, lead, x, mods, gate_row, coef, tm, tn, name, tile0=0, n_tiles=T // 1024):
    k = a.shape[1]
    return pl.pallas_call(
        functools.partial(_mm_residual_kernel, gate_row=gate_row, coef=coef, tm=tm),
        grid=(n_tiles, D // tn),
        in_specs=[
            pl.BlockSpec((tm, k), lambda i, j: (tile0 + i, 0)),
            _weight_spec(lead, k, tn),
            pl.BlockSpec((tm, tn), lambda i, j: (tile0 + i, j)),
            pl.BlockSpec((1, N_MOD, tn), lambda i, j: (_group_of_tile(tile0 + i, tm), 0, j)),
        ],
        out_specs=pl.BlockSpec((tm, tn), lambda i, j: (i, j)),
        out_shape=jax.ShapeDtypeStruct((n_tiles * tm, D), F32),
        compiler_params=_cparams(2),
        name=name,
    )(a, w, x, mods)


def _out_proj_kernel(*refs, widths, n_ctx_tiles, tm):
    n_a = len(widths)
    ctx_refs, lat_refs = refs[:n_a], refs[n_a:2 * n_a]
    w_ref, x_ref, mod_ref, o_ref = refs[2 * n_a:]
    gate = mod_ref[0, 5:6, :]

    def run(a_refs):
        for r in range(0, tm, M_CHUNK):
            acc = None
            k0 = 0
            for a_ref, kw in zip(a_refs, widths):
                part = jnp.dot(a_ref[r:r + M_CHUNK, :], w_ref[k0:k0 + kw, :], preferred_element_type=F32)
                acc = part if acc is None else acc + part
                k0 += kw
            o_ref[r:r + M_CHUNK, :] = x_ref[r:r + M_CHUNK, :] + gate * acc

    @pl.when(pl.program_id(0) < n_ctx_tiles)
    def _():
        run(ctx_refs)

    @pl.when(pl.program_id(0) >= n_ctx_tiles)
    def _():
        run(lat_refs)


def _out_proj(a_ctx, a_lat, w, lead, x, mods, tm=512):
    widths = tuple(a.shape[1] for a in a_ctx)
    n_ctx = T_CTX // tm
    specs = [pl.BlockSpec((tm, kw), lambda i: (jnp.minimum(i, n_ctx - 1), 0)) for kw in widths]
    specs += [pl.BlockSpec((tm, kw), lambda i: (jnp.maximum(i - n_ctx, 0), 0)) for kw in widths]
    return pl.pallas_call(
        functools.partial(_out_proj_kernel, widths=widths, n_ctx_tiles=n_ctx, tm=tm),
        grid=(T // tm,),
        in_specs=specs + [
            pl.BlockSpec((None,) * len(lead) + (D, D), lambda i: lead + (0, 0), pipeline_mode=pl.Buffered(1)),
            pl.BlockSpec((tm, D), lambda i: (i, 0)),
            pl.BlockSpec((1, N_MOD, D), lambda i: (_group_of_tile(i, tm), 0, 0)),
        ],
        out_specs=pl.BlockSpec((tm, D), lambda i: (i, 0)),
        out_shape=jax.ShapeDtypeStruct((T, D), F32),
        compiler_params=_cparams(1),
        name="mixer_out_proj",
    )(*a_ctx, *a_lat, w, x, mods)


def _rope(x, cos, sin, nf):
    lane = lax.broadcasted_iota(jnp.int32, x.shape, 1)
    first = (lane & (2 * nf - 1)) < nf
    partner = jnp.where(first, pltpu.roll(x, 128 - nf, 1), pltpu.roll(x, nf, 1))
    return x * cos + partner * sin


def _rms(x, g, n):
    inv = lax.rsqrt(jnp.sum(x * x, axis=-1, keepdims=True) * (1.0 / n) + EPS)
    return x * inv * g


def _mla_keys_values(ckvn, kr, wk_ref, wv_ref, gkn_ref, gkr_ref, cos, sin, km_ref, vm_ref):
    c16 = ckvn.astype(BF16)
    kn = jnp.dot(c16, wk_ref[...], preferred_element_type=F32)
    vm_ref[...] = jnp.dot(c16, wv_ref[...], preferred_element_type=F32).astype(BF16)
    kr_ss = jnp.sum(kr * kr, axis=-1, keepdims=True)
    for h in range(HA):
        x = kn[:, h * NOPE:(h + 1) * NOPE]
        inv = lax.rsqrt((jnp.sum(x * x, axis=-1, keepdims=True) + kr_ss) * (1.0 / QK) + EPS)
        km_ref[:, h * QK_PAD:h * QK_PAD + NOPE] = (x * inv * gkn_ref[...]).astype(BF16)
        r = kr * inv * gkr_ref[...]
        if cos is not None:
            r = _rope(r, cos, sin, ROPE // 4)
        km_ref[:, h * QK_PAD + NOPE:(h + 1) * QK_PAD] = r.astype(BF16)


def _even_in_kernel(x_ref, mod_ref, g_ref, w_ref, wq_ref, wk_ref, wv_ref, qn_ref, kvn_ref, gq_ref, gkn_ref,
                    gkr_ref, gnaq_ref, gnak_ref, cos_ref, sin_ref,
                    qm_ref, km_ref, vm_ref, qna_ref, kna_ref, vna_ref,
                    ckv_ref, kr_ref, knaf_ref, vnaf_ref, *, n_ctx_tiles):
    is_ctx = pl.program_id(0) < n_ctx_tiles
    cos = cos_ref[...]
    sin = sin_ref[...]
    gain, shift = _mod_rows(mod_ref, g_ref, 1)
    h = _modulate(x_ref[...], gain, shift)
    proj = lambda lo, hi: jnp.dot(h, w_ref[:, lo:hi], preferred_element_type=F32)
    base = Q_LORA + KV_LORA

    low = proj(0, base)
    kr = proj(IN_EVEN_PAD - 128, IN_EVEN_PAD)
    cq = _rms(low[:, :Q_LORA], qn_ref[...], Q_LORA)
    q = jnp.dot(cq.astype(BF16), wq_ref[...], preferred_element_type=F32)
    for hd in range(HA):
        y = _rms(q[:, hd * QK_PAD:(hd + 1) * QK_PAD], gq_ref[...], QK)
        qm_ref[:, hd * QK_PAD:hd * QK_PAD + NOPE] = y[:, :NOPE].astype(BF16)
        qm_ref[:, hd * QK_PAD + NOPE:(hd + 1) * QK_PAD] = _rope(y[:, NOPE:], cos, sin, ROPE // 4).astype(BF16)
    ckvn = _rms(low[:, Q_LORA:], kvn_ref[...], KV_LORA)
    _mla_keys_values(ckvn, kr, wk_ref, wv_ref, gkn_ref, gkr_ref, cos, sin, km_ref, vm_ref)

    qn = proj(base, base + HB * HEAD)
    for hd in range(HB):
        qna_ref[:, hd * HEAD:(hd + 1) * HEAD] = _rms(qn[:, hd * HEAD:(hd + 1) * HEAD], gnaq_ref[...],
                                                     HEAD).astype(BF16)
    kn = proj(base + HB * HEAD, base + 2 * HB * HEAD)
    k_heads = []
    for hd in range(HB):
        kh = _rms(kn[:, hd * HEAD:(hd + 1) * HEAD], gnak_ref[...], HEAD)
        kna_ref[:, hd * HEAD:(hd + 1) * HEAD] = kh.astype(BF16)
        k_heads.append(kh)
    vn = proj(base + 2 * HB * HEAD, base + 3 * HB * HEAD)
    vna_ref[...] = vn.astype(BF16)

    @pl.when(is_ctx)
    def _():
        ckv_ref[...] = ckvn
        kr_ref[...] = kr[:, :ROPE]
        vnaf_ref[...] = vn
        for hd, kh in enumerate(k_heads):
            knaf_ref[:, hd * HEAD:(hd + 1) * HEAD] = kh


def _even_in(x, mods, g, w, tm=256):
    n_ctx = T_CTX // tm
    resident = lambda a: pl.BlockSpec(a.shape, lambda i: (0, 0), pipeline_mode=pl.Buffered(1))
    small = lambda a: pl.BlockSpec(a.shape, lambda i: (0, 0))
    rows = lambda n: pl.BlockSpec((tm, n), lambda i: (i, 0))
    ctx_rows = lambda n: pl.BlockSpec((tm, n), lambda i: (jnp.minimum(i, n_ctx - 1), 0))
    table = pl.BlockSpec((tm, 128), lambda i: (_rope_block_of_tile(i, tm), 0))
    tok = lambda n, dt: jax.ShapeDtypeStruct((T, n), dt)
    ctx = lambda n: jax.ShapeDtypeStruct((T_CTX, n), F32)
    cos, sin = _rope_tables(ROPE)
    norms = [w[k] for k in ("q_norm", "kv_norm", "gq", "gkn", "gkr", "gnaq", "gnak")]
    return pl.pallas_call(
        functools.partial(_even_in_kernel, n_ctx_tiles=n_ctx),
        grid=(T // tm,),
        in_specs=[rows(D), pl.BlockSpec((1, N_MOD, D), lambda i: (_group_of_tile(i, tm), 0, 0)), small(g),
                  resident(w["w_in"]), resident(w["wq"]), resident(w["wk"]), resident(w["wv"])]
                 + [small(a) for a in norms] + [table, table],
        out_specs=[rows(HA * QK_PAD), rows(HA * QK_PAD), rows(HA * HEAD), rows(HB * HEAD), rows(HB * HEAD),
                   rows(HB * HEAD), ctx_rows(KV_LORA), ctx_rows(ROPE), ctx_rows(HB * HEAD), ctx_rows(HB * HEAD)],
        out_shape=[tok(HA * QK_PAD, BF16), tok(HA * QK_PAD, BF16), tok(HA * HEAD, BF16), tok(HB * HEAD, BF16),
                   tok(HB * HEAD, BF16), tok(HB * HEAD, BF16), ctx(KV_LORA), ctx(ROPE), ctx(HB * HEAD),
                   ctx(HB * HEAD)],
        compiler_params=_cparams(1),
        name="even_in",
    )(x, mods, g, w["w_in"], w["wq"], w["wk"], w["wv"], *norms, cos, sin)


def _cache_kv_kernel(ckv_ref, kr_ref, wk_ref, wv_ref, gkn_ref, gkr_ref, km_ref, vm_ref):
    _mla_keys_values(ckv_ref[...], kr_ref[...], wk_ref, wv_ref, gkn_ref, gkr_ref, None, None, km_ref, vm_ref)


def _cache_kv(ckv, kr, wk, wv, gkn, gkr, tm=512):
    n = ckv.shape[0]
    full = lambda shape: pl.BlockSpec(shape, lambda i: (0, 0))
    rows = lambda w: pl.BlockSpec((tm, w), lambda i: (i, 0))
    return pl.pallas_call(
        _cache_kv_kernel,
        grid=(n // tm,),
        in_specs=[rows(KV_LORA), rows(128), full(wk.shape), full(wv.shape), full(gkn.shape), full(gkr.shape)],
        out_specs=[rows(HA * QK_PAD), rows(HA * HEAD)],
        out_shape=[jax.ShapeDtypeStruct((n, HA * QK_PAD), BF16), jax.ShapeDtypeStruct((n, HA * HEAD), BF16)],
        compiler_params=_cparams(1),
        name="mla_cache_kv",
    )(ckv, kr, wk, wv, gkn, gkr)


def _odd_in_kernel(x_ref, mod_ref, g_ref, w_ref, gq_ref, gk_ref, cos_ref, sin_ref,
                   q_ref, k_ref, v_ref, kf_ref, vf_ref, *, n_ctx_tiles, tm):
    is_ctx = pl.program_id(0) < n_ctx_tiles
    gain, shift = _mod_rows(mod_ref, g_ref, 1)
    k0 = HC * HEAD
    v0 = k0 + KVH_C * HEAD
    cache_rows = []
    for r in range(0, tm, M_CHUNK):
        rows = slice(r, r + M_CHUNK)
        cos = cos_ref[rows, :]
        sin = sin_ref[rows, :]
        h = _modulate(x_ref[rows, :], gain, shift)
        q = jnp.dot(h, w_ref[:, :k0], preferred_element_type=F32)
        for hd in range(HC):
            y = _rms(q[:, hd * HEAD:(hd + 1) * HEAD], gq_ref[...], HEAD)
            q_ref[rows, hd * HEAD:(hd + 1) * HEAD] = _rope(y, cos, sin, HEAD // 4).astype(BF16)
        k = jnp.dot(h, w_ref[:, k0:v0], preferred_element_type=F32)
        k_heads = []
        for hd in range(KVH_C):
            y = _rms(k[:, hd * HEAD:(hd + 1) * HEAD], gk_ref[...], HEAD)
            k_ref[rows, hd * HEAD:(hd + 1) * HEAD] = _rope(y, cos, sin, HEAD // 4).astype(BF16)
            k_heads.append(y)
        v = jnp.dot(h, w_ref[:, v0:], preferred_element_type=F32)
        v_ref[rows, :] = v.astype(BF16)
        cache_rows.append((rows, k_heads, v))

    @pl.when(is_ctx)
    def _():
        for rows, k_heads, v in cache_rows:
            vf_ref[rows, :] = v
            for hd, y in enumerate(k_heads):
                kf_ref[rows, hd * HEAD:(hd + 1) * HEAD] = y


def _odd_in(x, mods, g, w, tm=512):
    n_ctx = T_CTX // tm
    small = lambda a: pl.BlockSpec(a.shape, lambda i: (0, 0))
    rows = lambda n: pl.BlockSpec((tm, n), lambda i: (i, 0))
    ctx_rows = lambda n: pl.BlockSpec((tm, n), lambda i: (jnp.minimum(i, n_ctx - 1), 0))
    table = pl.BlockSpec((tm, 128), lambda i: (_rope_block_of_tile(i, tm), 0))
    tok = lambda n, dt: jax.ShapeDtypeStruct((T, n), dt)
    ctx = lambda n: jax.ShapeDtypeStruct((T_CTX, n), F32)
    cos, sin = _rope_tables(HEAD)
    return pl.pallas_call(
        functools.partial(_odd_in_kernel, n_ctx_tiles=n_ctx, tm=tm),
        grid=(T // tm,),
        in_specs=[rows(D), pl.BlockSpec((1, N_MOD, D), lambda i: (_group_of_tile(i, tm), 0, 0)), small(g),
                  pl.BlockSpec(w["w_in"].shape, lambda i: (0, 0), pipeline_mode=pl.Buffered(1)),
                  small(w["gq"]), small(w["gk"]), table, table],
        out_specs=[rows(HC * HEAD), rows(KVH_C * HEAD), rows(KVH_C * HEAD), ctx_rows(KVH_C * HEAD),
                   ctx_rows(KVH_C * HEAD)],
        out_shape=[tok(HC * HEAD, BF16), tok(KVH_C * HEAD, BF16), tok(KVH_C * HEAD, BF16), ctx(KVH_C * HEAD),
                   ctx(KVH_C * HEAD)],
        compiler_params=_cparams(1),
        name="odd_in",
    )(x, mods, g, w["w_in"], w["gq"], w["gk"], cos, sin)


def _build_na_bias(toep_ref, bias_scr, hps):
    neg = jnp.full((GRID_W, GRID_W), NEG_INF, F32)
    for hh in range(hps):
        for r in range(GRID_H):
            rs = min(max(r - NA_KH // 2, 0), GRID_H - NA_KH)
            for kr in range(GRID_H):
                tile = toep_ref[hh, kr - r + NA_KH - 1] if rs <= kr < rs + NA_KH else neg
                bias_scr[hh, r * GRID_W:(r + 1) * GRID_W, kr * GRID_W:(kr + 1) * GRID_W] = tile


def _attn_kernel(*refs, plan, n_kv, bias_mode, has_sink, hps, groups, dk, dv, qb, c):
    it = iter(refs)
    q_ref = next(it)
    kv_refs = [(next(it), next(it)) for _ in range(n_kv)]
    bias_ref = next(it) if bias_mode is not None else None
    sink_ref = next(it) if has_sink else None
    o_ref = next(it)
    bias_scr = next(it) if bias_mode == "na" else None
    hblk = pl.program_id(0)

    if bias_mode == "na":
        @pl.when(pl.program_id(1) == 0)
        def _():
            _build_na_bias(bias_ref, bias_scr, hps)

    def one_head(hh, head, q_cols, o_cols):
        sink = sink_ref[hblk * hps * groups + head] if has_sink else None
        for qi, segs in enumerate(plan):
            q0 = qi * qb
            q = q_ref[q0:q0 + qb, q_cols]
            scores = []
            for (si, start, length, biased) in segs:
                k = kv_refs[si][0][start:start + length, hh * dk:(hh + 1) * dk].astype(BF16)
                t = lax.dot_general(q, k, _NT, preferred_element_type=F32) * c
                if biased and bias_mode == "na":
                    t = t + bias_scr[hh, q0:q0 + qb, start:start + length]
                elif biased:
                    t = t + bias_ref[qi, :, :length]
                scores.append(t)
            m = jnp.max(scores[0], axis=-1, keepdims=True)
            for t in scores[1:]:
                m = jnp.maximum(m, jnp.max(t, axis=-1, keepdims=True))
            if has_sink:
                m = jnp.maximum(m, sink)
            acc = None
            for t, (si, start, length, _) in zip(scores, segs):
                p = jnp.exp2(t - m).astype(BF16)
                v = kv_refs[si][1][start:start + length, hh * dv:(hh + 1) * dv].astype(BF16)
                v1 = jnp.concatenate([v, jnp.ones((length, dv), BF16)], axis=-1)
                pv = jnp.dot(p, v1, preferred_element_type=F32)
                acc = pv if acc is None else acc + pv
            denom = acc[:, dv:]
            if has_sink:
                denom = denom + jnp.exp2(sink - m)
            o_ref[q0:q0 + qb, o_cols] = (acc[:, :dv] / denom).astype(o_ref.dtype)

    for hh in range(hps):
        if groups > 1 and len(plan) > 1:
            def group_body(g, carry, hh=hh):
                head = hh * groups + g
                one_head(hh, head, pl.ds(pl.multiple_of(head * dk, dk), dk), pl.ds(pl.multiple_of(head * dv, dv), dv))
                return carry

            lax.fori_loop(0, groups, group_body, 0, unroll=2)
        else:
            for g in range(groups):
                head = hh * groups + g
                one_head(hh, head, slice(head * dk, (head + 1) * dk), slice(head * dv, (head + 1) * dv))


def _attention(q, kvs, plan, *, nb, sq, q_rowblk0, n_kv_heads, hps, groups, dk, dv, scale,
               bias=None, bias_mode=None, sink=None, name="attention"):
    grid = (n_kv_heads // hps, nb)
    in_specs = [pl.BlockSpec((sq, hps * groups * dk), lambda h, b: (q_rowblk0 + b, h))]
    args = [q]
    for (k, v, sk, r0) in kvs:
        in_specs.append(pl.BlockSpec((sk, hps * dk), lambda h, b, r0=r0: (r0 + b, h)))
        in_specs.append(pl.BlockSpec((sk, hps * dv), lambda h, b, r0=r0: (r0 + b, h)))
        args += [k, v]
    scratch = []
    if bias_mode == "na":
        in_specs.append(pl.BlockSpec((hps,) + bias.shape[1:], lambda h, b: (h, 0, 0, 0)))
        args.append(bias)
        scratch.append(pltpu.VMEM((hps, LAT_S, LAT_S), F32))
    elif bias_mode == "table":
        in_specs.append(pl.BlockSpec(bias.shape, lambda h, b: (0, 0, 0)))
        args.append(bias)
    if sink is not None:
        in_specs.append(pl.BlockSpec(memory_space=pltpu.SMEM))
        args.append(sink)
    body = functools.partial(
        _attn_kernel, plan=plan, n_kv=len(kvs), bias_mode=bias_mode, has_sink=sink is not None,
        hps=hps, groups=groups, dk=dk, dv=dv, qb=min(Q_BLOCK, sq), c=scale * LOG2E)
    return pl.pallas_call(
        body,
        grid=grid,
        in_specs=in_specs,
        out_specs=pl.BlockSpec((sq, hps * groups * dv), lambda h, b: (b, h)),
        out_shape=jax.ShapeDtypeStruct((nb * sq, n_kv_heads * groups * dv), BF16),
        scratch_shapes=scratch,
        compiler_params=_cparams(2),
        name=name,
    )(*args)


_CTX_PLAN = [[(0, 0, CTX_S, False)]]
_N_QB = LAT_S // Q_BLOCK
_MLA_PLAN = [[(0, 0, LAT_S, False), (1, 0, PAST, False)] for _ in range(_N_QB)]
_NA_ROWS = [(0, 8), (0, 12), (4, 16), (8, 16)]
_NA_PLAN = [[(0, lo * GRID_W, (hi - lo) * GRID_W, True), (1, 0, PAST, False)] for lo, hi in _NA_ROWS]
_WIN_SPANS = [(max(i * Q_BLOCK - WINDOW, 0), min((i + 1) * Q_BLOCK + WINDOW, LAT_S)) for i in range(_N_QB)]
_WIN_PLAN = [[(0, lo, hi - lo, True), (1, 0, PAST, False)] for lo, hi in _WIN_SPANS]


def _window_bias():
    width = max(hi - lo for lo, hi in _WIN_SPANS)
    out = np.full((_N_QB, Q_BLOCK, width), NEG_INF, np.float32)
    for i, (lo, hi) in enumerate(_WIN_SPANS):
        q_abs = i * Q_BLOCK + np.arange(Q_BLOCK)[:, None]
        k_abs = lo + np.arange(hi - lo)[None, :]
        out[i, :, :hi - lo] = np.where(np.abs(q_abs - k_abs) <= WINDOW, 0.0, NEG_INF)
    return jnp.asarray(out)


def _rope_tables(rot_dim):
    half = rot_dim // 2
    nf = half // 2
    t = np.arange(LAT_S)
    inv_freq = ROPE_THETA ** (-np.arange(nf, dtype=np.float64) / nf)
    cos = np.zeros((2 * LAT_S, 128), np.float64)
    sin = np.zeros((2 * LAT_S, 128), np.float64)
    cos[:LAT_S, :rot_dim] = 1.0
    for part, pos in enumerate((t // GRID_W, t % GRID_W)):
        ang = pos[:, None].astype(np.float64) * inv_freq[None, :]
        lo = part * half
        cos[LAT_S:, lo:lo + nf] = np.cos(ang)
        cos[LAT_S:, lo + nf:lo + half] = np.cos(ang)
        sin[LAT_S:, lo:lo + nf] = -np.sin(ang)
        sin[LAT_S:, lo + nf:lo + half] = np.sin(ang)
    return jnp.asarray(cos, F32), jnp.asarray(sin, F32)


def _na_toeplitz(rpb):
    c = np.arange(GRID_W)
    ws = np.clip(c - NA_KW // 2, 0, GRID_W - NA_KW)
    col_valid = (c[None, :] >= ws[:, None]) & (c[None, :] < ws[:, None] + NA_KW)
    n_off, n_rel = rpb.shape[1], rpb.shape[2]
    width = 2 * GRID_W
    lead = GRID_W - NA_KW
    u = jnp.pad(rpb * LOG2E, ((0, 0), (0, 0), (lead, width - lead - n_rel)))
    a = jnp.broadcast_to(u[:, :, None, :], (HB, n_off, GRID_W, width)).reshape(HB, n_off, GRID_W * width)
    a = jnp.pad(a, ((0, 0), (0, 0), (0, GRID_W))).reshape(HB, n_off, GRID_W, width + 1)
    t = a[:, :, ::-1, :GRID_W]
    return jnp.where(jnp.asarray(col_valid)[None, None], t, NEG_INF)


def _even_mixer(x, mods, g, w, caches):
    cache_ckv, cache_krope, cache_nak, cache_nav = caches
    qm, km, vm, qna, kna, vna, new_ckv, new_kr, new_nak, new_nav = _even_in(x, mods, g, w)
    km_ctx, vm_ctx = _cache_kv(cache_ckv, cache_krope, w["wk"], w["wv"], w["gkn"], w["gkr"])
    s_mla = 1.0 / math.sqrt(QK)
    s_na = 1.0 / math.sqrt(HEAD)
    lat0 = T_CTX // LAT_S
    o_mla_ctx = _attention(qm, [(km, vm, CTX_S, 0)], _CTX_PLAN, nb=CTX_B, sq=CTX_S, q_rowblk0=0, n_kv_heads=HA,
                           hps=HA, groups=1, dk=QK_PAD, dv=HEAD, scale=s_mla, name="mla_ctx")
    o_na_ctx = _attention(qna, [(kna, vna, CTX_S, 0)], _CTX_PLAN, nb=CTX_B, sq=CTX_S, q_rowblk0=0, n_kv_heads=HB,
                          hps=HB, groups=1, dk=HEAD, dv=HEAD, scale=s_na, name="na_ctx")
    o_mla_lat = _attention(qm, [(km, vm, LAT_S, lat0), (km_ctx, vm_ctx, PAST, 0)], _MLA_PLAN, nb=LAT_B, sq=LAT_S,
                           q_rowblk0=lat0, n_kv_heads=HA, hps=4, groups=1, dk=QK_PAD, dv=HEAD, scale=s_mla,
                           name="mla_lat")
    o_na_lat = _attention(qna, [(kna, vna, LAT_S, lat0), (cache_nak, cache_nav, PAST, 0)], _NA_PLAN, nb=LAT_B,
                          sq=LAT_S, q_rowblk0=lat0, n_kv_heads=HB, hps=2, groups=1, dk=HEAD, dv=HEAD, scale=s_na,
                          bias=w["na_toep"], bias_mode="na", name="na_lat")
    x = _out_proj([o_mla_ctx, o_na_ctx], [o_mla_lat, o_na_lat], w["w_out"], w["w_out_lead"], x, mods)
    return x, (new_ckv.reshape(CTX_B, CTX_S, KV_LORA), new_kr.reshape(CTX_B, CTX_S, ROPE),
               new_nak.reshape(CTX_B, CTX_S, HB, HEAD), new_nav.reshape(CTX_B, CTX_S, HB, HEAD))


def _odd_mixer(x, mods, g, w, caches):
    cache_k, cache_v = caches
    q, k, v, new_k, new_v = _odd_in(x, mods, g, w)
    scale = 1.0 / math.sqrt(HEAD)
    lat0 = T_CTX // LAT_S
    o_ctx = _attention(q, [(k, v, CTX_S, 0)], _CTX_PLAN, nb=CTX_B, sq=CTX_S, q_rowblk0=0, n_kv_heads=KVH_C,
                       hps=KVH_C, groups=GROUPS_C, dk=HEAD, dv=HEAD, scale=scale, sink=w["sink"], name="gqa_ctx")
    o_lat = _attention(q, [(k, v, LAT_S, lat0), (cache_k, cache_v, PAST, 0)], _WIN_PLAN, nb=LAT_B, sq=LAT_S,
                       q_rowblk0=lat0, n_kv_heads=KVH_C, hps=1, groups=GROUPS_C, dk=HEAD, dv=HEAD, scale=scale,
                       bias=_window_bias(), bias_mode="table", sink=w["sink"], name="gqa_lat")
    x = _out_proj([o_ctx], [o_lat], w["w_out"], w["w_out_lead"], x, mods)
    return x, (new_k.reshape(CTX_B, CTX_S, KVH_C, HEAD), new_v.reshape(CTX_B, CTX_S, KVH_C, HEAD))


def _even_weights(e, even_w_in, w_out_bf16, mla_q_norm, mla_w_q_up, mla_kv_norm, mla_w_kv_up, mla_qk_norm,
                  na_qk_norm, na_rpb):
    w_in = even_w_in[e]
    i0, i1, i2 = Q_LORA, Q_LORA + KV_LORA, Q_LORA + KV_LORA + ROPE
    w_in = jnp.concatenate([w_in[:, :i1], w_in[:, i2:], w_in[:, i1:i2], jnp.zeros((D, 128 - ROPE), F32)], axis=1)
    wq = jnp.pad(mla_w_q_up[e].reshape(Q_LORA, HA, QK), ((0, 0), (0, 0), (0, QK_PAD - QK)))
    wkv = mla_w_kv_up[e].reshape(KV_LORA, HA, NOPE + HEAD)
    qk = mla_qk_norm[e]
    return {
        "w_in": w_in.astype(BF16),
        "w_out": w_out_bf16,
        "w_out_lead": (e,),
        "wq": wq.reshape(Q_LORA, HA * QK_PAD).astype(BF16),
        "wk": wkv[:, :, :NOPE].reshape(KV_LORA, HA * NOPE).astype(BF16),
        "wv": wkv[:, :, NOPE:].reshape(KV_LORA, HA * HEAD).astype(BF16),
        "q_norm": mla_q_norm[e][None, :],
        "kv_norm": mla_kv_norm[e][None, :],
        "gq": jnp.pad(qk[0], (0, QK_PAD - QK))[None, :],
        "gkn": qk[1, :NOPE][None, :],
        "gkr": jnp.pad(qk[1, NOPE:], (0, 128 - ROPE))[None, :],
        "gnaq": na_qk_norm[e, 0][None, :],
        "gnak": na_qk_norm[e, 1][None, :],
        "na_toep": _na_toeplitz(na_rpb[e]),
    }


def kernel(x_prompt, x_sample, cache_mla_ckv, cache_mla_krope, cache_na_k, cache_na_v, cache_gqa_k, cache_gqa_v, c, c_ctx, ada_w, ada_b, norm_g, ffn_w_in, ffn_w_out, even_w_in, even_w_out, mla_q_norm, mla_w_q_up, mla_kv_norm, mla_w_kv_up, mla_qk_norm, na_qk_norm, na_rpb, odd_w_in, odd_w_out, gqa_qk_norm, gqa_sink):
    depth = ada_w.shape[0]
    x = jnp.concatenate([x_prompt.reshape(T_CTX, D), x_sample.reshape(T_LAT, D)], axis=0)
    cond = jnp.concatenate([c_ctx[None, :], c, jnp.zeros((16 - N_GROUPS, D), F32)], axis=0)
    mods_all = _ada_modulation(cond, ada_w, ada_b)
    even_w_out16 = even_w_out.astype(BF16)
    odd_w_out16 = odd_w_out.astype(BF16)
    n_ctx_tiles = T_CTX // 1024

    ckv_l, kr_l, nak_l, nav_l, gk_l, gv_l = [], [], [], [], [], []
    y_prompt = y_sample = None
    for layer in range(depth):
        mods = mods_all[layer, :N_GROUPS].reshape(N_GROUPS, N_MOD, D)
        g = norm_g[layer]
        e = layer // 2
        act, w_out16 = _ffn_in(x, mods, g[0:1], ffn_w_in, ffn_w_out, (layer, 0), 0)
        x = _mm_residual(act, w_out16, (), x, mods, 2, 0.5, tm=1024, tn=512, name="ffn_out")
        if layer % 2 == 0:
            w = _even_weights(e, even_w_in, even_w_out16, mla_q_norm, mla_w_q_up, mla_kv_norm, mla_w_kv_up,
                              mla_qk_norm, na_qk_norm, na_rpb)
            caches = (cache_mla_ckv[:, e].reshape(LAT_B * PAST, KV_LORA),
                      jnp.pad(cache_mla_krope[:, e].reshape(LAT_B * PAST, ROPE), ((0, 0), (0, 128 - ROPE))),
                      cache_na_k[:, e].reshape(LAT_B * PAST, HB * HEAD),
                      cache_na_v[:, e].reshape(LAT_B * PAST, HB * HEAD))
            x, (ckv, kr, nak, nav) = _even_mixer(x, mods, g[1:2], w, caches)
            ckv_l.append(ckv)
            kr_l.append(kr)
            nak_l.append(nak)
            nav_l.append(nav)
        else:
            w = {"w_in": odd_w_in[e].astype(BF16), "w_out": odd_w_out16, "w_out_lead": (e,),
                 "gq": gqa_qk_norm[e, 0][None, :], "gk": gqa_qk_norm[e, 1][None, :],
                 "sink": gqa_sink[e] * LOG2E}
            caches = (cache_gqa_k[:, e].reshape(LAT_B * PAST, KVH_C * HEAD),
                      cache_gqa_v[:, e].reshape(LAT_B * PAST, KVH_C * HEAD))
            x, (gk, gv) = _odd_mixer(x, mods, g[1:2], w, caches)
            gk_l.append(gk)
            gv_l.append(gv)
        act, w_out16 = _ffn_in(x, mods, g[2:3], ffn_w_in, ffn_w_out, (layer, 1), 2)
        ffn_out = functools.partial(_mm_residual, act, w_out16, (), x, mods, 8, 0.5, tm=1024, tn=512,
                                    name="ffn_out")
        if layer + 1 < depth:
            x = ffn_out()
        else:
            y_prompt = ffn_out(tile0=0, n_tiles=n_ctx_tiles).reshape(CTX_B, CTX_S, D)
            y_sample = ffn_out(tile0=n_ctx_tiles, n_tiles=T // 1024 - n_ctx_tiles).reshape(LAT_B, LAT_S, D)

    return (y_prompt, y_sample, jnp.stack(ckv_l, axis=1), jnp.stack(kr_l, axis=1), jnp.stack(nak_l, axis=1),
            jnp.stack(nav_l, axis=1), jnp.stack(gk_l, axis=1), jnp.stack(gv_l, axis=1))
```

```python
import functools
import math

import numpy as np
import jax
import jax.numpy as jnp
from jax import lax
from jax.experimental import pallas as pl
from jax.experimental.pallas import tpu as pltpu

F32 = jnp.float32
BF16 = jnp.bfloat16

D = 2048
D_FF = 5632
N_MOD = 9
CTX_B, CTX_S = 16, 256
LAT_B, LAT_S = 8, 1024
PAST = 512
T_CTX = CTX_B * CTX_S
T_LAT = LAT_B * LAT_S
T = T_CTX + T_LAT
GRID_W = 64
GRID_H = LAT_S // GRID_W
N_GROUPS = 1 + LAT_B
HEAD = 128
HA = 8
NOPE, ROPE = 128, 64
QK = NOPE + ROPE
QK_PAD = 256
Q_LORA = 512
KV_LORA = 512
HB = 8
NA_KH, NA_KW = 8, 16
HC, KVH_C = 16, 4
GROUPS_C = HC // KVH_C
WINDOW = 128
IN_EVEN_PAD = 4224
IN_ODD = 3072
ROPE_THETA = 10000.0
EPS = 1e-6
NEG_INF = -1e30
LOG2E = math.log2(math.e)

VMEM_LIMIT = 56 * 1024 * 1024
Q_BLOCK = 256
M_CHUNK = 256
WO_COL_TILES = 4
FFN_OUT_TM = 512

_NT = (((1,), (1,)), ((), ()))


def _cparams(n_axes):
    return pltpu.CompilerParams(dimension_semantics=("arbitrary",) * n_axes,
                                vmem_limit_bytes=VMEM_LIMIT)


def _group_of_tile(i, tm):
    n_ctx = T_CTX // tm
    per = LAT_S // tm
    return jnp.where(i < n_ctx, 0, (i - n_ctx) // per + 1)


def _group_of_row(row):
    return jnp.where(row < T_CTX, 0, (row - T_CTX) // LAT_S + 1)


def _rope_block_of_tile(i, tm):
    n_ctx = T_CTX // tm
    per = LAT_S // tm
    return jnp.where(i < n_ctx, 0, per + (i - n_ctx) % per)


def _weight_spec(lead, rows, tn):
    if tn == D:
        return pl.BlockSpec((None,) * len(lead) + (rows, tn), lambda i, j: lead + (0, j),
                            pipeline_mode=pl.Buffered(1))
    return pl.BlockSpec((None,) * len(lead) + (rows, tn), lambda i, j: lead + (0, j))


def _ada_kernel(c_ref, w_ref, b_ref, o_ref):
    c = c_ref[...]
    a = (c * jax.nn.sigmoid(c)).astype(BF16)
    o_ref[0] = jnp.dot(a, w_ref[0].astype(BF16), preferred_element_type=F32) + b_ref[0]


def _ada_modulation(cond, ada_w, ada_b, tn=1024):
    depth = ada_w.shape[0]
    n = N_MOD * D
    rows = cond.shape[0]
    return pl.pallas_call(
        _ada_kernel,
        grid=(depth, n // tn),
        in_specs=[
            pl.BlockSpec((rows, D), lambda l, j: (0, 0)),
            pl.BlockSpec((1, D, tn), lambda l, j: (l, 0, j)),
            pl.BlockSpec((1, 1, tn), lambda l, j: (l, 0, j)),
        ],
        out_specs=pl.BlockSpec((1, rows, tn), lambda l, j: (l, 0, j)),
        out_shape=jax.ShapeDtypeStruct((depth, rows, n), F32),
        compiler_params=_cparams(2),
        name="ada_modulation",
    )(cond, ada_w, ada_b.reshape(depth, 1, n))


def _mod_rows(mod_ref, g_ref, which):
    shift = mod_ref[0, 3 * which:3 * which + 1, :]
    scale = mod_ref[0, 3 * which + 1:3 * which + 2, :]
    return g_ref[...] * (1.0 + scale), shift


def _modulate(x, gain, shift):
    inv = lax.rsqrt(jnp.mean(x * x, axis=-1, keepdims=True) + EPS)
    return ((x * inv) * gain + shift).astype(BF16)


def _ffn_in_kernel(x_ref, mod_ref, g_ref, wg_ref, wu_ref, wo_ref, o_ref, wo16_ref, h_ref, *, which, tm):
    i = pl.program_id(0)

    @pl.when(i < WO_COL_TILES)
    def _():
        wo16_ref[...] = wo_ref[...].astype(BF16)

    def chunks(first):
        wg = wg_ref[...].astype(BF16)
        wu = wu_ref[...].astype(BF16)
        for r in range(0, tm, M_CHUNK):
            if first:
                if r % LAT_S == 0:
                    group = _group_of_row(i * tm + r)
                    shift = mod_ref[group, 3 * which:3 * which + 1, :]
                    gain = g_ref[...] * (1.0 + mod_ref[group, 3 * which + 1:3 * which + 2, :])
                h = _modulate(x_ref[r:r + M_CHUNK, :], gain, shift)
                h_ref[r:r + M_CHUNK, :] = h
            else:
                h = h_ref[r:r + M_CHUNK, :]
            gate = jnp.dot(h, wg, preferred_element_type=F32)
            up = jnp.dot(h, wu, preferred_element_type=F32)
            o_ref[r:r + M_CHUNK, :] = (gate * jax.nn.sigmoid(gate) * up).astype(BF16)

    @pl.when(pl.program_id(1) == 0)
    def _():
        chunks(True)

    @pl.when(pl.program_id(1) != 0)
    def _():
        chunks(False)


def _ffn_in(x, mods, g, w_in, w_out, lead, which, tm=2048, tn=512):
    nj = D_FF // tn
    n_lead = len(lead)
    wo_cols = D // WO_COL_TILES
    assert nj * tn == D_FF and T // tm >= WO_COL_TILES and tm % LAT_S == 0

    def wo_tile(i, j):
        parked = i >= WO_COL_TILES
        return jnp.where(parked, nj - 1, j), jnp.minimum(i, WO_COL_TILES - 1)

    return pl.pallas_call(
        functools.partial(_ffn_in_kernel, which=which, tm=tm),
        grid=(T // tm, nj),
        in_specs=[
            pl.BlockSpec((tm, D), lambda i, j: (i, 0), pipeline_mode=pl.Buffered(1)),
            pl.BlockSpec((N_GROUPS, N_MOD, D), lambda i, j: (0, 0, 0)),
            pl.BlockSpec((1, D), lambda i, j: (0, 0)),
            pl.BlockSpec((None,) * n_lead + (D, tn), lambda i, j: lead + (0, j)),
            pl.BlockSpec((None,) * n_lead + (D, tn), lambda i, j: lead + (0, j + nj)),
            pl.BlockSpec((None,) * n_lead + (tn, wo_cols), lambda i, j: lead + wo_tile(i, j)),
        ],
        out_specs=[pl.BlockSpec((tm, tn), lambda i, j: (i, j)),
                   pl.BlockSpec((tn, wo_cols), wo_tile)],
        out_shape=[jax.ShapeDtypeStruct((T, D_FF), BF16), jax.ShapeDtypeStruct((D_FF, D), BF16)],
        scratch_shapes=[pltpu.VMEM((tm, D), BF16)],
        compiler_params=_cparams(2),
        name="ffn_in",
    )(x, mods, g, w_in, w_in, w_out)


def _mm_residual_kernel(a_ref, w_ref, x_ref, mod_ref, o_ref, *, gate_row, coef, tm):
    gate = coef * mod_ref[0, gate_row:gate_row + 1, :]
    for r in range(0, tm, M_CHUNK):
        acc = jnp.dot(a_ref[r:r + M_CHUNK, :], w_ref[...], preferred_element_type=F32)
        o_ref[r:r + M_CHUNK, :] = x_ref[r:r + M_CHUNK, :] + gate * acc


def _mm_residual(a, w, lead, x, mods, gate_row, coef, tm, tn, name, tile0=0, n_tiles=None):
    k = a.shape[1]
    return pl.pallas_call(
        functools.partial(_mm_residual_kernel, gate_row=gate_row, coef=coef, tm=tm),
        grid=(n_tiles, D // tn),
        in_specs=[
            pl.BlockSpec((tm, k), lambda i, j: (tile0 + i, 0)),
            _weight_spec(lead, k, tn),
            pl.BlockSpec((tm, tn), lambda i, j: (tile0 + i, j)),
            pl.BlockSpec((1, N_MOD, tn), lambda i, j: (_group_of_tile(tile0 + i, tm), 0, j)),
        ],
        out_specs=pl.BlockSpec((tm, tn), lambda i, j: (i, j)),
        out_shape=jax.ShapeDtypeStruct((n_tiles * tm, D), F32),
        compiler_params=_cparams(2),
        name=name,
    )(a, w, x, mods)


def _out_proj_kernel(*refs, widths, n_ctx_tiles, tm):
    n_a = len(widths)
    ctx_refs, lat_refs = refs[:n_a], refs[n_a:2 * n_a]
    w_ref, x_ref, mod_ref, o_ref = refs[2 * n_a:]
    gate = mod_ref[0, 5:6, :]

    def run(a_refs):
        for r in range(0, tm, M_CHUNK):
            acc = None
            k0 = 0
            for a_ref, kw in zip(a_refs, widths):
                part = jnp.dot(a_ref[r:r + M_CHUNK, :], w_ref[k0:k0 + kw, :], preferred_element_type=F32)
                acc = part if acc is None else acc + part
                k0 += kw
            o_ref[r:r + M_CHUNK, :] = x_ref[r:r + M_CHUNK, :] + gate * acc

    @pl.when(pl.program_id(0) < n_ctx_tiles)
    def _():
        run(ctx_refs)

    @pl.when(pl.program_id(0) >= n_ctx_tiles)
    def _():
        run(lat_refs)


def _out_proj(a_ctx, a_lat, w, lead, x, mods, tm=512):
    widths = tuple(a.shape[1] for a in a_ctx)
    n_ctx = T_CTX // tm
    specs = [pl.BlockSpec((tm, kw), lambda i: (jnp.minimum(i, n_ctx - 1), 0)) for kw in widths]
    specs += [pl.BlockSpec((tm, kw), lambda i: (jnp.maximum(i - n_ctx, 0), 0)) for kw in widths]
    return pl.pallas_call(
        functools.partial(_out_proj_kernel, widths=widths, n_ctx_tiles=n_ctx, tm=tm),
        grid=(T // tm,),
        in_specs=specs + [
            pl.BlockSpec((None,) * len(lead) + (D, D), lambda i: lead + (0, 0), pipeline_mode=pl.Buffered(1)),
            pl.BlockSpec((tm, D), lambda i: (i, 0)),
            pl.BlockSpec((1, N_MOD, D), lambda i: (_group_of_tile(i, tm), 0, 0)),
        ],
        out_specs=pl.BlockSpec((tm, D), lambda i: (i, 0)),
        out_shape=jax.ShapeDtypeStruct((T, D), F32),
        compiler_params=_cparams(1),
        name="mixer_out_proj",
    )(*a_ctx, *a_lat, w, x, mods)


def _rope(x, cos, sin, nf):
    lane = lax.broadcasted_iota(jnp.int32, x.shape, 1)
    first = (lane & (2 * nf - 1)) < nf
    partner = jnp.where(first, pltpu.roll(x, 128 - nf, 1), pltpu.roll(x, nf, 1))
    return x * cos + partner * sin


def _rms(x, g, n):
    inv = lax.rsqrt(jnp.sum(x * x, axis=-1, keepdims=True) * (1.0 / n) + EPS)
    return x * inv * g


def _mla_keys_values(ckvn, kr, wk_ref, wv_ref, gkn_ref, gkr_ref, cos, sin, km_ref, vm_ref):
    c16 = ckvn.astype(BF16)
    kn = jnp.dot(c16, wk_ref[...], preferred_element_type=F32)
    vm_ref[...] = jnp.dot(c16, wv_ref[...], preferred_element_type=F32).astype(BF16)
    kr_ss = jnp.sum(kr * kr, axis=-1, keepdims=True)
    for h in range(HA):
        x = kn[:, h * NOPE:(h + 1) * NOPE]
        inv = lax.rsqrt((jnp.sum(x * x, axis=-1, keepdims=True) + kr_ss) * (1.0 / QK) + EPS)
        km_ref[:, h * QK_PAD:h * QK_PAD + NOPE] = (x * inv * gkn_ref[...]).astype(BF16)
        r = kr * inv * gkr_ref[...]
        if cos is not None:
            r = _rope(r, cos, sin, ROPE // 4)
        km_ref[:, h * QK_PAD + NOPE:(h + 1) * QK_PAD] = r.astype(BF16)


def _even_in_kernel(x_ref, mod_ref, g_ref, w_ref, wq_ref, wk_ref, wv_ref, qn_ref, kvn_ref, gq_ref, gkn_ref,
                    gkr_ref, gnaq_ref, gnak_ref, cos_ref, sin_ref,
                    qm_ref, km_ref, vm_ref, qna_ref, kna_ref, vna_ref,
                    ckv_ref, kr_ref, knaf_ref, vnaf_ref, *, n_ctx_tiles):
    is_ctx = pl.program_id(0) < n_ctx_tiles
    cos = cos_ref[...]
    sin = sin_ref[...]
    gain, shift = _mod_rows(mod_ref, g_ref, 1)
    h = _modulate(x_ref[...], gain, shift)
    proj = lambda lo, hi: jnp.dot(h, w_ref[:, lo:hi], preferred_element_type=F32)
    base = Q_LORA + KV_LORA

    low = proj(0, base)
    kr = proj(IN_EVEN_PAD - 128, IN_EVEN_PAD)
    cq = _rms(low[:, :Q_LORA], qn_ref[...], Q_LORA)
    q = jnp.dot(cq.astype(BF16), wq_ref[...], preferred_element_type=F32)
    for hd in range(HA):
        y = _rms(q[:, hd * QK_PAD:(hd + 1) * QK_PAD], gq_ref[...], QK)
        qm_ref[:, hd * QK_PAD:hd * QK_PAD + NOPE] = y[:, :NOPE].astype(BF16)
        qm_ref[:, hd * QK_PAD + NOPE:(hd + 1) * QK_PAD] = _rope(y[:, NOPE:], cos, sin, ROPE // 4).astype(BF16)
    ckvn = _rms(low[:, Q_LORA:], kvn_ref[...], KV_LORA)
    _mla_keys_values(ckvn, kr, wk_ref, wv_ref, gkn_ref, gkr_ref, cos, sin, km_ref, vm_ref)

    qn = proj(base, base + HB * HEAD)
    for hd in range(HB):
        qna_ref[:, hd * HEAD:(hd + 1) * HEAD] = _rms(qn[:, hd * HEAD:(hd + 1) * HEAD], gnaq_ref[...],
                                                     HEAD).astype(BF16)
    kn = proj(base + HB * HEAD, base + 2 * HB * HEAD)
    k_heads = []
    for hd in range(HB):
        kh = _rms(kn[:, hd * HEAD:(hd + 1) * HEAD], gnak_ref[...], HEAD)
        kna_ref[:, hd * HEAD:(hd + 1) * HEAD] = kh.astype(BF16)
        k_heads.append(kh)
    vn = proj(base + 2 * HB * HEAD, base + 3 * HB * HEAD)
    vna_ref[...] = vn.astype(BF16)

    @pl.when(is_ctx)
    def _():
        ckv_ref[...] = ckvn
        kr_ref[...] = kr[:, :ROPE]
        vnaf_ref[...] = vn
        for hd, kh in enumerate(k_heads):
            knaf_ref[:, hd * HEAD:(hd + 1) * HEAD] = kh


def _even_in(x, mods, g, w, tm=256):
    n_ctx = T_CTX // tm
    resident = lambda a: pl.BlockSpec(a.shape, lambda i: (0, 0), pipeline_mode=pl.Buffered(1))
    small = lambda a: pl.BlockSpec(a.shape, lambda i: (0, 0))
    rows = lambda n: pl.BlockSpec((tm, n), lambda i: (i, 0))
    ctx_rows = lambda n: pl.BlockSpec((tm, n), lambda i: (jnp.minimum(i, n_ctx - 1), 0))
    table = pl.BlockSpec((tm, 128), lambda i: (_rope_block_of_tile(i, tm), 0))
    tok = lambda n, dt: jax.ShapeDtypeStruct((T, n), dt)
    ctx = lambda n: jax.ShapeDtypeStruct((T_CTX, n), F32)
    cos, sin = _rope_tables(ROPE)
    norms = [w[k] for k in ("q_norm", "kv_norm", "gq", "gkn", "gkr", "gnaq", "gnak")]
    return pl.pallas_call(
        functools.partial(_even_in_kernel, n_ctx_tiles=n_ctx),
        grid=(T // tm,),
        in_specs=[rows(D), pl.BlockSpec((1, N_MOD, D), lambda i: (_group_of_tile(i, tm), 0, 0)), small(g),
                  resident(w["w_in"]), resident(w["wq"]), resident(w["wk"]), resident(w["wv"])]
                 + [small(a) for a in norms] + [table, table],
        out_specs=[rows(HA * QK_PAD), rows(HA * QK_PAD), rows(HA * HEAD), rows(HB * HEAD), rows(HB * HEAD),
                   rows(HB * HEAD), ctx_rows(KV_LORA), ctx_rows(ROPE), ctx_rows(HB * HEAD), ctx_rows(HB * HEAD)],
        out_shape=[tok(HA * QK_PAD, BF16), tok(HA * QK_PAD, BF16), tok(HA * HEAD, BF16), tok(HB * HEAD, BF16),
                   tok(HB * HEAD, BF16), tok(HB * HEAD, BF16), ctx(KV_LORA), ctx(ROPE), ctx(HB * HEAD),
                   ctx(HB * HEAD)],
        compiler_params=_cparams(1),
        name="even_in",
    )(x, mods, g, w["w_in"], w["wq"], w["wk"], w["wv"], *norms, cos, sin)


def _cache_kv_kernel(ckv_ref, kr_ref, wk_ref, wv_ref, gkn_ref, gkr_ref, km_ref, vm_ref):
    _mla_keys_values(ckv_ref[...], kr_ref[...], wk_ref, wv_ref, gkn_ref, gkr_ref, None, None, km_ref, vm_ref)


def _cache_kv(ckv, kr, wk, wv, gkn, gkr, tm=512):
    n = ckv.shape[0]
    full = lambda shape: pl.BlockSpec(shape, lambda i: (0, 0))
    rows = lambda w: pl.BlockSpec((tm, w), lambda i: (i, 0))
    return pl.pallas_call(
        _cache_kv_kernel,
        grid=(n // tm,),
        in_specs=[rows(KV_LORA), rows(128), full(wk.shape), full(wv.shape), full(gkn.shape), full(gkr.shape)],
        out_specs=[rows(HA * QK_PAD), rows(HA * HEAD)],
        out_shape=[jax.ShapeDtypeStruct((n, HA * QK_PAD), BF16), jax.ShapeDtypeStruct((n, HA * HEAD), BF16)],
        compiler_params=_cparams(1),
        name="mla_cache_kv",
    )(ckv, kr, wk, wv, gkn, gkr)


def _odd_in_kernel(x_ref, mod_ref, g_ref, w_ref, gq_ref, gk_ref, cos_ref, sin_ref,
                   q_ref, k_ref, v_ref, kf_ref, vf_ref, *, n_ctx_tiles, tm):
    is_ctx = pl.program_id(0) < n_ctx_tiles
    gain, shift = _mod_rows(mod_ref, g_ref, 1)
    k0 = HC * HEAD
    v0 = k0 + KVH_C * HEAD
    cache_rows = []
    for r in range(0, tm, M_CHUNK):
        rows = slice(r, r + M_CHUNK)
        cos = cos_ref[rows, :]
        sin = sin_ref[rows, :]
        h = _modulate(x_ref[rows, :], gain, shift)
        q = jnp.dot(h, w_ref[:, :k0], preferred_element_type=F32)
        for hd in range(HC):
            y = _rms(q[:, hd * HEAD:(hd + 1) * HEAD], gq_ref[...], HEAD)
            q_ref[rows, hd * HEAD:(hd + 1) * HEAD] = _rope(y, cos, sin, HEAD // 4).astype(BF16)
        k = jnp.dot(h, w_ref[:, k0:v0], preferred_element_type=F32)
        k_heads = []
        for hd in range(KVH_C):
            y = _rms(k[:, hd * HEAD:(hd + 1) * HEAD], gk_ref[...], HEAD)
            k_ref[rows, hd * HEAD:(hd + 1) * HEAD] = _rope(y, cos, sin, HEAD // 4).astype(BF16)
            k_heads.append(y)
        v = jnp.dot(h, w_ref[:, v0:], preferred_element_type=F32)
        v_ref[rows, :] = v.astype(BF16)
        cache_rows.append((rows, k_heads, v))

    @pl.when(is_ctx)
    def _():
        for rows, k_heads, v in cache_rows:
            vf_ref[rows, :] = v
            for hd, y in enumerate(k_heads):
                kf_ref[rows, hd * HEAD:(hd + 1) * HEAD] = y


def _odd_in(x, mods, g, w, tm=512):
    n_ctx = T_CTX // tm
    small = lambda a: pl.BlockSpec(a.shape, lambda i: (0, 0))
    rows = lambda n: pl.BlockSpec((tm, n), lambda i: (i, 0))
    ctx_rows = lambda n: pl.BlockSpec((tm, n), lambda i: (jnp.minimum(i, n_ctx - 1), 0))
    table = pl.BlockSpec((tm, 128), lambda i: (_rope_block_of_tile(i, tm), 0))
    tok = lambda n, dt: jax.ShapeDtypeStruct((T, n), dt)
    ctx = lambda n: jax.ShapeDtypeStruct((T_CTX, n), F32)
    cos, sin = _rope_tables(HEAD)
    return pl.pallas_call(
        functools.partial(_odd_in_kernel, n_ctx_tiles=n_ctx, tm=tm),
        grid=(T // tm,),
        in_specs=[rows(D), pl.BlockSpec((1, N_MOD, D), lambda i: (_group_of_tile(i, tm), 0, 0)), small(g),
                  pl.BlockSpec(w["w_in"].shape, lambda i: (0, 0), pipeline_mode=pl.Buffered(1)),
                  small(w["gq"]), small(w["gk"]), table, table],
        out_specs=[rows(HC * HEAD), rows(KVH_C * HEAD), rows(KVH_C * HEAD), ctx_rows(KVH_C * HEAD),
                   ctx_rows(KVH_C * HEAD)],
        out_shape=[tok(HC * HEAD, BF16), tok(KVH_C * HEAD, BF16), tok(KVH_C * HEAD, BF16), ctx(KVH_C * HEAD),
                   ctx(KVH_C * HEAD)],
        compiler_params=_cparams(1),
        name="odd_in",
    )(x, mods, g, w["w_in"], w["gq"], w["gk"], cos, sin)


def _build_na_bias(toep_ref, bias_scr, hps):
    neg = jnp.full((GRID_W, GRID_W), NEG_INF, F32)
    for hh in range(hps):
        for r in range(GRID_H):
            rs = min(max(r - NA_KH // 2, 0), GRID_H - NA_KH)
            for kr in range(GRID_H):
                tile = toep_ref[hh, kr - r + NA_KH - 1] if rs <= kr < rs + NA_KH else neg
                bias_scr[hh, r * GRID_W:(r + 1) * GRID_W, kr * GRID_W:(kr + 1) * GRID_W] = tile


def _attn_kernel(*refs, plan, n_kv, bias_mode, has_sink, hps, groups, dk, dv, qb, c):
    it = iter(refs)
    q_ref = next(it)
    kv_refs = [(next(it), next(it)) for _ in range(n_kv)]
    bias_ref = next(it) if bias_mode is not None else None
    sink_ref = next(it) if has_sink else None
    o_ref = next(it)
    bias_scr = next(it) if bias_mode == "na" else None
    hblk = pl.program_id(0)

    if bias_mode == "na":
        @pl.when(pl.program_id(1) == 0)
        def _():
            _build_na_bias(bias_ref, bias_scr, hps)

    def one_head(hh, head, q_cols, o_cols):
        sink = sink_ref[hblk * hps * groups + head] if has_sink else None
        for qi, segs in enumerate(plan):
            q0 = qi * qb
            q = q_ref[q0:q0 + qb, q_cols]
            scores = []
            for (si, start, length, biased) in segs:
                k = kv_refs[si][0][start:start + length, hh * dk:(hh + 1) * dk].astype(BF16)
                t = lax.dot_general(q, k, _NT, preferred_element_type=F32) * c
                if biased and bias_mode == "na":
                    t = t + bias_scr[hh, q0:q0 + qb, start:start + length]
                elif biased:
                    t = t + bias_ref[qi, :, :length]
                scores.append(t)
            m = jnp.max(scores[0], axis=-1, keepdims=True)
            for t in scores[1:]:
                m = jnp.maximum(m, jnp.max(t, axis=-1, keepdims=True))
            if has_sink:
                m = jnp.maximum(m, sink)
            acc = None
            for t, (si, start, length, _) in zip(scores, segs):
                p = jnp.exp2(t - m).astype(BF16)
                v = kv_refs[si][1][start:start + length, hh * dv:(hh + 1) * dv].astype(BF16)
                v1 = jnp.concatenate([v, jnp.ones((length, dv), BF16)], axis=-1)
                pv = jnp.dot(p, v1, preferred_element_type=F32)
                acc = pv if acc is None else acc + pv
            denom = acc[:, dv:]
            if has_sink:
                denom = denom + jnp.exp2(sink - m)
            o_ref[q0:q0 + qb, o_cols] = (acc[:, :dv] / denom).astype(o_ref.dtype)

    for hh in range(hps):
        if groups > 1 and len(plan) > 1:
            def group_body(g, carry, hh=hh):
                head = hh * groups + g
                one_head(hh, head, pl.ds(pl.multiple_of(head * dk, dk), dk), pl.ds(pl.multiple_of(head * dv, dv), dv))
                return carry

            lax.fori_loop(0, groups, group_body, 0, unroll=2)
        else:
            for g in range(groups):
                head = hh * groups + g
                one_head(hh, head, slice(head * dk, (head + 1) * dk), slice(head * dv, (head + 1) * dv))


def _attention(q, kvs, plan, *, nb, sq, q_rowblk0, n_kv_heads, hps, groups, dk, dv, scale,
               bias=None, bias_mode=None, sink=None, name="attention"):
    grid = (n_kv_heads // hps, nb)
    in_specs = [pl.BlockSpec((sq, hps * groups * dk), lambda h, b: (q_rowblk0 + b, h))]
    args = [q]
    for (k, v, sk, r0) in kvs:
        in_specs.append(pl.BlockSpec((sk, hps * dk), lambda h, b, r0=r0: (r0 + b, h)))
        in_specs.append(pl.BlockSpec((sk, hps * dv), lambda h, b, r0=r0: (r0 + b, h)))
        args += [k, v]
    scratch = []
    if bias_mode == "na":
        in_specs.append(pl.BlockSpec((hps,) + bias.shape[1:], lambda h, b: (h, 0, 0, 0)))
        args.append(bias)
        scratch.append(pltpu.VMEM((hps, LAT_S, LAT_S), F32))
    elif bias_mode == "table":
        in_specs.append(pl.BlockSpec(bias.shape, lambda h, b: (0, 0, 0)))
        args.append(bias)
    if sink is not None:
        in_specs.append(pl.BlockSpec(memory_space=pltpu.SMEM))
        args.append(sink)
    body = functools.partial(
        _attn_kernel, plan=plan, n_kv=len(kvs), bias_mode=bias_mode, has_sink=sink is not None,
        hps=hps, groups=groups, dk=dk, dv=dv, qb=min(Q_BLOCK, sq), c=scale * LOG2E)
    return pl.pallas_call(
        body,
        grid=grid,
        in_specs=in_specs,
        out_specs=pl.BlockSpec((sq, hps * groups * dv), lambda h, b: (b, h)),
        out_shape=jax.ShapeDtypeStruct((nb * sq, n_kv_heads * groups * dv), BF16),
        scratch_shapes=scratch,
        compiler_params=_cparams(2),
        name=name,
    )(*args)


_CTX_PLAN = [[(0, 0, CTX_S, False)]]
_N_QB = LAT_S // Q_BLOCK
_MLA_PLAN = [[(0, 0, LAT_S, False), (1, 0, PAST, False)] for _ in range(_N_QB)]
_NA_ROWS = [(0, 8), (0, 12), (4, 16), (8, 16)]
_NA_PLAN = [[(0, lo * GRID_W, (hi - lo) * GRID_W, True), (1, 0, PAST, False)] for lo, hi in _NA_ROWS]
_WIN_SPANS = [(max(i * Q_BLOCK - WINDOW, 0), min((i + 1) * Q_BLOCK + WINDOW, LAT_S)) for i in range(_N_QB)]
_WIN_PLAN = [[(0, lo, hi - lo, True), (1, 0, PAST, False)] for lo, hi in _WIN_SPANS]


def _window_bias():
    width = max(hi - lo for lo, hi in _WIN_SPANS)
    out = np.full((_N_QB, Q_BLOCK, width), NEG_INF, np.float32)
    for i, (lo, hi) in enumerate(_WIN_SPANS):
        q_abs = i * Q_BLOCK + np.arange(Q_BLOCK)[:, None]
        k_abs = lo + np.arange(hi - lo)[None, :]
        out[i, :, :hi - lo] = np.where(np.abs(q_abs - k_abs) <= WINDOW, 0.0, NEG_INF)
    return jnp.asarray(out)


def _rope_tables(rot_dim):
    half = rot_dim // 2
    nf = half // 2
    t = np.arange(LAT_S)
    inv_freq = ROPE_THETA ** (-np.arange(nf, dtype=np.float64) / nf)
    cos = np.zeros((2 * LAT_S, 128), np.float64)
    sin = np.zeros((2 * LAT_S, 128), np.float64)
    cos[:LAT_S, :rot_dim] = 1.0
    for part, pos in enumerate((t // GRID_W, t % GRID_W)):
        ang = pos[:, None].astype(np.float64) * inv_freq[None, :]
        lo = part * half
        cos[LAT_S:, lo:lo + nf] = np.cos(ang)
        cos[LAT_S:, lo + nf:lo + half] = np.cos(ang)
        sin[LAT_S:, lo:lo + nf] = -np.sin(ang)
        sin[LAT_S:, lo + nf:lo + half] = np.sin(ang)
    return jnp.asarray(cos, F32), jnp.asarray(sin, F32)


def _na_toeplitz(rpb):
    c = np.arange(GRID_W)
    ws = np.clip(c - NA_KW // 2, 0, GRID_W - NA_KW)
    col_valid = (c[None, :] >= ws[:, None]) & (c[None, :] < ws[:, None] + NA_KW)
    n_off, n_rel = rpb.shape[1], rpb.shape[2]
    width = 2 * GRID_W
    lead = GRID_W - NA_KW
    u = jnp.pad(rpb * LOG2E, ((0, 0), (0, 0), (lead, width - lead - n_rel)))
    a = jnp.broadcast_to(u[:, :, None, :], (HB, n_off, GRID_W, width)).reshape(HB, n_off, GRID_W * width)
    a = jnp.pad(a, ((0, 0), (0, 0), (0, GRID_W))).reshape(HB, n_off, GRID_W, width + 1)
    t = a[:, :, ::-1, :GRID_W]
    return jnp.where(jnp.asarray(col_valid)[None, None], t, NEG_INF)


def _even_mixer(x, mods, g, w, caches):
    cache_ckv, cache_krope, cache_nak, cache_nav = caches
    qm, km, vm, qna, kna, vna, new_ckv, new_kr, new_nak, new_nav = _even_in(x, mods, g, w)
    km_ctx, vm_ctx = _cache_kv(cache_ckv, cache_krope, w["wk"], w["wv"], w["gkn"], w["gkr"])
    s_mla = 1.0 / math.sqrt(QK)
    s_na = 1.0 / math.sqrt(HEAD)
    lat0 = T_CTX // LAT_S
    o_mla_ctx = _attention(qm, [(km, vm, CTX_S, 0)], _CTX_PLAN, nb=CTX_B, sq=CTX_S, q_rowblk0=0, n_kv_heads=HA,
                           hps=HA, groups=1, dk=QK_PAD, dv=HEAD, scale=s_mla, name="mla_ctx")
    o_na_ctx = _attention(qna, [(kna, vna, CTX_S, 0)], _CTX_PLAN, nb=CTX_B, sq=CTX_S, q_rowblk0=0, n_kv_heads=HB,
                          hps=HB, groups=1, dk=HEAD, dv=HEAD, scale=s_na, name="na_ctx")
    o_mla_lat = _attention(qm, [(km, vm, LAT_S, lat0), (km_ctx, vm_ctx, PAST, 0)], _MLA_PLAN, nb=LAT_B, sq=LAT_S,
                           q_rowblk0=lat0, n_kv_heads=HA, hps=4, groups=1, dk=QK_PAD, dv=HEAD, scale=s_mla,
                           name="mla_lat")
    o_na_lat = _attention(qna, [(kna, vna, LAT_S, lat0), (cache_nak, cache_nav, PAST, 0)], _NA_PLAN, nb=LAT_B,
                          sq=LAT_S, q_rowblk0=lat0, n_kv_heads=HB, hps=2, groups=1, dk=HEAD, dv=HEAD, scale=s_na,
                          bias=w["na_toep"], bias_mode="na", name="na_lat")
    x = _out_proj([o_mla_ctx, o_na_ctx], [o_mla_lat, o_na_lat], w["w_out"], w["w_out_lead"], x, mods)
    return x, (new_ckv.reshape(CTX_B, CTX_S, KV_LORA), new_kr.reshape(CTX_B, CTX_S, ROPE),
               new_nak.reshape(CTX_B, CTX_S, HB, HEAD), new_nav.reshape(CTX_B, CTX_S, HB, HEAD))


def _odd_mixer(x, mods, g, w, caches):
    cache_k, cache_v = caches
    q, k, v, new_k, new_v = _odd_in(x, mods, g, w)
    scale = 1.0 / math.sqrt(HEAD)
    lat0 = T_CTX // LAT_S
    o_ctx = _attention(q, [(k, v, CTX_S, 0)], _CTX_PLAN, nb=CTX_B, sq=CTX_S, q_rowblk0=0, n_kv_heads=KVH_C,
                       hps=KVH_C, groups=GROUPS_C, dk=HEAD, dv=HEAD, scale=scale, sink=w["sink"], name="gqa_ctx")
    o_lat = _attention(q, [(k, v, LAT_S, lat0), (cache_k, cache_v, PAST, 0)], _WIN_PLAN, nb=LAT_B, sq=LAT_S,
                       q_rowblk0=lat0, n_kv_heads=KVH_C, hps=1, groups=GROUPS_C, dk=HEAD, dv=HEAD, scale=scale,
                       bias=_window_bias(), bias_mode="table", sink=w["sink"], name="gqa_lat")
    x = _out_proj([o_ctx], [o_lat], w["w_out"], w["w_out_lead"], x, mods)
    return x, (new_k.reshape(CTX_B, CTX_S, KVH_C, HEAD), new_v.reshape(CTX_B, CTX_S, KVH_C, HEAD))


def _even_weights(e, even_w_in, w_out_bf16, mla_q_norm, mla_w_q_up, mla_kv_norm, mla_w_kv_up, mla_qk_norm,
                  na_qk_norm, na_rpb):
    w_in = even_w_in[e]
    i0, i1, i2 = Q_LORA, Q_LORA + KV_LORA, Q_LORA + KV_LORA + ROPE
    w_in = jnp.concatenate([w_in[:, :i1], w_in[:, i2:], w_in[:, i1:i2], jnp.zeros((D, 128 - ROPE), F32)], axis=1)
    wq = jnp.pad(mla_w_q_up[e].reshape(Q_LORA, HA, QK), ((0, 0), (0, 0), (0, QK_PAD - QK)))
    wkv = mla_w_kv_up[e].reshape(KV_LORA, HA, NOPE + HEAD)
    qk = mla_qk_norm[e]
    return {
        "w_in": w_in.astype(BF16),
        "w_out": w_out_bf16,
        "w_out_lead": (e,),
        "wq": wq.reshape(Q_LORA, HA * QK_PAD).astype(BF16),
        "wk": wkv[:, :, :NOPE].reshape(KV_LORA, HA * NOPE).astype(BF16),
        "wv": wkv[:, :, NOPE:].reshape(KV_LORA, HA * HEAD).astype(BF16),
        "q_norm": mla_q_norm[e][None, :],
        "kv_norm": mla_kv_norm[e][None, :],
        "gq": jnp.pad(qk[0], (0, QK_PAD - QK))[None, :],
        "gkn": qk[1, :NOPE][None, :],
        "gkr": jnp.pad(qk[1, NOPE:], (0, 128 - ROPE))[None, :],
        "gnaq": na_qk_norm[e, 0][None, :],
        "gnak": na_qk_norm[e, 1][None, :],
        "na_toep": _na_toeplitz(na_rpb[e]),
    }


def kernel(x_prompt, x_sample, cache_mla_ckv, cache_mla_krope, cache_na_k, cache_na_v, cache_gqa_k, cache_gqa_v, c, c_ctx, ada_w, ada_b, norm_g, ffn_w_in, ffn_w_out, even_w_in, even_w_out, mla_q_norm, mla_w_q_up, mla_kv_norm, mla_w_kv_up, mla_qk_norm, na_qk_norm, na_rpb, odd_w_in, odd_w_out, gqa_qk_norm, gqa_sink):
    depth = ada_w.shape[0]
    x = jnp.concatenate([x_prompt.reshape(T_CTX, D), x_sample.reshape(T_LAT, D)], axis=0)
    cond = jnp.concatenate([c_ctx[None, :], c, jnp.zeros((16 - N_GROUPS, D), F32)], axis=0)
    mods_all = _ada_modulation(cond, ada_w, ada_b)
    even_w_out16 = even_w_out.astype(BF16)
    odd_w_out16 = odd_w_out.astype(BF16)
    n_ctx_tiles = T_CTX // FFN_OUT_TM

    ckv_l, kr_l, nak_l, nav_l, gk_l, gv_l = [], [], [], [], [], []
    y_prompt = y_sample = None
    for layer in range(depth):
        mods = mods_all[layer, :N_GROUPS].reshape(N_GROUPS, N_MOD, D)
        g = norm_g[layer]
        e = layer // 2
        act, w_out16 = _ffn_in(x, mods, g[0:1], ffn_w_in, ffn_w_out, (layer, 0), 0)
        x = _mm_residual(act, w_out16, (), x, mods, 2, 0.5, tm=FFN_OUT_TM, tn=D, name="ffn_out",
                         n_tiles=T // FFN_OUT_TM)
        if layer % 2 == 0:
            w = _even_weights(e, even_w_in, even_w_out16, mla_q_norm, mla_w_q_up, mla_kv_norm, mla_w_kv_up,
                              mla_qk_norm, na_qk_norm, na_rpb)
            caches = (cache_mla_ckv[:, e].reshape(LAT_B * PAST, KV_LORA),
                      jnp.pad(cache_mla_krope[:, e].reshape(LAT_B * PAST, ROPE), ((0, 0), (0, 128 - ROPE))),
                      cache_na_k[:, e].reshape(LAT_B * PAST, HB * HEAD),
                      cache_na_v[:, e].reshape(LAT_B * PAST, HB * HEAD))
            x, (ckv, kr, nak, nav) = _even_mixer(x, mods, g[1:2], w, caches)
            ckv_l.append(ckv)
            kr_l.append(kr)
            nak_l.append(nak)
            nav_l.append(nav)
        else:
            w = {"w_in": odd_w_in[e].astype(BF16), "w_out": odd_w_out16, "w_out_lead": (e,),
                 "gq": gqa_qk_norm[e, 0][None, :], "gk": gqa_qk_norm[e, 1][None, :],
                 "sink": gqa_sink[e] * LOG2E}
            caches = (cache_gqa_k[:, e].reshape(LAT_B * PAST, KVH_C * HEAD),
                      cache_gqa_v[:, e].reshape(LAT_B * PAST, KVH_C * HEAD))
            x, (gk, gv) = _odd_mixer(x, mods, g[1:2], w, caches)
            gk_l.append(gk)
            gv_l.append(gv)
        act, w_out16 = _ffn_in(x, mods, g[2:3], ffn_w_in, ffn_w_out, (layer, 1), 2)
        ffn_out = functools.partial(_mm_residual, act, w_out16, (), x, mods, 8, 0.5, tm=FFN_OUT_TM, tn=D,
                                    name="ffn_out")
        if layer + 1 < depth:
            x = ffn_out(n_tiles=T // FFN_OUT_TM)
        else:
            y_prompt = ffn_out(tile0=0, n_tiles=n_ctx_tiles).reshape(CTX_B, CTX_S, D)
            y_sample = ffn_out(tile0=n_ctx_tiles, n_tiles=T // FFN_OUT_TM - n_ctx_tiles).reshape(LAT_B, LAT_S, D)

    return (y_prompt, y_sample, jnp.stack(ckv_l, axis=1), jnp.stack(kr_l, axis=1), jnp.stack(nak_l, axis=1),
            jnp.stack(nav_l, axis=1), jnp.stack(gk_l, axis=1), jnp.stack(gv_l, axis=1))
```

```python
import functools
import math

import numpy as np
import jax
import jax.numpy as jnp
from jax import lax
from jax.experimental import pallas as pl
from jax.experimental.pallas import tpu as pltpu

F32 = jnp.float32
BF16 = jnp.bfloat16

D = 2048
D_FF = 5632
N_MOD = 9
CTX_B, CTX_S = 16, 256
LAT_B, LAT_S = 8, 1024
PAST = 512
T_CTX = CTX_B * CTX_S
T_LAT = LAT_B * LAT_S
T = T_CTX + T_LAT
GRID_W = 64
GRID_H = LAT_S // GRID_W
N_GROUPS = 1 + LAT_B
HEAD = 128
HA = 8
NOPE, ROPE = 128, 64
QK = NOPE + ROPE
QK_PAD = 256
Q_LORA = 512
KV_LORA = 512
HB = 8
NA_KH, NA_KW = 8, 16
HC, KVH_C = 16, 4
GROUPS_C = HC // KVH_C
WINDOW = 128
IN_EVEN_PAD = 4224
IN_ODD = 3072
ROPE_THETA = 10000.0
EPS = 1e-6
NEG_INF = -1e30
LOG2E = math.log2(math.e)

VMEM_LIMIT = 56 * 1024 * 1024
Q_BLOCK = 256
M_CHUNK = 256
WO_COL_TILES = 4
FFN_OUT_TM = 512

_NT = (((1,), (1,)), ((), ()))


def _cparams(n_axes):
    return pltpu.CompilerParams(dimension_semantics=("arbitrary",) * n_axes,
                                vmem_limit_bytes=VMEM_LIMIT)


def _group_of_tile(i, tm):
    n_ctx = T_CTX // tm
    per = LAT_S // tm
    return jnp.where(i < n_ctx, 0, (i - n_ctx) // per + 1)


def _rope_block_of_tile(i, tm):
    n_ctx = T_CTX // tm
    per = LAT_S // tm
    return jnp.where(i < n_ctx, 0, per + (i - n_ctx) % per)


def _ada_kernel(c_ref, w_ref, b_ref, o_ref):
    c = c_ref[...]
    a = (c * jax.nn.sigmoid(c)).astype(BF16)
    o_ref[0] = jnp.dot(a, w_ref[0].astype(BF16), preferred_element_type=F32) + b_ref[0]


def _ada_modulation(cond, ada_w, ada_b, tn=1024):
    depth = ada_w.shape[0]
    n = N_MOD * D
    rows = cond.shape[0]
    return pl.pallas_call(
        _ada_kernel,
        grid=(depth, n // tn),
        in_specs=[
            pl.BlockSpec((rows, D), lambda l, j: (0, 0)),
            pl.BlockSpec((1, D, tn), lambda l, j: (l, 0, j)),
            pl.BlockSpec((1, 1, tn), lambda l, j: (l, 0, j)),
        ],
        out_specs=pl.BlockSpec((1, rows, tn), lambda l, j: (l, 0, j)),
        out_shape=jax.ShapeDtypeStruct((depth, rows, n), F32),
        compiler_params=_cparams(2),
        name="ada_modulation",
    )(cond, ada_w, ada_b.reshape(depth, 1, n))


def _mod_rows(mod_ref, g_ref, which):
    shift = mod_ref[0, 3 * which:3 * which + 1, :]
    scale = mod_ref[0, 3 * which + 1:3 * which + 2, :]
    return g_ref[...] * (1.0 + scale), shift


def _modulate(x, gain, shift):
    inv = lax.rsqrt(jnp.mean(x * x, axis=-1, keepdims=True) + EPS)
    return ((x * inv) * gain + shift).astype(BF16)


def _embed_kernel(xp_ref, xs_ref, mod_ref, g_ref, x_ref, h_ref, *, n_ctx_tiles, tm):
    gain, shift = _mod_rows(mod_ref, g_ref, 0)

    def run(src_ref):
        for r in range(0, tm, M_CHUNK):
            x = src_ref[r:r + M_CHUNK, :]
            x_ref[r:r + M_CHUNK, :] = x
            h_ref[r:r + M_CHUNK, :] = _modulate(x, gain, shift)

    @pl.when(pl.program_id(0) < n_ctx_tiles)
    def _():
        run(xp_ref)

    @pl.when(pl.program_id(0) >= n_ctx_tiles)
    def _():
        run(xs_ref)


def _embed(x_prompt, x_sample, mods, g, tm=512):
    n_ctx = T_CTX // tm
    return pl.pallas_call(
        functools.partial(_embed_kernel, n_ctx_tiles=n_ctx, tm=tm),
        grid=(T // tm,),
        in_specs=[
            pl.BlockSpec((tm, D), lambda i: (jnp.minimum(i, n_ctx - 1), 0)),
            pl.BlockSpec((tm, D), lambda i: (jnp.maximum(i - n_ctx, 0), 0)),
            pl.BlockSpec((1, N_MOD, D), lambda i: (_group_of_tile(i, tm), 0, 0)),
            pl.BlockSpec((1, D), lambda i: (0, 0)),
        ],
        out_specs=[pl.BlockSpec((tm, D), lambda i: (i, 0)), pl.BlockSpec((tm, D), lambda i: (i, 0))],
        out_shape=[jax.ShapeDtypeStruct((T, D), F32), jax.ShapeDtypeStruct((T, D), BF16)],
        compiler_params=_cparams(1),
        name="embed",
    )(x_prompt.reshape(T_CTX, D), x_sample.reshape(T_LAT, D), mods, g)


def _ffn_in_kernel(h_ref, wg_ref, wu_ref, wo_ref, o_ref, wo16_ref, *, tm):
    @pl.when(pl.program_id(0) < WO_COL_TILES)
    def _():
        wo16_ref[...] = wo_ref[...].astype(BF16)

    wg = wg_ref[...].astype(BF16)
    wu = wu_ref[...].astype(BF16)
    for r in range(0, tm, M_CHUNK):
        h = h_ref[r:r + M_CHUNK, :]
        gate = jnp.dot(h, wg, preferred_element_type=F32)
        up = jnp.dot(h, wu, preferred_element_type=F32)
        o_ref[r:r + M_CHUNK, :] = (gate * jax.nn.sigmoid(gate) * up).astype(BF16)


def _ffn_in(h, w_in, w_out, lead, tm=2048, tn=512):
    nj = D_FF // tn
    n_lead = len(lead)
    wo_cols = D // WO_COL_TILES
    assert nj * tn == D_FF and T // tm >= WO_COL_TILES

    def wo_tile(i, j):
        parked = i >= WO_COL_TILES
        return jnp.where(parked, nj - 1, j), jnp.minimum(i, WO_COL_TILES - 1)

    return pl.pallas_call(
        functools.partial(_ffn_in_kernel, tm=tm),
        grid=(T // tm, nj),
        in_specs=[
            pl.BlockSpec((tm, D), lambda i, j: (i, 0)),
            pl.BlockSpec((None,) * n_lead + (D, tn), lambda i, j: lead + (0, j)),
            pl.BlockSpec((None,) * n_lead + (D, tn), lambda i, j: lead + (0, j + nj)),
            pl.BlockSpec((None,) * n_lead + (tn, wo_cols), lambda i, j: lead + wo_tile(i, j)),
        ],
        out_specs=[pl.BlockSpec((tm, tn), lambda i, j: (i, j)),
                   pl.BlockSpec((tn, wo_cols), wo_tile)],
        out_shape=[jax.ShapeDtypeStruct((T, D_FF), BF16), jax.ShapeDtypeStruct((D_FF, D), BF16)],
        compiler_params=_cparams(2),
        name="ffn_in",
    )(h, w_in, w_in, w_out)


def _ffn_out_kernel(*refs, gate_row, next_which, tm):
    if next_which is None:
        a_ref, w_ref, x_ref, mod_ref, o_ref = refs
    else:
        a_ref, w_ref, x_ref, mod_ref, modn_ref, gn_ref, o_ref, h_ref = refs
        gain, shift = _mod_rows(modn_ref, gn_ref, next_which)
    gate = 0.5 * mod_ref[0, gate_row:gate_row + 1, :]
    for r in range(0, tm, M_CHUNK):
        acc = jnp.dot(a_ref[r:r + M_CHUNK, :], w_ref[...], preferred_element_type=F32)
        x = x_ref[r:r + M_CHUNK, :] + gate * acc
        o_ref[r:r + M_CHUNK, :] = x
        if next_which is not None:
            h_ref[r:r + M_CHUNK, :] = _modulate(x, gain, shift)


def _ffn_out(a, w, x, mods, gate_row, nxt=None, tile0=0, n_tiles=T // FFN_OUT_TM, tm=FFN_OUT_TM):
    k = a.shape[1]
    rows = lambda n: pl.BlockSpec((tm, n), lambda i: (tile0 + i, 0))
    mod_spec = pl.BlockSpec((1, N_MOD, D), lambda i: (_group_of_tile(tile0 + i, tm), 0, 0))
    in_specs = [rows(k), pl.BlockSpec((k, D), lambda i: (0, 0), pipeline_mode=pl.Buffered(1)), rows(D), mod_spec]
    args = [a, w, x, mods]
    out_specs = [pl.BlockSpec((tm, D), lambda i: (i, 0))]
    out_shape = [jax.ShapeDtypeStruct((n_tiles * tm, D), F32)]
    if nxt is not None:
        in_specs += [mod_spec, pl.BlockSpec((1, D), lambda i: (0, 0))]
        args += [nxt[0], nxt[1]]
        out_specs.append(pl.BlockSpec((tm, D), lambda i: (i, 0)))
        out_shape.append(jax.ShapeDtypeStruct((n_tiles * tm, D), BF16))
    out = pl.pallas_call(
        functools.partial(_ffn_out_kernel, gate_row=gate_row, next_which=None if nxt is None else nxt[2], tm=tm),
        grid=(n_tiles,),
        in_specs=in_specs,
        out_specs=out_specs,
        out_shape=out_shape,
        compiler_params=_cparams(1),
        name="ffn_out",
    )(*args)
    return out if nxt is not None else out[0]


def _out_proj_kernel(*refs, widths, n_ctx_tiles, tm):
    n_a = len(widths)
    ctx_refs, lat_refs = refs[:n_a], refs[n_a:2 * n_a]
    w_ref, x_ref, mod_ref, g_ref, o_ref, h_ref = refs[2 * n_a:]
    gate = mod_ref[0, 5:6, :]
    gain, shift = _mod_rows(mod_ref, g_ref, 2)

    def run(a_refs):
        for r in range(0, tm, M_CHUNK):
            acc = None
            k0 = 0
            for a_ref, kw in zip(a_refs, widths):
                part = jnp.dot(a_ref[r:r + M_CHUNK, :], w_ref[k0:k0 + kw, :], preferred_element_type=F32)
                acc = part if acc is None else acc + part
                k0 += kw
            x = x_ref[r:r + M_CHUNK, :] + gate * acc
            o_ref[r:r + M_CHUNK, :] = x
            h_ref[r:r + M_CHUNK, :] = _modulate(x, gain, shift)

    @pl.when(pl.program_id(0) < n_ctx_tiles)
    def _():
        run(ctx_refs)

    @pl.when(pl.program_id(0) >= n_ctx_tiles)
    def _():
        run(lat_refs)


def _out_proj(a_ctx, a_lat, w, lead, x, mods, g_next, tm=512):
    widths = tuple(a.shape[1] for a in a_ctx)
    n_ctx = T_CTX // tm
    specs = [pl.BlockSpec((tm, kw), lambda i: (jnp.minimum(i, n_ctx - 1), 0)) for kw in widths]
    specs += [pl.BlockSpec((tm, kw), lambda i: (jnp.maximum(i - n_ctx, 0), 0)) for kw in widths]
    return pl.pallas_call(
        functools.partial(_out_proj_kernel, widths=widths, n_ctx_tiles=n_ctx, tm=tm),
        grid=(T // tm,),
        in_specs=specs + [
            pl.BlockSpec((None,) * len(lead) + (D, D), lambda i: lead + (0, 0), pipeline_mode=pl.Buffered(1)),
            pl.BlockSpec((tm, D), lambda i: (i, 0)),
            pl.BlockSpec((1, N_MOD, D), lambda i: (_group_of_tile(i, tm), 0, 0)),
            pl.BlockSpec((1, D), lambda i: (0, 0)),
        ],
        out_specs=[pl.BlockSpec((tm, D), lambda i: (i, 0)), pl.BlockSpec((tm, D), lambda i: (i, 0))],
        out_shape=[jax.ShapeDtypeStruct((T, D), F32), jax.ShapeDtypeStruct((T, D), BF16)],
        compiler_params=_cparams(1),
        name="mixer_out_proj",
    )(*a_ctx, *a_lat, w, x, mods, g_next)


def _rope(x, cos, sin, nf):
    lane = lax.broadcasted_iota(jnp.int32, x.shape, 1)
    first = (lane & (2 * nf - 1)) < nf
    partner = jnp.where(first, pltpu.roll(x, 128 - nf, 1), pltpu.roll(x, nf, 1))
    return x * cos + partner * sin


def _rms(x, g, n):
    inv = lax.rsqrt(jnp.sum(x * x, axis=-1, keepdims=True) * (1.0 / n) + EPS)
    return x * inv * g


def _mla_keys_values(ckvn, kr, wk_ref, wv_ref, gkn_ref, gkr_ref, cos, sin, km_ref, vm_ref):
    c16 = ckvn.astype(BF16)
    kn = jnp.dot(c16, wk_ref[...], preferred_element_type=F32)
    vm_ref[...] = jnp.dot(c16, wv_ref[...], preferred_element_type=F32).astype(BF16)
    kr_ss = jnp.sum(kr * kr, axis=-1, keepdims=True)
    for h in range(HA):
        x = kn[:, h * NOPE:(h + 1) * NOPE]
        inv = lax.rsqrt((jnp.sum(x * x, axis=-1, keepdims=True) + kr_ss) * (1.0 / QK) + EPS)
        km_ref[:, h * QK_PAD:h * QK_PAD + NOPE] = (x * inv * gkn_ref[...]).astype(BF16)
        r = kr * inv * gkr_ref[...]
        if cos is not None:
            r = _rope(r, cos, sin, ROPE // 4)
        km_ref[:, h * QK_PAD + NOPE:(h + 1) * QK_PAD] = r.astype(BF16)


def _even_in_kernel(h_ref, w_ref, wq_ref, wk_ref, wv_ref, qn_ref, kvn_ref, gq_ref, gkn_ref,
                    gkr_ref, gnaq_ref, gnak_ref, cos_ref, sin_ref,
                    qm_ref, km_ref, vm_ref, qna_ref, kna_ref, vna_ref,
                    ckv_ref, kr_ref, knaf_ref, vnaf_ref, *, n_ctx_tiles):
    is_ctx = pl.program_id(0) < n_ctx_tiles
    cos = cos_ref[...]
    sin = sin_ref[...]
    h = h_ref[...]
    proj = lambda lo, hi: jnp.dot(h, w_ref[:, lo:hi], preferred_element_type=F32)
    base = Q_LORA + KV_LORA

    low = proj(0, base)
    kr = proj(IN_EVEN_PAD - 128, IN_EVEN_PAD)
    cq = _rms(low[:, :Q_LORA], qn_ref[...], Q_LORA)
    q = jnp.dot(cq.astype(BF16), wq_ref[...], preferred_element_type=F32)
    for hd in range(HA):
        y = _rms(q[:, hd * QK_PAD:(hd + 1) * QK_PAD], gq_ref[...], QK)
        qm_ref[:, hd * QK_PAD:hd * QK_PAD + NOPE] = y[:, :NOPE].astype(BF16)
        qm_ref[:, hd * QK_PAD + NOPE:(hd + 1) * QK_PAD] = _rope(y[:, NOPE:], cos, sin, ROPE // 4).astype(BF16)
    ckvn = _rms(low[:, Q_LORA:], kvn_ref[...], KV_LORA)
    _mla_keys_values(ckvn, kr, wk_ref, wv_ref, gkn_ref, gkr_ref, cos, sin, km_ref, vm_ref)

    qn = proj(base, base + HB * HEAD)
    for hd in range(HB):
        qna_ref[:, hd * HEAD:(hd + 1) * HEAD] = _rms(qn[:, hd * HEAD:(hd + 1) * HEAD], gnaq_ref[...],
                                                     HEAD).astype(BF16)
    kn = proj(base + HB * HEAD, base + 2 * HB * HEAD)
    k_heads = []
    for hd in range(HB):
        kh = _rms(kn[:, hd * HEAD:(hd + 1) * HEAD], gnak_ref[...], HEAD)
        kna_ref[:, hd * HEAD:(hd + 1) * HEAD] = kh.astype(BF16)
        k_heads.append(kh)
    vn = proj(base + 2 * HB * HEAD, base + 3 * HB * HEAD)
    vna_ref[...] = vn.astype(BF16)

    @pl.when(is_ctx)
    def _():
        ckv_ref[...] = ckvn
        kr_ref[...] = kr[:, :ROPE]
        vnaf_ref[...] = vn
        for hd, kh in enumerate(k_heads):
            knaf_ref[:, hd * HEAD:(hd + 1) * HEAD] = kh


def _even_in(h, w, tm=256):
    n_ctx = T_CTX // tm
    resident = lambda a: pl.BlockSpec(a.shape, lambda i: (0, 0), pipeline_mode=pl.Buffered(1))
    small = lambda a: pl.BlockSpec(a.shape, lambda i: (0, 0))
    rows = lambda n: pl.BlockSpec((tm, n), lambda i: (i, 0))
    ctx_rows = lambda n: pl.BlockSpec((tm, n), lambda i: (jnp.minimum(i, n_ctx - 1), 0))
    table = pl.BlockSpec((tm, 128), lambda i: (_rope_block_of_tile(i, tm), 0))
    tok = lambda n, dt: jax.ShapeDtypeStruct((T, n), dt)
    ctx = lambda n: jax.ShapeDtypeStruct((T_CTX, n), F32)
    cos, sin = _rope_tables(ROPE)
    norms = [w[k] for k in ("q_norm", "kv_norm", "gq", "gkn", "gkr", "gnaq", "gnak")]
    return pl.pallas_call(
        functools.partial(_even_in_kernel, n_ctx_tiles=n_ctx),
        grid=(T // tm,),
        in_specs=[rows(D), resident(w["w_in"]), resident(w["wq"]), resident(w["wk"]), resident(w["wv"])]
                 + [small(a) for a in norms] + [table, table],
        out_specs=[rows(HA * QK_PAD), rows(HA * QK_PAD), rows(HA * HEAD), rows(HB * HEAD), rows(HB * HEAD),
                   rows(HB * HEAD), ctx_rows(KV_LORA), ctx_rows(ROPE), ctx_rows(HB * HEAD), ctx_rows(HB * HEAD)],
        out_shape=[tok(HA * QK_PAD, BF16), tok(HA * QK_PAD, BF16), tok(HA * HEAD, BF16), tok(HB * HEAD, BF16),
                   tok(HB * HEAD, BF16), tok(HB * HEAD, BF16), ctx(KV_LORA), ctx(ROPE), ctx(HB * HEAD),
                   ctx(HB * HEAD)],
        compiler_params=_cparams(1),
        name="even_in",
    )(h, w["w_in"], w["wq"], w["wk"], w["wv"], *norms, cos, sin)


def _cache_kv_kernel(ckv_ref, kr_ref, wk_ref, wv_ref, gkn_ref, gkr_ref, km_ref, vm_ref):
    _mla_keys_values(ckv_ref[...], kr_ref[...], wk_ref, wv_ref, gkn_ref, gkr_ref, None, None, km_ref, vm_ref)


def _cache_kv(ckv, kr, wk, wv, gkn, gkr, tm=512):
    n = ckv.shape[0]
    full = lambda shape: pl.BlockSpec(shape, lambda i: (0, 0))
    rows = lambda w: pl.BlockSpec((tm, w), lambda i: (i, 0))
    return pl.pallas_call(
        _cache_kv_kernel,
        grid=(n // tm,),
        in_specs=[rows(KV_LORA), rows(128), full(wk.shape), full(wv.shape), full(gkn.shape), full(gkr.shape)],
        out_specs=[rows(HA * QK_PAD), rows(HA * HEAD)],
        out_shape=[jax.ShapeDtypeStruct((n, HA * QK_PAD), BF16), jax.ShapeDtypeStruct((n, HA * HEAD), BF16)],
        compiler_params=_cparams(1),
        name="mla_cache_kv",
    )(ckv, kr, wk, wv, gkn, gkr)


def _odd_in_kernel(h_ref, w_ref, gq_ref, gk_ref, cos_ref, sin_ref,
                   q_ref, k_ref, v_ref, kf_ref, vf_ref, *, n_ctx_tiles, tm):
    is_ctx = pl.program_id(0) < n_ctx_tiles
    k0 = HC * HEAD
    v0 = k0 + KVH_C * HEAD
    cache_rows = []
    for r in range(0, tm, M_CHUNK):
        rows = slice(r, r + M_CHUNK)
        cos = cos_ref[rows, :]
        sin = sin_ref[rows, :]
        h = h_ref[rows, :]
        q = jnp.dot(h, w_ref[:, :k0], preferred_element_type=F32)
        for hd in range(HC):
            y = _rms(q[:, hd * HEAD:(hd + 1) * HEAD], gq_ref[...], HEAD)
            q_ref[rows, hd * HEAD:(hd + 1) * HEAD] = _rope(y, cos, sin, HEAD // 4).astype(BF16)
        k = jnp.dot(h, w_ref[:, k0:v0], preferred_element_type=F32)
        k_heads = []
        for hd in range(KVH_C):
            y = _rms(k[:, hd * HEAD:(hd + 1) * HEAD], gk_ref[...], HEAD)
            k_ref[rows, hd * HEAD:(hd + 1) * HEAD] = _rope(y, cos, sin, HEAD // 4).astype(BF16)
            k_heads.append(y)
        v = jnp.dot(h, w_ref[:, v0:], preferred_element_type=F32)
        v_ref[rows, :] = v.astype(BF16)
        cache_rows.append((rows, k_heads, v))

    @pl.when(is_ctx)
    def _():
        for rows, k_heads, v in cache_rows:
            vf_ref[rows, :] = v
            for hd, y in enumerate(k_heads):
                kf_ref[rows, hd * HEAD:(hd + 1) * HEAD] = y


def _odd_in(h, w, tm=512):
    n_ctx = T_CTX // tm
    small = lambda a: pl.BlockSpec(a.shape, lambda i: (0, 0))
    rows = lambda n: pl.BlockSpec((tm, n), lambda i: (i, 0))
    ctx_rows = lambda n: pl.BlockSpec((tm, n), lambda i: (jnp.minimum(i, n_ctx - 1), 0))
    table = pl.BlockSpec((tm, 128), lambda i: (_rope_block_of_tile(i, tm), 0))
    tok = lambda n, dt: jax.ShapeDtypeStruct((T, n), dt)
    ctx = lambda n: jax.ShapeDtypeStruct((T_CTX, n), F32)
    cos, sin = _rope_tables(HEAD)
    return pl.pallas_call(
        functools.partial(_odd_in_kernel, n_ctx_tiles=n_ctx, tm=tm),
        grid=(T // tm,),
        in_specs=[rows(D), pl.BlockSpec(w["w_in"].shape, lambda i: (0, 0), pipeline_mode=pl.Buffered(1)),
                  small(w["gq"]), small(w["gk"]), table, table],
        out_specs=[rows(HC * HEAD), rows(KVH_C * HEAD), rows(KVH_C * HEAD), ctx_rows(KVH_C * HEAD),
                   ctx_rows(KVH_C * HEAD)],
        out_shape=[tok(HC * HEAD, BF16), tok(KVH_C * HEAD, BF16), tok(KVH_C * HEAD, BF16), ctx(KVH_C * HEAD),
                   ctx(KVH_C * HEAD)],
        compiler_params=_cparams(1),
        name="odd_in",
    )(h, w["w_in"], w["gq"], w["gk"], cos, sin)


def _build_na_bias(toep_ref, bias_scr, hps):
    neg = jnp.full((GRID_W, GRID_W), NEG_INF, F32)
    for hh in range(hps):
        for r in range(GRID_H):
            rs = min(max(r - NA_KH // 2, 0), GRID_H - NA_KH)
            for kr in range(GRID_H):
                tile = toep_ref[hh, kr - r + NA_KH - 1] if rs <= kr < rs + NA_KH else neg
                bias_scr[hh, r * GRID_W:(r + 1) * GRID_W, kr * GRID_W:(kr + 1) * GRID_W] = tile


def _attn_kernel(*refs, plan, n_kv, bias_mode, has_sink, hps, groups, dk, dv, qb, c):
    it = iter(refs)
    q_ref = next(it)
    kv_refs = [(next(it), next(it)) for _ in range(n_kv)]
    bias_ref = next(it) if bias_mode is not None else None
    sink_ref = next(it) if has_sink else None
    o_ref = next(it)
    bias_scr = next(it) if bias_mode == "na" else None
    hblk = pl.program_id(0)

    if bias_mode == "na":
        @pl.when(pl.program_id(1) == 0)
        def _():
            _build_na_bias(bias_ref, bias_scr, hps)

    def one_head(hh, head, q_cols, o_cols):
        sink = sink_ref[hblk * hps * groups + head] if has_sink else None
        for qi, segs in enumerate(plan):
            q0 = qi * qb
            q = q_ref[q0:q0 + qb, q_cols]
            scores = []
            for (si, start, length, biased) in segs:
                k = kv_refs[si][0][start:start + length, hh * dk:(hh + 1) * dk].astype(BF16)
                t = lax.dot_general(q, k, _NT, preferred_element_type=F32) * c
                if biased and bias_mode == "na":
                    t = t + bias_scr[hh, q0:q0 + qb, start:start + length]
                elif biased:
                    t = t + bias_ref[qi, :, :length]
                scores.append(t)
            m = jnp.max(scores[0], axis=-1, keepdims=True)
            for t in scores[1:]:
                m = jnp.maximum(m, jnp.max(t, axis=-1, keepdims=True))
            if has_sink:
                m = jnp.maximum(m, sink)
            acc = None
            for t, (si, start, length, _) in zip(scores, segs):
                p = jnp.exp2(t - m).astype(BF16)
                v = kv_refs[si][1][start:start + length, hh * dv:(hh + 1) * dv].astype(BF16)
                v1 = jnp.concatenate([v, jnp.ones((length, dv), BF16)], axis=-1)
                pv = jnp.dot(p, v1, preferred_element_type=F32)
                acc = pv if acc is None else acc + pv
            denom = acc[:, dv:]
            if has_sink:
                denom = denom + jnp.exp2(sink - m)
            o_ref[q0:q0 + qb, o_cols] = (acc[:, :dv] / denom).astype(o_ref.dtype)

    for hh in range(hps):
        if groups > 1 and len(plan) > 1:
            def group_body(g, carry, hh=hh):
                head = hh * groups + g
                one_head(hh, head, pl.ds(pl.multiple_of(head * dk, dk), dk), pl.ds(pl.multiple_of(head * dv, dv), dv))
                return carry

            lax.fori_loop(0, groups, group_body, 0, unroll=2)
        else:
            for g in range(groups):
                head = hh * groups + g
                one_head(hh, head, slice(head * dk, (head + 1) * dk), slice(head * dv, (head + 1) * dv))


def _attention(q, kvs, plan, *, nb, sq, q_rowblk0, n_kv_heads, hps, groups, dk, dv, scale,
               bias=None, bias_mode=None, sink=None, name="attention"):
    grid = (n_kv_heads // hps, nb)
    in_specs = [pl.BlockSpec((sq, hps * groups * dk), lambda h, b: (q_rowblk0 + b, h))]
    args = [q]
    for (k, v, sk, r0) in kvs:
        in_specs.append(pl.BlockSpec((sk, hps * dk), lambda h, b, r0=r0: (r0 + b, h)))
        in_specs.append(pl.BlockSpec((sk, hps * dv), lambda h, b, r0=r0: (r0 + b, h)))
        args += [k, v]
    scratch = []
    if bias_mode == "na":
        in_specs.append(pl.BlockSpec((hps,) + bias.shape[1:], lambda h, b: (h, 0, 0, 0)))
        args.append(bias)
        scratch.append(pltpu.VMEM((hps, LAT_S, LAT_S), F32))
    elif bias_mode == "table":
        in_specs.append(pl.BlockSpec(bias.shape, lambda h, b: (0, 0, 0)))
        args.append(bias)
    if sink is not None:
        in_specs.append(pl.BlockSpec(memory_space=pltpu.SMEM))
        args.append(sink)
    body = functools.partial(
        _attn_kernel, plan=plan, n_kv=len(kvs), bias_mode=bias_mode, has_sink=sink is not None,
        hps=hps, groups=groups, dk=dk, dv=dv, qb=min(Q_BLOCK, sq), c=scale * LOG2E)
    return pl.pallas_call(
        body,
        grid=grid,
        in_specs=in_specs,
        out_specs=pl.BlockSpec((sq, hps * groups * dv), lambda h, b: (b, h)),
        out_shape=jax.ShapeDtypeStruct((nb * sq, n_kv_heads * groups * dv), BF16),
        scratch_shapes=scratch,
        compiler_params=_cparams(2),
        name=name,
    )(*args)


_CTX_PLAN = [[(0, 0, CTX_S, False)]]
_N_QB = LAT_S // Q_BLOCK
_MLA_PLAN = [[(0, 0, LAT_S, False), (1, 0, PAST, False)] for _ in range(_N_QB)]
_NA_ROWS = [(0, 8), (0, 12), (4, 16), (8, 16)]
_NA_PLAN = [[(0, lo * GRID_W, (hi - lo) * GRID_W, True), (1, 0, PAST, False)] for lo, hi in _NA_ROWS]
_WIN_SPANS = [(max(i * Q_BLOCK - WINDOW, 0), min((i + 1) * Q_BLOCK + WINDOW, LAT_S)) for i in range(_N_QB)]
_WIN_PLAN = [[(0, lo, hi - lo, True), (1, 0, PAST, False)] for lo, hi in _WIN_SPANS]


def _window_bias():
    width = max(hi - lo for lo, hi in _WIN_SPANS)
    out = np.full((_N_QB, Q_BLOCK, width), NEG_INF, np.float32)
    for i, (lo, hi) in enumerate(_WIN_SPANS):
        q_abs = i * Q_BLOCK + np.arange(Q_BLOCK)[:, None]
        k_abs = lo + np.arange(hi - lo)[None, :]
        out[i, :, :hi - lo] = np.where(np.abs(q_abs - k_abs) <= WINDOW, 0.0, NEG_INF)
    return jnp.asarray(out)


def _rope_tables(rot_dim):
    half = rot_dim // 2
    nf = half // 2
    t = np.arange(LAT_S)
    inv_freq = ROPE_THETA ** (-np.arange(nf, dtype=np.float64) / nf)
    cos = np.zeros((2 * LAT_S, 128), np.float64)
    sin = np.zeros((2 * LAT_S, 128), np.float64)
    cos[:LAT_S, :rot_dim] = 1.0
    for part, pos in enumerate((t // GRID_W, t % GRID_W)):
        ang = pos[:, None].astype(np.float64) * inv_freq[None, :]
        lo = part * half
        cos[LAT_S:, lo:lo + nf] = np.cos(ang)
        cos[LAT_S:, lo + nf:lo + half] = np.cos(ang)
        sin[LAT_S:, lo:lo + nf] = -np.sin(ang)
        sin[LAT_S:, lo + nf:lo + half] = np.sin(ang)
    return jnp.asarray(cos, F32), jnp.asarray(sin, F32)


def _na_toeplitz(rpb):
    c = np.arange(GRID_W)
    ws = np.clip(c - NA_KW // 2, 0, GRID_W - NA_KW)
    col_valid = (c[None, :] >= ws[:, None]) & (c[None, :] < ws[:, None] + NA_KW)
    n_off, n_rel = rpb.shape[1], rpb.shape[2]
    width = 2 * GRID_W
    lead = GRID_W - NA_KW
    u = jnp.pad(rpb * LOG2E, ((0, 0), (0, 0), (lead, width - lead - n_rel)))
    a = jnp.broadcast_to(u[:, :, None, :], (HB, n_off, GRID_W, width)).reshape(HB, n_off, GRID_W * width)
    a = jnp.pad(a, ((0, 0), (0, 0), (0, GRID_W))).reshape(HB, n_off, GRID_W, width + 1)
    t = a[:, :, ::-1, :GRID_W]
    return jnp.where(jnp.asarray(col_valid)[None, None], t, NEG_INF)


def _even_mixer(x, h, mods, g_next, w, caches):
    cache_ckv, cache_krope, cache_nak, cache_nav = caches
    qm, km, vm, qna, kna, vna, new_ckv, new_kr, new_nak, new_nav = _even_in(h, w)
    km_ctx, vm_ctx = _cache_kv(cache_ckv, cache_krope, w["wk"], w["wv"], w["gkn"], w["gkr"])
    s_mla = 1.0 / math.sqrt(QK)
    s_na = 1.0 / math.sqrt(HEAD)
    lat0 = T_CTX // LAT_S
    o_mla_ctx = _attention(qm, [(km, vm, CTX_S, 0)], _CTX_PLAN, nb=CTX_B, sq=CTX_S, q_rowblk0=0, n_kv_heads=HA,
                           hps=HA, groups=1, dk=QK_PAD, dv=HEAD, scale=s_mla, name="mla_ctx")
    o_na_ctx = _attention(qna, [(kna, vna, CTX_S, 0)], _CTX_PLAN, nb=CTX_B, sq=CTX_S, q_rowblk0=0, n_kv_heads=HB,
                          hps=HB, groups=1, dk=HEAD, dv=HEAD, scale=s_na, name="na_ctx")
    o_mla_lat = _attention(qm, [(km, vm, LAT_S, lat0), (km_ctx, vm_ctx, PAST, 0)], _MLA_PLAN, nb=LAT_B, sq=LAT_S,
                           q_rowblk0=lat0, n_kv_heads=HA, hps=4, groups=1, dk=QK_PAD, dv=HEAD, scale=s_mla,
                           name="mla_lat")
    o_na_lat = _attention(qna, [(kna, vna, LAT_S, lat0), (cache_nak, cache_nav, PAST, 0)], _NA_PLAN, nb=LAT_B,
                          sq=LAT_S, q_rowblk0=lat0, n_kv_heads=HB, hps=2, groups=1, dk=HEAD, dv=HEAD, scale=s_na,
                          bias=w["na_toep"], bias_mode="na", name="na_lat")
    x, h = _out_proj([o_mla_ctx, o_na_ctx], [o_mla_lat, o_na_lat], w["w_out"], w["w_out_lead"], x, mods, g_next)
    return x, h, (new_ckv.reshape(CTX_B, CTX_S, KV_LORA), new_kr.reshape(CTX_B, CTX_S, ROPE),
                  new_nak.reshape(CTX_B, CTX_S, HB, HEAD), new_nav.reshape(CTX_B, CTX_S, HB, HEAD))


def _odd_mixer(x, h, mods, g_next, w, caches):
    cache_k, cache_v = caches
    q, k, v, new_k, new_v = _odd_in(h, w)
    scale = 1.0 / math.sqrt(HEAD)
    lat0 = T_CTX // LAT_S
    o_ctx = _attention(q, [(k, v, CTX_S, 0)], _CTX_PLAN, nb=CTX_B, sq=CTX_S, q_rowblk0=0, n_kv_heads=KVH_C,
                       hps=KVH_C, groups=GROUPS_C, dk=HEAD, dv=HEAD, scale=scale, sink=w["sink"], name="gqa_ctx")
    o_lat = _attention(q, [(k, v, LAT_S, lat0), (cache_k, cache_v, PAST, 0)], _WIN_PLAN, nb=LAT_B, sq=LAT_S,
                       q_rowblk0=lat0, n_kv_heads=KVH_C, hps=1, groups=GROUPS_C, dk=HEAD, dv=HEAD, scale=scale,
                       bias=_window_bias(), bias_mode="table", sink=w["sink"], name="gqa_lat")
    x, h = _out_proj([o_ctx], [o_lat], w["w_out"], w["w_out_lead"], x, mods, g_next)
    return x, h, (new_k.reshape(CTX_B, CTX_S, KVH_C, HEAD), new_v.reshape(CTX_B, CTX_S, KVH_C, HEAD))


def _even_weights(e, even_w_in, w_out_bf16, mla_q_norm, mla_w_q_up, mla_kv_norm, mla_w_kv_up, mla_qk_norm,
                  na_qk_norm, na_rpb):
    w_in = even_w_in[e]
    i0, i1, i2 = Q_LORA, Q_LORA + KV_LORA, Q_LORA + KV_LORA + ROPE
    w_in = jnp.concatenate([w_in[:, :i1], w_in[:, i2:], w_in[:, i1:i2], jnp.zeros((D, 128 - ROPE), F32)], axis=1)
    wq = jnp.pad(mla_w_q_up[e].reshape(Q_LORA, HA, QK), ((0, 0), (0, 0), (0, QK_PAD - QK)))
    wkv = mla_w_kv_up[e].reshape(KV_LORA, HA, NOPE + HEAD)
    qk = mla_qk_norm[e]
    return {
        "w_in": w_in.astype(BF16),
        "w_out": w_out_bf16,
        "w_out_lead": (e,),
        "wq": wq.reshape(Q_LORA, HA * QK_PAD).astype(BF16),
        "wk": wkv[:, :, :NOPE].reshape(KV_LORA, HA * NOPE).astype(BF16),
        "wv": wkv[:, :, NOPE:].reshape(KV_LORA, HA * HEAD).astype(BF16),
        "q_norm": mla_q_norm[e][None, :],
        "kv_norm": mla_kv_norm[e][None, :],
        "gq": jnp.pad(qk[0], (0, QK_PAD - QK))[None, :],
        "gkn": qk[1, :NOPE][None, :],
        "gkr": jnp.pad(qk[1, NOPE:], (0, 128 - ROPE))[None, :],
        "gnaq": na_qk_norm[e, 0][None, :],
        "gnak": na_qk_norm[e, 1][None, :],
        "na_toep": _na_toeplitz(na_rpb[e]),
    }


def kernel(x_prompt, x_sample, cache_mla_ckv, cache_mla_krope, cache_na_k, cache_na_v, cache_gqa_k, cache_gqa_v, c, c_ctx, ada_w, ada_b, norm_g, ffn_w_in, ffn_w_out, even_w_in, even_w_out, mla_q_norm, mla_w_q_up, mla_kv_norm, mla_w_kv_up, mla_qk_norm, na_qk_norm, na_rpb, odd_w_in, odd_w_out, gqa_qk_norm, gqa_sink):
    depth = ada_w.shape[0]
    cond = jnp.concatenate([c_ctx[None, :], c, jnp.zeros((16 - N_GROUPS, D), F32)], axis=0)
    mods_all = _ada_modulation(cond, ada_w, ada_b)
    layer_mods = [mods_all[layer, :N_GROUPS].reshape(N_GROUPS, N_MOD, D) for layer in range(depth)]
    even_w_out16 = even_w_out.astype(BF16)
    odd_w_out16 = odd_w_out.astype(BF16)
    n_ctx_tiles = T_CTX // FFN_OUT_TM

    ckv_l, kr_l, nak_l, nav_l, gk_l, gv_l = [], [], [], [], [], []
    y_prompt = y_sample = None
    x, h = _embed(x_prompt, x_sample, layer_mods[0], norm_g[0, 0:1])
    for layer in range(depth):
        mods = layer_mods[layer]
        g = norm_g[layer]
        e = layer // 2
        act, w_out16 = _ffn_in(h, ffn_w_in, ffn_w_out, (layer, 0))
        x, h = _ffn_out(act, w_out16, x, mods, 2, nxt=(mods, g[1:2], 1))
        if layer % 2 == 0:
            w = _even_weights(e, even_w_in, even_w_out16, mla_q_norm, mla_w_q_up, mla_kv_norm, mla_w_kv_up,
                              mla_qk_norm, na_qk_norm, na_rpb)
            caches = (cache_mla_ckv[:, e].reshape(LAT_B * PAST, KV_LORA),
                      jnp.pad(cache_mla_krope[:, e].reshape(LAT_B * PAST, ROPE), ((0, 0), (0, 128 - ROPE))),
                      cache_na_k[:, e].reshape(LAT_B * PAST, HB * HEAD),
                      cache_na_v[:, e].reshape(LAT_B * PAST, HB * HEAD))
            x, h, (ckv, kr, nak, nav) = _even_mixer(x, h, mods, g[2:3], w, caches)
            ckv_l.append(ckv)
            kr_l.append(kr)
            nak_l.append(nak)
            nav_l.append(nav)
        else:
            w = {"w_in": odd_w_in[e].astype(BF16), "w_out": odd_w_out16, "w_out_lead": (e,),
                 "gq": gqa_qk_norm[e, 0][None, :], "gk": gqa_qk_norm[e, 1][None, :],
                 "sink": gqa_sink[e] * LOG2E}
            caches = (cache_gqa_k[:, e].reshape(LAT_B * PAST, KVH_C * HEAD),
                      cache_gqa_v[:, e].reshape(LAT_B * PAST, KVH_C * HEAD))
            x, h, (gk, gv) = _odd_mixer(x, h, mods, g[2:3], w, caches)
            gk_l.append(gk)
            gv_l.append(gv)
        act, w_out16 = _ffn_in(h, ffn_w_in, ffn_w_out, (layer, 1))
        if layer + 1 < depth:
            x, h = _ffn_out(act, w_out16, x, mods, 8, nxt=(layer_mods[layer + 1], norm_g[layer + 1, 0:1], 0))
        else:
            y_prompt = _ffn_out(act, w_out16, x, mods, 8, tile0=0, n_tiles=n_ctx_tiles).reshape(CTX_B, CTX_S, D)
            y_sample = _ffn_out(act, w_out16, x, mods, 8, tile0=n_ctx_tiles,
                                n_tiles=T // FFN_OUT_TM - n_ctx_tiles).reshape(LAT_B, LAT_S, D)

    return (y_prompt, y_sample, jnp.stack(ckv_l, axis=1), jnp.stack(kr_l, axis=1), jnp.stack(nak_l, axis=1),
            jnp.stack(nav_l, axis=1), jnp.stack(gk_l, axis=1), jnp.stack(gv_l, axis=1))
```

```python
import functools
import math

import numpy as np
import jax
import jax.numpy as jnp
from jax import lax
from jax.experimental import pallas as pl
from jax.experimental.pallas import tpu as pltpu

F32 = jnp.float32
BF16 = jnp.bfloat16

D = 2048
D_FF = 5632
N_MOD = 9
CTX_B, CTX_S = 16, 256
LAT_B, LAT_S = 8, 1024
PAST = 512
T_CTX = CTX_B * CTX_S
T_LAT = LAT_B * LAT_S
T = T_CTX + T_LAT
GRID_W = 64
GRID_H = LAT_S // GRID_W
N_GROUPS = 1 + LAT_B
HEAD = 128
HA = 8
NOPE, ROPE = 128, 64
QK = NOPE + ROPE
QK_PAD = 256
Q_LORA = 512
KV_LORA = 512
HB = 8
NA_KH, NA_KW = 8, 16
HC, KVH_C = 16, 4
GROUPS_C = HC // KVH_C
WINDOW = 128
IN_EVEN_PAD = 4224
IN_ODD = 3072
ROPE_THETA = 10000.0
EPS = 1e-6
NEG_INF = -1e30
LOG2E = math.log2(math.e)

VMEM_LIMIT = 56 * 1024 * 1024
Q_BLOCK = 256
M_CHUNK = 256
WO_COL_TILES = 4
FFN_OUT_TM = 512

_NT = (((1,), (1,)), ((), ()))


def _cparams(n_axes):
    return pltpu.CompilerParams(dimension_semantics=("arbitrary",) * n_axes,
                                vmem_limit_bytes=VMEM_LIMIT)


def _group_of_tile(i, tm):
    n_ctx = T_CTX // tm
    per = LAT_S // tm
    return jnp.where(i < n_ctx, 0, (i - n_ctx) // per + 1)


def _rope_block_of_tile(i, tm):
    n_ctx = T_CTX // tm
    per = LAT_S // tm
    return jnp.where(i < n_ctx, 0, per + (i - n_ctx) % per)


def _ada_kernel(c_ref, w_ref, b_ref, o_ref):
    c = c_ref[...]
    a = (c * jax.nn.sigmoid(c)).astype(BF16)
    o_ref[0] = jnp.dot(a, w_ref[0].astype(BF16), preferred_element_type=F32) + b_ref[0]


def _ada_modulation(cond, ada_w, ada_b, tn=2048):
    depth = ada_w.shape[0]
    n = N_MOD * D
    rows = cond.shape[0]
    return pl.pallas_call(
        _ada_kernel,
        grid=(depth, n // tn),
        in_specs=[
            pl.BlockSpec((rows, D), lambda l, j: (0, 0)),
            pl.BlockSpec((1, D, tn), lambda l, j: (l, 0, j)),
            pl.BlockSpec((1, 1, tn), lambda l, j: (l, 0, j)),
        ],
        out_specs=pl.BlockSpec((1, rows, tn), lambda l, j: (l, 0, j)),
        out_shape=jax.ShapeDtypeStruct((depth, rows, n), F32),
        compiler_params=_cparams(2),
        name="ada_modulation",
    )(cond, ada_w, ada_b.reshape(depth, 1, n))


def _mod_rows(mod_ref, g_ref, which):
    shift = mod_ref[0, 3 * which:3 * which + 1, :]
    scale = mod_ref[0, 3 * which + 1:3 * which + 2, :]
    return g_ref[...] * (1.0 + scale), shift


def _modulate(x, gain, shift):
    inv = lax.rsqrt(jnp.mean(x * x, axis=-1, keepdims=True) + EPS)
    return ((x * inv) * gain + shift).astype(BF16)


def _embed_kernel(xp_ref, xs_ref, mod_ref, g_ref, x_ref, h_ref, *, n_ctx_tiles, tm):
    gain, shift = _mod_rows(mod_ref, g_ref, 0)

    def run(src_ref):
        for r in range(0, tm, M_CHUNK):
            x = src_ref[r:r + M_CHUNK, :]
            x_ref[r:r + M_CHUNK, :] = x
            h_ref[r:r + M_CHUNK, :] = _modulate(x, gain, shift)

    @pl.when(pl.program_id(0) < n_ctx_tiles)
    def _():
        run(xp_ref)

    @pl.when(pl.program_id(0) >= n_ctx_tiles)
    def _():
        run(xs_ref)


def _embed(x_prompt, x_sample, mods, g, tm=512):
    n_ctx = T_CTX // tm
    return pl.pallas_call(
        functools.partial(_embed_kernel, n_ctx_tiles=n_ctx, tm=tm),
        grid=(T // tm,),
        in_specs=[
            pl.BlockSpec((tm, D), lambda i: (jnp.minimum(i, n_ctx - 1), 0)),
            pl.BlockSpec((tm, D), lambda i: (jnp.maximum(i - n_ctx, 0), 0)),
            pl.BlockSpec((1, N_MOD, D), lambda i: (_group_of_tile(i, tm), 0, 0)),
            pl.BlockSpec((1, D), lambda i: (0, 0)),
        ],
        out_specs=[pl.BlockSpec((tm, D), lambda i: (i, 0)), pl.BlockSpec((tm, D), lambda i: (i, 0))],
        out_shape=[jax.ShapeDtypeStruct((T, D), F32), jax.ShapeDtypeStruct((T, D), BF16)],
        compiler_params=_cparams(1),
        name="embed",
    )(x_prompt.reshape(T_CTX, D), x_sample.reshape(T_LAT, D), mods, g)


def _ffn_in_kernel(h_ref, wg_ref, wu_ref, wo_ref, o_ref, wo16_ref, *, tm):
    @pl.when(pl.program_id(0) < WO_COL_TILES)
    def _():
        wo16_ref[...] = wo_ref[...].astype(BF16)

    wg = wg_ref[...].astype(BF16)
    wu = wu_ref[...].astype(BF16)
    for r in range(0, tm, M_CHUNK):
        h = h_ref[r:r + M_CHUNK, :]
        gate = jnp.dot(h, wg, preferred_element_type=F32)
        up = jnp.dot(h, wu, preferred_element_type=F32)
        o_ref[r:r + M_CHUNK, :] = (gate * jax.nn.sigmoid(gate) * up).astype(BF16)


def _ffn_in(h, w_in, w_out, lead, tm=2048, tn=512):
    nj = D_FF // tn
    n_lead = len(lead)
    wo_cols = D // WO_COL_TILES
    assert nj * tn == D_FF and T // tm >= WO_COL_TILES

    def wo_tile(i, j):
        parked = i >= WO_COL_TILES
        return jnp.where(parked, nj - 1, j), jnp.minimum(i, WO_COL_TILES - 1)

    return pl.pallas_call(
        functools.partial(_ffn_in_kernel, tm=tm),
        grid=(T // tm, nj),
        in_specs=[
            pl.BlockSpec((tm, D), lambda i, j: (i, 0)),
            pl.BlockSpec((None,) * n_lead + (D, tn), lambda i, j: lead + (0, j)),
            pl.BlockSpec((None,) * n_lead + (D, tn), lambda i, j: lead + (0, j + nj)),
            pl.BlockSpec((None,) * n_lead + (tn, wo_cols), lambda i, j: lead + wo_tile(i, j)),
        ],
        out_specs=[pl.BlockSpec((tm, tn), lambda i, j: (i, j)),
                   pl.BlockSpec((tn, wo_cols), wo_tile)],
        out_shape=[jax.ShapeDtypeStruct((T, D_FF), BF16), jax.ShapeDtypeStruct((D_FF, D), BF16)],
        compiler_params=_cparams(2),
        name="ffn_in",
    )(h, w_in, w_in, w_out)


def _ffn_out_kernel(*refs, gate_row, next_which, tm):
    if next_which is None:
        a_ref, w_ref, x_ref, mod_ref, o_ref = refs
    else:
        a_ref, w_ref, x_ref, mod_ref, modn_ref, gn_ref, o_ref, h_ref = refs
        gain, shift = _mod_rows(modn_ref, gn_ref, next_which)
    gate = 0.5 * mod_ref[0, gate_row:gate_row + 1, :]
    for r in range(0, tm, M_CHUNK):
        acc = jnp.dot(a_ref[r:r + M_CHUNK, :], w_ref[...], preferred_element_type=F32)
        x = x_ref[r:r + M_CHUNK, :] + gate * acc
        o_ref[r:r + M_CHUNK, :] = x
        if next_which is not None:
            h_ref[r:r + M_CHUNK, :] = _modulate(x, gain, shift)


def _ffn_out(a, w, x, mods, gate_row, nxt=None, tile0=0, n_tiles=T // FFN_OUT_TM, tm=FFN_OUT_TM):
    k = a.shape[1]
    rows = lambda n: pl.BlockSpec((tm, n), lambda i: (tile0 + i, 0))
    mod_spec = pl.BlockSpec((1, N_MOD, D), lambda i: (_group_of_tile(tile0 + i, tm), 0, 0))
    in_specs = [rows(k), pl.BlockSpec((k, D), lambda i: (0, 0), pipeline_mode=pl.Buffered(1)), rows(D), mod_spec]
    args = [a, w, x, mods]
    out_specs = [pl.BlockSpec((tm, D), lambda i: (i, 0))]
    out_shape = [jax.ShapeDtypeStruct((n_tiles * tm, D), F32)]
    if nxt is not None:
        in_specs += [mod_spec, pl.BlockSpec((1, D), lambda i: (0, 0))]
        args += [nxt[0], nxt[1]]
        out_specs.append(pl.BlockSpec((tm, D), lambda i: (i, 0)))
        out_shape.append(jax.ShapeDtypeStruct((n_tiles * tm, D), BF16))
    out = pl.pallas_call(
        functools.partial(_ffn_out_kernel, gate_row=gate_row, next_which=None if nxt is None else nxt[2], tm=tm),
        grid=(n_tiles,),
        in_specs=in_specs,
        out_specs=out_specs,
        out_shape=out_shape,
        compiler_params=_cparams(1),
        name="ffn_out",
    )(*args)
    return out if nxt is not None else out[0]


def _out_proj_kernel(*refs, widths, n_ctx_tiles, tm):
    n_a = len(widths)
    ctx_refs, lat_refs = refs[:n_a], refs[n_a:2 * n_a]
    w_ref, x_ref, mod_ref, g_ref, o_ref, h_ref = refs[2 * n_a:]
    gate = mod_ref[0, 5:6, :]
    gain, shift = _mod_rows(mod_ref, g_ref, 2)

    def run(a_refs):
        for r in range(0, tm, M_CHUNK):
            acc = None
            k0 = 0
            for a_ref, kw in zip(a_refs, widths):
                part = jnp.dot(a_ref[r:r + M_CHUNK, :], w_ref[k0:k0 + kw, :], preferred_element_type=F32)
                acc = part if acc is None else acc + part
                k0 += kw
            x = x_ref[r:r + M_CHUNK, :] + gate * acc
            o_ref[r:r + M_CHUNK, :] = x
            h_ref[r:r + M_CHUNK, :] = _modulate(x, gain, shift)

    @pl.when(pl.program_id(0) < n_ctx_tiles)
    def _():
        run(ctx_refs)

    @pl.when(pl.program_id(0) >= n_ctx_tiles)
    def _():
        run(lat_refs)


def _out_proj(a_ctx, a_lat, w, lead, x, mods, g_next, tm=512):
    widths = tuple(a.shape[1] for a in a_ctx)
    n_ctx = T_CTX // tm
    specs = [pl.BlockSpec((tm, kw), lambda i: (jnp.minimum(i, n_ctx - 1), 0)) for kw in widths]
    specs += [pl.BlockSpec((tm, kw), lambda i: (jnp.maximum(i - n_ctx, 0), 0)) for kw in widths]
    return pl.pallas_call(
        functools.partial(_out_proj_kernel, widths=widths, n_ctx_tiles=n_ctx, tm=tm),
        grid=(T // tm,),
        in_specs=specs + [
            pl.BlockSpec((None,) * len(lead) + (D, D), lambda i: lead + (0, 0), pipeline_mode=pl.Buffered(1)),
            pl.BlockSpec((tm, D), lambda i: (i, 0)),
            pl.BlockSpec((1, N_MOD, D), lambda i: (_group_of_tile(i, tm), 0, 0)),
            pl.BlockSpec((1, D), lambda i: (0, 0)),
        ],
        out_specs=[pl.BlockSpec((tm, D), lambda i: (i, 0)), pl.BlockSpec((tm, D), lambda i: (i, 0))],
        out_shape=[jax.ShapeDtypeStruct((T, D), F32), jax.ShapeDtypeStruct((T, D), BF16)],
        compiler_params=_cparams(1),
        name="mixer_out_proj",
    )(*a_ctx, *a_lat, w, x, mods, g_next)


def _rope(x, cos, sin, nf):
    lane = lax.broadcasted_iota(jnp.int32, x.shape, 1)
    first = (lane & (2 * nf - 1)) < nf
    partner = jnp.where(first, pltpu.roll(x, 128 - nf, 1), pltpu.roll(x, nf, 1))
    return x * cos + partner * sin


def _rms(x, g, n):
    inv = lax.rsqrt(jnp.sum(x * x, axis=-1, keepdims=True) * (1.0 / n) + EPS)
    return x * inv * g


def _mla_keys_values(ckvn, kr, wk_ref, wv_ref, gkn_ref, gkr_ref, cos, sin, km_ref, vm_ref):
    c16 = ckvn.astype(BF16)
    kn = jnp.dot(c16, wk_ref[...], preferred_element_type=F32)
    vm_ref[...] = jnp.dot(c16, wv_ref[...], preferred_element_type=F32).astype(BF16)
    kr_ss = jnp.sum(kr * kr, axis=-1, keepdims=True)
    for h in range(HA):
        x = kn[:, h * NOPE:(h + 1) * NOPE]
        inv = lax.rsqrt((jnp.sum(x * x, axis=-1, keepdims=True) + kr_ss) * (1.0 / QK) + EPS)
        km_ref[:, h * QK_PAD:h * QK_PAD + NOPE] = (x * inv * gkn_ref[...]).astype(BF16)
        r = kr * inv * gkr_ref[...]
        if cos is not None:
            r = _rope(r, cos, sin, ROPE // 4)
        km_ref[:, h * QK_PAD + NOPE:(h + 1) * QK_PAD] = r.astype(BF16)


def _even_in_kernel(h_ref, w_ref, wq_ref, wk_ref, wv_ref, qn_ref, kvn_ref, gq_ref, gkn_ref,
                    gkr_ref, gnaq_ref, gnak_ref, cos_ref, sin_ref,
                    qm_ref, km_ref, vm_ref, qna_ref, kna_ref, vna_ref,
                    ckv_ref, kr_ref, knaf_ref, vnaf_ref, *, n_ctx_tiles):
    is_ctx = pl.program_id(0) < n_ctx_tiles
    cos = cos_ref[...]
    sin = sin_ref[...]
    h = h_ref[...]
    proj = lambda lo, hi: jnp.dot(h, w_ref[:, lo:hi], preferred_element_type=F32)
    base = Q_LORA + KV_LORA

    low = proj(0, base)
    kr = proj(IN_EVEN_PAD - 128, IN_EVEN_PAD)
    cq = _rms(low[:, :Q_LORA], qn_ref[...], Q_LORA)
    q = jnp.dot(cq.astype(BF16), wq_ref[...], preferred_element_type=F32)
    for hd in range(HA):
        y = _rms(q[:, hd * QK_PAD:(hd + 1) * QK_PAD], gq_ref[...], QK)
        qm_ref[:, hd * QK_PAD:hd * QK_PAD + NOPE] = y[:, :NOPE].astype(BF16)
        qm_ref[:, hd * QK_PAD + NOPE:(hd + 1) * QK_PAD] = _rope(y[:, NOPE:], cos, sin, ROPE // 4).astype(BF16)
    ckvn = _rms(low[:, Q_LORA:], kvn_ref[...], KV_LORA)
    _mla_keys_values(ckvn, kr, wk_ref, wv_ref, gkn_ref, gkr_ref, cos, sin, km_ref, vm_ref)

    qn = proj(base, base + HB * HEAD)
    for hd in range(HB):
        qna_ref[:, hd * HEAD:(hd + 1) * HEAD] = _rms(qn[:, hd * HEAD:(hd + 1) * HEAD], gnaq_ref[...],
                                                     HEAD).astype(BF16)
    kn = proj(base + HB * HEAD, base + 2 * HB * HEAD)
    k_heads = []
    for hd in range(HB):
        kh = _rms(kn[:, hd * HEAD:(hd + 1) * HEAD], gnak_ref[...], HEAD)
        kna_ref[:, hd * HEAD:(hd + 1) * HEAD] = kh.astype(BF16)
        k_heads.append(kh)
    vn = proj(base + 2 * HB * HEAD, base + 3 * HB * HEAD)
    vna_ref[...] = vn.astype(BF16)

    @pl.when(is_ctx)
    def _():
        ckv_ref[...] = ckvn
        kr_ref[...] = kr[:, :ROPE]
        vnaf_ref[...] = vn
        for hd, kh in enumerate(k_heads):
            knaf_ref[:, hd * HEAD:(hd + 1) * HEAD] = kh


def _even_in(h, w, tm=256):
    n_ctx = T_CTX // tm
    resident = lambda a: pl.BlockSpec(a.shape, lambda i: (0, 0), pipeline_mode=pl.Buffered(1))
    small = lambda a: pl.BlockSpec(a.shape, lambda i: (0, 0))
    rows = lambda n: pl.BlockSpec((tm, n), lambda i: (i, 0))
    ctx_rows = lambda n: pl.BlockSpec((tm, n), lambda i: (jnp.minimum(i, n_ctx - 1), 0))
    table = pl.BlockSpec((tm, 128), lambda i: (_rope_block_of_tile(i, tm), 0))
    tok = lambda n, dt: jax.ShapeDtypeStruct((T, n), dt)
    ctx = lambda n: jax.ShapeDtypeStruct((T_CTX, n), F32)
    cos, sin = _rope_tables(ROPE)
    norms = [w[k] for k in ("q_norm", "kv_norm", "gq", "gkn", "gkr", "gnaq", "gnak")]
    return pl.pallas_call(
        functools.partial(_even_in_kernel, n_ctx_tiles=n_ctx),
        grid=(T // tm,),
        in_specs=[rows(D), resident(w["w_in"]), resident(w["wq"]), resident(w["wk"]), resident(w["wv"])]
                 + [small(a) for a in norms] + [table, table],
        out_specs=[rows(HA * QK_PAD), rows(HA * QK_PAD), rows(HA * HEAD), rows(HB * HEAD), rows(HB * HEAD),
                   rows(HB * HEAD), ctx_rows(KV_LORA), ctx_rows(ROPE), ctx_rows(HB * HEAD), ctx_rows(HB * HEAD)],
        out_shape=[tok(HA * QK_PAD, BF16), tok(HA * QK_PAD, BF16), tok(HA * HEAD, BF16), tok(HB * HEAD, BF16),
                   tok(HB * HEAD, BF16), tok(HB * HEAD, BF16), ctx(KV_LORA), ctx(ROPE), ctx(HB * HEAD),
                   ctx(HB * HEAD)],
        compiler_params=_cparams(1),
        name="even_in",
    )(h, w["w_in"], w["wq"], w["wk"], w["wv"], *norms, cos, sin)


def _cache_kv_kernel(ckv_ref, kr_ref, wk_ref, wv_ref, gkn_ref, gkr_ref, km_ref, vm_ref):
    _mla_keys_values(ckv_ref[...], kr_ref[...], wk_ref, wv_ref, gkn_ref, gkr_ref, None, None, km_ref, vm_ref)


def _cache_kv(ckv, kr, wk, wv, gkn, gkr, tm=512):
    n = ckv.shape[0]
    full = lambda shape: pl.BlockSpec(shape, lambda i: (0, 0))
    rows = lambda w: pl.BlockSpec((tm, w), lambda i: (i, 0))
    return pl.pallas_call(
        _cache_kv_kernel,
        grid=(n // tm,),
        in_specs=[rows(KV_LORA), rows(128), full(wk.shape), full(wv.shape), full(gkn.shape), full(gkr.shape)],
        out_specs=[rows(HA * QK_PAD), rows(HA * HEAD)],
        out_shape=[jax.ShapeDtypeStruct((n, HA * QK_PAD), BF16), jax.ShapeDtypeStruct((n, HA * HEAD), BF16)],
        compiler_params=_cparams(1),
        name="mla_cache_kv",
    )(ckv, kr, wk, wv, gkn, gkr)


def _odd_in_kernel(h_ref, w_ref, gq_ref, gk_ref, cos_ref, sin_ref,
                   q_ref, k_ref, v_ref, kf_ref, vf_ref, *, n_ctx_tiles, tm):
    is_ctx = pl.program_id(0) < n_ctx_tiles
    k0 = HC * HEAD
    v0 = k0 + KVH_C * HEAD
    cache_rows = []
    for r in range(0, tm, M_CHUNK):
        rows = slice(r, r + M_CHUNK)
        cos = cos_ref[rows, :]
        sin = sin_ref[rows, :]
        h = h_ref[rows, :]
        q = jnp.dot(h, w_ref[:, :k0], preferred_element_type=F32)
        for hd in range(HC):
            y = _rms(q[:, hd * HEAD:(hd + 1) * HEAD], gq_ref[...], HEAD)
            q_ref[rows, hd * HEAD:(hd + 1) * HEAD] = _rope(y, cos, sin, HEAD // 4).astype(BF16)
        k = jnp.dot(h, w_ref[:, k0:v0], preferred_element_type=F32)
        k_heads = []
        for hd in range(KVH_C):
            y = _rms(k[:, hd * HEAD:(hd + 1) * HEAD], gk_ref[...], HEAD)
            k_ref[rows, hd * HEAD:(hd + 1) * HEAD] = _rope(y, cos, sin, HEAD // 4).astype(BF16)
            k_heads.append(y)
        v = jnp.dot(h, w_ref[:, v0:], preferred_element_type=F32)
        v_ref[rows, :] = v.astype(BF16)
        cache_rows.append((rows, k_heads, v))

    @pl.when(is_ctx)
    def _():
        for rows, k_heads, v in cache_rows:
            vf_ref[rows, :] = v
            for hd, y in enumerate(k_heads):
                kf_ref[rows, hd * HEAD:(hd + 1) * HEAD] = y


def _odd_in(h, w, tm=512):
    n_ctx = T_CTX // tm
    small = lambda a: pl.BlockSpec(a.shape, lambda i: (0, 0))
    rows = lambda n: pl.BlockSpec((tm, n), lambda i: (i, 0))
    ctx_rows = lambda n: pl.BlockSpec((tm, n), lambda i: (jnp.minimum(i, n_ctx - 1), 0))
    table = pl.BlockSpec((tm, 128), lambda i: (_rope_block_of_tile(i, tm), 0))
    tok = lambda n, dt: jax.ShapeDtypeStruct((T, n), dt)
    ctx = lambda n: jax.ShapeDtypeStruct((T_CTX, n), F32)
    cos, sin = _rope_tables(HEAD)
    return pl.pallas_call(
        functools.partial(_odd_in_kernel, n_ctx_tiles=n_ctx, tm=tm),
        grid=(T // tm,),
        in_specs=[rows(D), pl.BlockSpec(w["w_in"].shape, lambda i: (0, 0), pipeline_mode=pl.Buffered(1)),
                  small(w["gq"]), small(w["gk"]), table, table],
        out_specs=[rows(HC * HEAD), rows(KVH_C * HEAD), rows(KVH_C * HEAD), ctx_rows(KVH_C * HEAD),
                   ctx_rows(KVH_C * HEAD)],
        out_shape=[tok(HC * HEAD, BF16), tok(KVH_C * HEAD, BF16), tok(KVH_C * HEAD, BF16), ctx(KVH_C * HEAD),
                   ctx(KVH_C * HEAD)],
        compiler_params=_cparams(1),
        name="odd_in",
    )(h, w["w_in"], w["gq"], w["gk"], cos, sin)


def _build_na_bias(toep_ref, bias_scr, hps):
    neg = jnp.full((GRID_W, GRID_W), NEG_INF, F32)
    qc = lax.broadcasted_iota(jnp.int32, (GRID_W, GRID_W), 0)
    kc = lax.broadcasted_iota(jnp.int32, (GRID_W, GRID_W), 1)
    lo = jnp.clip(qc - NA_KW // 2, 0, GRID_W - NA_KW)
    col_valid = (kc >= lo) & (kc < lo + NA_KW)
    for hh in range(hps):
        tiles = []
        for off in range(2 * NA_KH - 1):
            v = jnp.broadcast_to(toep_ref[hh, off:off + 1, :], (GRID_W, 128))
            t = pltpu.roll(v, 128 - (NA_KW - 1), 1, stride=1, stride_axis=0)[:, :GRID_W]
            tiles.append(jnp.where(col_valid, t, NEG_INF))
        for r in range(GRID_H):
            rs = min(max(r - NA_KH // 2, 0), GRID_H - NA_KH)
            for kr in range(GRID_H):
                tile = tiles[kr - r + NA_KH - 1] if rs <= kr < rs + NA_KH else neg
                bias_scr[hh, r * GRID_W:(r + 1) * GRID_W, kr * GRID_W:(kr + 1) * GRID_W] = tile


def _attn_kernel(*refs, plan, n_kv, bias_mode, has_sink, hps, groups, dk, dv, qb, c):
    it = iter(refs)
    q_ref = next(it)
    kv_refs = [(next(it), next(it)) for _ in range(n_kv)]
    bias_ref = next(it) if bias_mode is not None else None
    sink_ref = next(it) if has_sink else None
    o_ref = next(it)
    bias_scr = next(it) if bias_mode == "na" else None
    hblk = pl.program_id(0)

    if bias_mode == "na":
        @pl.when(pl.program_id(1) == 0)
        def _():
            _build_na_bias(bias_ref, bias_scr, hps)

    def one_head(hh, head, q_cols, o_cols):
        sink = sink_ref[hblk * hps * groups + head] if has_sink else None
        for qi, segs in enumerate(plan):
            q0 = qi * qb
            q = q_ref[q0:q0 + qb, q_cols]
            scores = []
            for (si, start, length, biased) in segs:
                k = kv_refs[si][0][start:start + length, hh * dk:(hh + 1) * dk].astype(BF16)
                t = lax.dot_general(q, k, _NT, preferred_element_type=F32) * c
                if biased and bias_mode == "na":
                    t = t + bias_scr[hh, q0:q0 + qb, start:start + length]
                elif biased:
                    t = t + bias_ref[qi, :, :length]
                scores.append(t)
            m = jnp.max(scores[0], axis=-1, keepdims=True)
            for t in scores[1:]:
                m = jnp.maximum(m, jnp.max(t, axis=-1, keepdims=True))
            if has_sink:
                m = jnp.maximum(m, sink)
            acc = None
            for t, (si, start, length, _) in zip(scores, segs):
                p = jnp.exp2(t - m).astype(BF16)
                v = kv_refs[si][1][start:start + length, hh * dv:(hh + 1) * dv].astype(BF16)
                v1 = jnp.concatenate([v, jnp.ones((length, dv), BF16)], axis=-1)
                pv = jnp.dot(p, v1, preferred_element_type=F32)
                acc = pv if acc is None else acc + pv
            denom = acc[:, dv:]
            if has_sink:
                denom = denom + jnp.exp2(sink - m)
            o_ref[q0:q0 + qb, o_cols] = (acc[:, :dv] / denom).astype(o_ref.dtype)

    for hh in range(hps):
        if groups > 1 and len(plan) > 1:
            def group_body(g, carry, hh=hh):
                head = hh * groups + g
                one_head(hh, head, pl.ds(pl.multiple_of(head * dk, dk), dk), pl.ds(pl.multiple_of(head * dv, dv), dv))
                return carry

            lax.fori_loop(0, groups, group_body, 0, unroll=2)
        else:
            for g in range(groups):
                head = hh * groups + g
                one_head(hh, head, slice(head * dk, (head + 1) * dk), slice(head * dv, (head + 1) * dv))


def _attention(q, kvs, plan, *, nb, sq, q_rowblk0, n_kv_heads, hps, groups, dk, dv, scale,
               bias=None, bias_mode=None, sink=None, name="attention"):
    grid = (n_kv_heads // hps, nb)
    in_specs = [pl.BlockSpec((sq, hps * groups * dk), lambda h, b: (q_rowblk0 + b, h))]
    args = [q]
    for (k, v, sk, r0) in kvs:
        in_specs.append(pl.BlockSpec((sk, hps * dk), lambda h, b, r0=r0: (r0 + b, h)))
        in_specs.append(pl.BlockSpec((sk, hps * dv), lambda h, b, r0=r0: (r0 + b, h)))
        args += [k, v]
    scratch = []
    if bias_mode == "na":
        in_specs.append(pl.BlockSpec((hps,) + bias.shape[1:], lambda h, b: (h, 0, 0)))
        args.append(bias)
        scratch.append(pltpu.VMEM((hps, LAT_S, LAT_S), F32))
    elif bias_mode == "table":
        in_specs.append(pl.BlockSpec(bias.shape, lambda h, b: (0, 0, 0)))
        args.append(bias)
    if sink is not None:
        in_specs.append(pl.BlockSpec(memory_space=pltpu.SMEM))
        args.append(sink)
    body = functools.partial(
        _attn_kernel, plan=plan, n_kv=len(kvs), bias_mode=bias_mode, has_sink=sink is not None,
        hps=hps, groups=groups, dk=dk, dv=dv, qb=min(Q_BLOCK, sq), c=scale * LOG2E)
    return pl.pallas_call(
        body,
        grid=grid,
        in_specs=in_specs,
        out_specs=pl.BlockSpec((sq, hps * groups * dv), lambda h, b: (b, h)),
        out_shape=jax.ShapeDtypeStruct((nb * sq, n_kv_heads * groups * dv), BF16),
        scratch_shapes=scratch,
        compiler_params=_cparams(2),
        name=name,
    )(*args)


_CTX_PLAN = [[(0, 0, CTX_S, False)]]
_N_QB = LAT_S // Q_BLOCK
_MLA_PLAN = [[(0, 0, LAT_S, False), (1, 0, PAST, False)] for _ in range(_N_QB)]
_NA_ROWS = [(0, 8), (0, 12), (4, 16), (8, 16)]
_NA_PLAN = [[(0, lo * GRID_W, (hi - lo) * GRID_W, True), (1, 0, PAST, False)] for lo, hi in _NA_ROWS]
_WIN_SPANS = [(max(i * Q_BLOCK - WINDOW, 0), min((i + 1) * Q_BLOCK + WINDOW, LAT_S)) for i in range(_N_QB)]
_WIN_PLAN = [[(0, lo, hi - lo, True), (1, 0, PAST, False)] for lo, hi in _WIN_SPANS]


def _window_bias():
    width = max(hi - lo for lo, hi in _WIN_SPANS)
    out = np.full((_N_QB, Q_BLOCK, width), NEG_INF, np.float32)
    for i, (lo, hi) in enumerate(_WIN_SPANS):
        q_abs = i * Q_BLOCK + np.arange(Q_BLOCK)[:, None]
        k_abs = lo + np.arange(hi - lo)[None, :]
        out[i, :, :hi - lo] = np.where(np.abs(q_abs - k_abs) <= WINDOW, 0.0, NEG_INF)
    return jnp.asarray(out)


def _rope_tables(rot_dim):
    half = rot_dim // 2
    nf = half // 2
    t = np.arange(LAT_S)
    inv_freq = ROPE_THETA ** (-np.arange(nf, dtype=np.float64) / nf)
    cos = np.zeros((2 * LAT_S, 128), np.float64)
    sin = np.zeros((2 * LAT_S, 128), np.float64)
    cos[:LAT_S, :rot_dim] = 1.0
    for part, pos in enumerate((t // GRID_W, t % GRID_W)):
        ang = pos[:, None].astype(np.float64) * inv_freq[None, :]
        lo = part * half
        cos[LAT_S:, lo:lo + nf] = np.cos(ang)
        cos[LAT_S:, lo + nf:lo + half] = np.cos(ang)
        sin[LAT_S:, lo:lo + nf] = -np.sin(ang)
        sin[LAT_S:, lo + nf:lo + half] = np.sin(ang)
    return jnp.asarray(cos, F32), jnp.asarray(sin, F32)


def _na_bias_rows(rpb):
    n_off, n_rel = rpb.shape[1], rpb.shape[2]
    return jnp.pad(rpb * LOG2E, ((0, 0), (0, 16 - n_off), (0, 128 - n_rel)))


def _even_mixer(x, h, mods, g_next, w, caches):
    cache_ckv, cache_krope, cache_nak, cache_nav = caches
    qm, km, vm, qna, kna, vna, new_ckv, new_kr, new_nak, new_nav = _even_in(h, w)
    km_ctx, vm_ctx = _cache_kv(cache_ckv, cache_krope, w["wk"], w["wv"], w["gkn"], w["gkr"])
    s_mla = 1.0 / math.sqrt(QK)
    s_na = 1.0 / math.sqrt(HEAD)
    lat0 = T_CTX // LAT_S
    o_mla_ctx = _attention(qm, [(km, vm, CTX_S, 0)], _CTX_PLAN, nb=CTX_B, sq=CTX_S, q_rowblk0=0, n_kv_heads=HA,
                           hps=HA, groups=1, dk=QK_PAD, dv=HEAD, scale=s_mla, name="mla_ctx")
    o_na_ctx = _attention(qna, [(kna, vna, CTX_S, 0)], _CTX_PLAN, nb=CTX_B, sq=CTX_S, q_rowblk0=0, n_kv_heads=HB,
                          hps=HB, groups=1, dk=HEAD, dv=HEAD, scale=s_na, name="na_ctx")
    o_mla_lat = _attention(qm, [(km, vm, LAT_S, lat0), (km_ctx, vm_ctx, PAST, 0)], _MLA_PLAN, nb=LAT_B, sq=LAT_S,
                           q_rowblk0=lat0, n_kv_heads=HA, hps=4, groups=1, dk=QK_PAD, dv=HEAD, scale=s_mla,
                           name="mla_lat")
    o_na_lat = _attention(qna, [(kna, vna, LAT_S, lat0), (cache_nak, cache_nav, PAST, 0)], _NA_PLAN, nb=LAT_B,
                          sq=LAT_S, q_rowblk0=lat0, n_kv_heads=HB, hps=4, groups=1, dk=HEAD, dv=HEAD, scale=s_na,
                          bias=w["na_toep"], bias_mode="na", name="na_lat")
    x, h = _out_proj([o_mla_ctx, o_na_ctx], [o_mla_lat, o_na_lat], w["w_out"], w["w_out_lead"], x, mods, g_next)
    return x, h, (new_ckv.reshape(CTX_B, CTX_S, KV_LORA), new_kr.reshape(CTX_B, CTX_S, ROPE),
                  new_nak.reshape(CTX_B, CTX_S, HB, HEAD), new_nav.reshape(CTX_B, CTX_S, HB, HEAD))


def _odd_mixer(x, h, mods, g_next, w, caches):
    cache_k, cache_v = caches
    q, k, v, new_k, new_v = _odd_in(h, w)
    scale = 1.0 / math.sqrt(HEAD)
    lat0 = T_CTX // LAT_S
    o_ctx = _attention(q, [(k, v, CTX_S, 0)], _CTX_PLAN, nb=CTX_B, sq=CTX_S, q_rowblk0=0, n_kv_heads=KVH_C,
                       hps=KVH_C, groups=GROUPS_C, dk=HEAD, dv=HEAD, scale=scale, sink=w["sink"], name="gqa_ctx")
    o_lat = _attention(q, [(k, v, LAT_S, lat0), (cache_k, cache_v, PAST, 0)], _WIN_PLAN, nb=LAT_B, sq=LAT_S,
                       q_rowblk0=lat0, n_kv_heads=KVH_C, hps=1, groups=GROUPS_C, dk=HEAD, dv=HEAD, scale=scale,
                       bias=_window_bias(), bias_mode="table", sink=w["sink"], name="gqa_lat")
    x, h = _out_proj([o_ctx], [o_lat], w["w_out"], w["w_out_lead"], x, mods, g_next)
    return x, h, (new_k.reshape(CTX_B, CTX_S, KVH_C, HEAD), new_v.reshape(CTX_B, CTX_S, KVH_C, HEAD))


def _even_weights(e, even_w_in, w_out_bf16, mla_q_norm, mla_w_q_up, mla_kv_norm, mla_w_kv_up, mla_qk_norm,
                  na_qk_norm, na_rpb):
    w_in = even_w_in[e]
    i0, i1, i2 = Q_LORA, Q_LORA + KV_LORA, Q_LORA + KV_LORA + ROPE
    w_in = jnp.concatenate([w_in[:, :i1], w_in[:, i2:], w_in[:, i1:i2], jnp.zeros((D, 128 - ROPE), F32)], axis=1)
    wq = jnp.pad(mla_w_q_up[e].reshape(Q_LORA, HA, QK), ((0, 0), (0, 0), (0, QK_PAD - QK)))
    wkv = mla_w_kv_up[e].reshape(KV_LORA, HA, NOPE + HEAD)
    qk = mla_qk_norm[e]
    return {
        "w_in": w_in.astype(BF16),
        "w_out": w_out_bf16,
        "w_out_lead": (e,),
        "wq": wq.reshape(Q_LORA, HA * QK_PAD).astype(BF16),
        "wk": wkv[:, :, :NOPE].reshape(KV_LORA, HA * NOPE).astype(BF16),
        "wv": wkv[:, :, NOPE:].reshape(KV_LORA, HA * HEAD).astype(BF16),
        "q_norm": mla_q_norm[e][None, :],
        "kv_norm": mla_kv_norm[e][None, :],
        "gq": jnp.pad(qk[0], (0, QK_PAD - QK))[None, :],
        "gkn": qk[1, :NOPE][None, :],
        "gkr": jnp.pad(qk[1, NOPE:], (0, 128 - ROPE))[None, :],
        "gnaq": na_qk_norm[e, 0][None, :],
        "gnak": na_qk_norm[e, 1][None, :],
        "na_toep": _na_bias_rows(na_rpb[e]),
    }


def kernel(x_prompt, x_sample, cache_mla_ckv, cache_mla_krope, cache_na_k, cache_na_v, cache_gqa_k, cache_gqa_v, c, c_ctx, ada_w, ada_b, norm_g, ffn_w_in, ffn_w_out, even_w_in, even_w_out, mla_q_norm, mla_w_q_up, mla_kv_norm, mla_w_kv_up, mla_qk_norm, na_qk_norm, na_rpb, odd_w_in, odd_w_out, gqa_qk_norm, gqa_sink):
    depth = ada_w.shape[0]
    cond = jnp.concatenate([c_ctx[None, :], c, jnp.zeros((16 - N_GROUPS, D), F32)], axis=0)
    mods_all = _ada_modulation(cond, ada_w, ada_b)
    layer_mods = [mods_all[layer, :N_GROUPS].reshape(N_GROUPS, N_MOD, D) for layer in range(depth)]
    even_w_out16 = even_w_out.astype(BF16)
    odd_w_out16 = odd_w_out.astype(BF16)
    n_ctx_tiles = T_CTX // FFN_OUT_TM

    ckv_l, kr_l, nak_l, nav_l, gk_l, gv_l = [], [], [], [], [], []
    y_prompt = y_sample = None
    x, h = _embed(x_prompt, x_sample, layer_mods[0], norm_g[0, 0:1])
    for layer in range(depth):
        mods = layer_mods[layer]
        g = norm_g[layer]
        e = layer // 2
        act, w_out16 = _ffn_in(h, ffn_w_in, ffn_w_out, (layer, 0))
        x, h = _ffn_out(act, w_out16, x, mods, 2, nxt=(mods, g[1:2], 1))
        if layer % 2 == 0:
            w = _even_weights(e, even_w_in, even_w_out16, mla_q_norm, mla_w_q_up, mla_kv_norm, mla_w_kv_up,
                              mla_qk_norm, na_qk_norm, na_rpb)
            caches = (cache_mla_ckv[:, e].reshape(LAT_B * PAST, KV_LORA),
                      jnp.pad(cache_mla_krope[:, e].reshape(LAT_B * PAST, ROPE), ((0, 0), (0, 128 - ROPE))),
                      cache_na_k[:, e].reshape(LAT_B * PAST, HB * HEAD),
                      cache_na_v[:, e].reshape(LAT_B * PAST, HB * HEAD))
            x, h, (ckv, kr, nak, nav) = _even_mixer(x, h, mods, g[2:3], w, caches)
            ckv_l.append(ckv)
            kr_l.append(kr)
            nak_l.append(nak)
            nav_l.append(nav)
        else:
            w = {"w_in": odd_w_in[e].astype(BF16), "w_out": odd_w_out16, "w_out_lead": (e,),
                 "gq": gqa_qk_norm[e, 0][None, :], "gk": gqa_qk_norm[e, 1][None, :],
                 "sink": gqa_sink[e] * LOG2E}
            caches = (cache_gqa_k[:, e].reshape(LAT_B * PAST, KVH_C * HEAD),
                      cache_gqa_v[:, e].reshape(LAT_B * PAST, KVH_C * HEAD))
            x, h, (gk, gv) = _odd_mixer(x, h, mods, g[2:3], w, caches)
            gk_l.append(gk)
            gv_l.append(gv)
        act, w_out16 = _ffn_in(h, ffn_w_in, ffn_w_out, (layer, 1))
        if layer + 1 < depth:
            x, h = _ffn_out(act, w_out16, x, mods, 8, nxt=(layer_mods[layer + 1], norm_g[layer + 1, 0:1], 0))
        else:
            y_prompt = _ffn_out(act, w_out16, x, mods, 8, tile0=0, n_tiles=n_ctx_tiles).reshape(CTX_B, CTX_S, D)
            y_sample = _ffn_out(act, w_out16, x, mods, 8, tile0=n_ctx_tiles,
                                n_tiles=T // FFN_OUT_TM - n_ctx_tiles).reshape(LAT_B, LAT_S, D)

    return (y_prompt, y_sample, jnp.stack(ckv_l, axis=1), jnp.stack(kr_l, axis=1), jnp.stack(nak_l, axis=1),
            jnp.stack(nav_l, axis=1), jnp.stack(gk_l, axis=1), jnp.stack(gv_l, axis=1))
```

```python
import functools
import math

import numpy as np
import jax
import jax.numpy as jnp
from jax import lax
from jax.experimental import pallas as pl
from jax.experimental.pallas import tpu as pltpu

F32 = jnp.float32
BF16 = jnp.bfloat16

D = 2048
D_FF = 5632
N_MOD = 9
CTX_B, CTX_S = 16, 256
LAT_B, LAT_S = 8, 1024
PAST = 512
T_CTX = CTX_B * CTX_S
T_LAT = LAT_B * LAT_S
T = T_CTX + T_LAT
GRID_W = 64
GRID_H = LAT_S // GRID_W
N_GROUPS = 1 + LAT_B
HEAD = 128
HA = 8
NOPE, ROPE = 128, 64
QK = NOPE + ROPE
QK_PAD = 256
Q_LORA = 512
KV_LORA = 512
HB = 8
NA_KH, NA_KW = 8, 16
HC, KVH_C = 16, 4
GROUPS_C = HC // KVH_C
WINDOW = 128
IN_EVEN_PAD = 4224
IN_ODD = 3072
ROPE_THETA = 10000.0
EPS = 1e-6
NEG_INF = -1e30
LOG2E = math.log2(math.e)

VMEM_LIMIT = 56 * 1024 * 1024
Q_BLOCK = 256
M_CHUNK = 256
EMBED_CHUNK = 32
WO_COL_TILES = 4
FFN_OUT_TM = 512

_NT = (((1,), (1,)), ((), ()))


def _cparams(n_axes):
    return pltpu.CompilerParams(dimension_semantics=("arbitrary",) * n_axes,
                                vmem_limit_bytes=VMEM_LIMIT)


def _group_of_tile(i, tm):
    n_ctx = T_CTX // tm
    per = LAT_S // tm
    return jnp.where(i < n_ctx, 0, (i - n_ctx) // per + 1)


def _rope_block_of_tile(i, tm):
    n_ctx = T_CTX // tm
    per = LAT_S // tm
    return jnp.where(i < n_ctx, 0, per + (i - n_ctx) % per)


def _ada_kernel(c_ref, w_ref, b_ref, o_ref):
    c = c_ref[...]
    a = (c * jax.nn.sigmoid(c)).astype(BF16)
    o_ref[0] = jnp.dot(a, w_ref[0].astype(BF16), preferred_element_type=F32) + b_ref[0]


def _ada_modulation(cond, ada_w, ada_b, tn=2048):
    depth = ada_w.shape[0]
    n = N_MOD * D
    rows = cond.shape[0]
    return pl.pallas_call(
        _ada_kernel,
        grid=(depth, n // tn),
        in_specs=[
            pl.BlockSpec((rows, D), lambda l, j: (0, 0)),
            pl.BlockSpec((1, D, tn), lambda l, j: (l, 0, j)),
            pl.BlockSpec((1, 1, tn), lambda l, j: (l, 0, j)),
        ],
        out_specs=pl.BlockSpec((1, rows, tn), lambda l, j: (l, 0, j)),
        out_shape=jax.ShapeDtypeStruct((depth, rows, n), F32),
        compiler_params=_cparams(2),
        name="ada_modulation",
    )(cond, ada_w, ada_b.reshape(depth, 1, n))


def _mod_rows(mod_ref, g_ref, which):
    shift = mod_ref[0, 3 * which:3 * which + 1, :]
    scale = mod_ref[0, 3 * which + 1:3 * which + 2, :]
    return g_ref[...] * (1.0 + scale), shift


def _modulate(x, gain, shift):
    inv = lax.rsqrt(jnp.mean(x * x, axis=-1, keepdims=True) + EPS)
    return ((x * inv) * gain + shift).astype(BF16)


def _embed_kernel(xp_ref, xs_ref, mod_ref, g_ref, x_ref, h_ref, *, n_ctx_tiles, tm):
    gain, shift = _mod_rows(mod_ref, g_ref, 0)

    def run(src_ref):
        for r in range(0, tm, EMBED_CHUNK):
            x = src_ref[r:r + EMBED_CHUNK, :]
            x_ref[r:r + EMBED_CHUNK, :] = x
            h_ref[r:r + EMBED_CHUNK, :] = _modulate(x, gain, shift)

    @pl.when(pl.program_id(0) < n_ctx_tiles)
    def _():
        run(xp_ref)

    @pl.when(pl.program_id(0) >= n_ctx_tiles)
    def _():
        run(xs_ref)


def _embed(x_prompt, x_sample, mods, g, tm=512):
    n_ctx = T_CTX // tm
    return pl.pallas_call(
        functools.partial(_embed_kernel, n_ctx_tiles=n_ctx, tm=tm),
        grid=(T // tm,),
        in_specs=[
            pl.BlockSpec((tm, D), lambda i: (jnp.minimum(i, n_ctx - 1), 0)),
            pl.BlockSpec((tm, D), lambda i: (jnp.maximum(i - n_ctx, 0), 0)),
            pl.BlockSpec((1, N_MOD, D), lambda i: (_group_of_tile(i, tm), 0, 0)),
            pl.BlockSpec((1, D), lambda i: (0, 0)),
        ],
        out_specs=[pl.BlockSpec((tm, D), lambda i: (i, 0)), pl.BlockSpec((tm, D), lambda i: (i, 0))],
        out_shape=[jax.ShapeDtypeStruct((T, D), F32), jax.ShapeDtypeStruct((T, D), BF16)],
        compiler_params=_cparams(1),
        name="embed",
    )(x_prompt.reshape(T_CTX, D), x_sample.reshape(T_LAT, D), mods, g)


def _ffn_in_kernel(h_ref, wg_ref, wu_ref, wo_ref, o_ref, wo16_ref, *, tm):
    @pl.when(pl.program_id(0) < WO_COL_TILES)
    def _():
        wo16_ref[...] = wo_ref[...].astype(BF16)

    wg = wg_ref[...].astype(BF16)
    wu = wu_ref[...].astype(BF16)
    for r in range(0, tm, M_CHUNK):
        h = h_ref[r:r + M_CHUNK, :]
        gate = jnp.dot(h, wg, preferred_element_type=F32)
        up = jnp.dot(h, wu, preferred_element_type=F32)
        o_ref[r:r + M_CHUNK, :] = (gate * jax.nn.sigmoid(gate) * up).astype(BF16)


def _ffn_in(h, w_in, w_out, lead, tm=2048, tn=512):
    nj = D_FF // tn
    n_lead = len(lead)
    wo_cols = D // WO_COL_TILES
    assert nj * tn == D_FF and T // tm >= WO_COL_TILES

    def wo_tile(i, j):
        parked = i >= WO_COL_TILES
        return jnp.where(parked, nj - 1, j), jnp.minimum(i, WO_COL_TILES - 1)

    return pl.pallas_call(
        functools.partial(_ffn_in_kernel, tm=tm),
        grid=(T // tm, nj),
        in_specs=[
            pl.BlockSpec((tm, D), lambda i, j: (i, 0)),
            pl.BlockSpec((None,) * n_lead + (D, tn), lambda i, j: lead + (0, j)),
            pl.BlockSpec((None,) * n_lead + (D, tn), lambda i, j: lead + (0, j + nj)),
            pl.BlockSpec((None,) * n_lead + (tn, wo_cols), lambda i, j: lead + wo_tile(i, j)),
        ],
        out_specs=[pl.BlockSpec((tm, tn), lambda i, j: (i, j)),
                   pl.BlockSpec((tn, wo_cols), wo_tile)],
        out_shape=[jax.ShapeDtypeStruct((T, D_FF), BF16), jax.ShapeDtypeStruct((D_FF, D), BF16)],
        compiler_params=_cparams(2),
        name="ffn_in",
    )(h, w_in, w_in, w_out)


def _ffn_out_kernel(*refs, gate_row, next_which, tm):
    if next_which is None:
        a_ref, w_ref, x_ref, mod_ref, o_ref = refs
    else:
        a_ref, w_ref, x_ref, mod_ref, modn_ref, gn_ref, o_ref, h_ref = refs
        gain, shift = _mod_rows(modn_ref, gn_ref, next_which)
    gate = 0.5 * mod_ref[0, gate_row:gate_row + 1, :]
    for r in range(0, tm, M_CHUNK):
        acc = jnp.dot(a_ref[r:r + M_CHUNK, :], w_ref[...], preferred_element_type=F32)
        x = x_ref[r:r + M_CHUNK, :] + gate * acc
        o_ref[r:r + M_CHUNK, :] = x
        if next_which is not None:
            h_ref[r:r + M_CHUNK, :] = _modulate(x, gain, shift)


def _ffn_out(a, w, x, mods, gate_row, nxt=None, tile0=0, n_tiles=T // FFN_OUT_TM, tm=FFN_OUT_TM):
    k = a.shape[1]
    rows = lambda n: pl.BlockSpec((tm, n), lambda i: (tile0 + i, 0))
    mod_spec = pl.BlockSpec((1, N_MOD, D), lambda i: (_group_of_tile(tile0 + i, tm), 0, 0))
    in_specs = [rows(k), pl.BlockSpec((k, D), lambda i: (0, 0), pipeline_mode=pl.Buffered(1)), rows(D), mod_spec]
    args = [a, w, x, mods]
    out_specs = [pl.BlockSpec((tm, D), lambda i: (i, 0))]
    out_shape = [jax.ShapeDtypeStruct((n_tiles * tm, D), F32)]
    if nxt is not None:
        in_specs += [mod_spec, pl.BlockSpec((1, D), lambda i: (0, 0))]
        args += [nxt[0], nxt[1]]
        out_specs.append(pl.BlockSpec((tm, D), lambda i: (i, 0)))
        out_shape.append(jax.ShapeDtypeStruct((n_tiles * tm, D), BF16))
    out = pl.pallas_call(
        functools.partial(_ffn_out_kernel, gate_row=gate_row, next_which=None if nxt is None else nxt[2], tm=tm),
        grid=(n_tiles,),
        in_specs=in_specs,
        out_specs=out_specs,
        out_shape=out_shape,
        compiler_params=_cparams(1),
        name="ffn_out",
    )(*args)
    return out if nxt is not None else out[0]


def _out_proj_kernel(*refs, widths, n_ctx_tiles, tm):
    n_a = len(widths)
    ctx_refs, lat_refs = refs[:n_a], refs[n_a:2 * n_a]
    w_ref, x_ref, mod_ref, g_ref, o_ref, h_ref = refs[2 * n_a:]
    gate = mod_ref[0, 5:6, :]
    gain, shift = _mod_rows(mod_ref, g_ref, 2)

    def run(a_refs):
        for r in range(0, tm, M_CHUNK):
            acc = None
            k0 = 0
            for a_ref, kw in zip(a_refs, widths):
                part = jnp.dot(a_ref[r:r + M_CHUNK, :], w_ref[k0:k0 + kw, :], preferred_element_type=F32)
                acc = part if acc is None else acc + part
                k0 += kw
            x = x_ref[r:r + M_CHUNK, :] + gate * acc
            o_ref[r:r + M_CHUNK, :] = x
            h_ref[r:r + M_CHUNK, :] = _modulate(x, gain, shift)

    @pl.when(pl.program_id(0) < n_ctx_tiles)
    def _():
        run(ctx_refs)

    @pl.when(pl.program_id(0) >= n_ctx_tiles)
    def _():
        run(lat_refs)


def _out_proj(a_ctx, a_lat, w, lead, x, mods, g_next, tm=512):
    widths = tuple(a.shape[1] for a in a_ctx)
    n_ctx = T_CTX // tm
    specs = [pl.BlockSpec((tm, kw), lambda i: (jnp.minimum(i, n_ctx - 1), 0)) for kw in widths]
    specs += [pl.BlockSpec((tm, kw), lambda i: (jnp.maximum(i - n_ctx, 0), 0)) for kw in widths]
    return pl.pallas_call(
        functools.partial(_out_proj_kernel, widths=widths, n_ctx_tiles=n_ctx, tm=tm),
        grid=(T // tm,),
        in_specs=specs + [
            pl.BlockSpec((None,) * len(lead) + (D, D), lambda i: lead + (0, 0), pipeline_mode=pl.Buffered(1)),
            pl.BlockSpec((tm, D), lambda i: (i, 0)),
            pl.BlockSpec((1, N_MOD, D), lambda i: (_group_of_tile(i, tm), 0, 0)),
            pl.BlockSpec((1, D), lambda i: (0, 0)),
        ],
        out_specs=[pl.BlockSpec((tm, D), lambda i: (i, 0)), pl.BlockSpec((tm, D), lambda i: (i, 0))],
        out_shape=[jax.ShapeDtypeStruct((T, D), F32), jax.ShapeDtypeStruct((T, D), BF16)],
        compiler_params=_cparams(1),
        name="mixer_out_proj",
    )(*a_ctx, *a_lat, w, x, mods, g_next)


def _rope(x, cos, sin, nf):
    lane = lax.broadcasted_iota(jnp.int32, x.shape, 1)
    first = (lane & (2 * nf - 1)) < nf
    partner = jnp.where(first, pltpu.roll(x, 128 - nf, 1), pltpu.roll(x, nf, 1))
    return x * cos + partner * sin


def _rms(x, g, n):
    inv = lax.rsqrt(jnp.sum(x * x, axis=-1, keepdims=True) * (1.0 / n) + EPS)
    return x * inv * g


def _mla_keys_values(ckvn, kr, wk_ref, wv_ref, gkn_ref, gkr_ref, cos, sin, km_ref, vm_ref):
    c16 = ckvn.astype(BF16)
    kn = jnp.dot(c16, wk_ref[...], preferred_element_type=F32)
    vm_ref[...] = jnp.dot(c16, wv_ref[...], preferred_element_type=F32).astype(BF16)
    kr_ss = jnp.sum(kr * kr, axis=-1, keepdims=True)
    for h in range(HA):
        x = kn[:, h * NOPE:(h + 1) * NOPE]
        inv = lax.rsqrt((jnp.sum(x * x, axis=-1, keepdims=True) + kr_ss) * (1.0 / QK) + EPS)
        km_ref[:, h * QK_PAD:h * QK_PAD + NOPE] = (x * inv * gkn_ref[...]).astype(BF16)
        r = kr * inv * gkr_ref[...]
        if cos is not None:
            r = _rope(r, cos, sin, ROPE // 4)
        km_ref[:, h * QK_PAD + NOPE:(h + 1) * QK_PAD] = r.astype(BF16)


def _even_in_kernel(h_ref, w_ref, wq_ref, wk_ref, wv_ref, qn_ref, kvn_ref, gq_ref, gkn_ref,
                    gkr_ref, gnaq_ref, gnak_ref, cos_ref, sin_ref,
                    qm_ref, km_ref, vm_ref, qna_ref, kna_ref, vna_ref,
                    ckv_ref, kr_ref, knaf_ref, vnaf_ref, *, n_ctx_tiles):
    is_ctx = pl.program_id(0) < n_ctx_tiles
    cos = cos_ref[...]
    sin = sin_ref[...]
    h = h_ref[...]
    proj = lambda lo, hi: jnp.dot(h, w_ref[:, lo:hi], preferred_element_type=F32)
    base = Q_LORA + KV_LORA

    low = proj(0, base)
    kr = proj(IN_EVEN_PAD - 128, IN_EVEN_PAD)
    cq = _rms(low[:, :Q_LORA], qn_ref[...], Q_LORA)
    q = jnp.dot(cq.astype(BF16), wq_ref[...], preferred_element_type=F32)
    for hd in range(HA):
        y = _rms(q[:, hd * QK_PAD:(hd + 1) * QK_PAD], gq_ref[...], QK)
        qm_ref[:, hd * QK_PAD:hd * QK_PAD + NOPE] = y[:, :NOPE].astype(BF16)
        qm_ref[:, hd * QK_PAD + NOPE:(hd + 1) * QK_PAD] = _rope(y[:, NOPE:], cos, sin, ROPE // 4).astype(BF16)
    ckvn = _rms(low[:, Q_LORA:], kvn_ref[...], KV_LORA)
    _mla_keys_values(ckvn, kr, wk_ref, wv_ref, gkn_ref, gkr_ref, cos, sin, km_ref, vm_ref)

    qn = proj(base, base + HB * HEAD)
    for hd in range(HB):
        qna_ref[:, hd * HEAD:(hd + 1) * HEAD] = _rms(qn[:, hd * HEAD:(hd + 1) * HEAD], gnaq_ref[...],
                                                     HEAD).astype(BF16)
    kn = proj(base + HB * HEAD, base + 2 * HB * HEAD)
    k_heads = []
    for hd in range(HB):
        kh = _rms(kn[:, hd * HEAD:(hd + 1) * HEAD], gnak_ref[...], HEAD)
        kna_ref[:, hd * HEAD:(hd + 1) * HEAD] = kh.astype(BF16)
        k_heads.append(kh)
    vn = proj(base + 2 * HB * HEAD, base + 3 * HB * HEAD)
    vna_ref[...] = vn.astype(BF16)

    @pl.when(is_ctx)
    def _():
        ckv_ref[...] = ckvn
        kr_ref[...] = kr[:, :ROPE]
        vnaf_ref[...] = vn
        for hd, kh in enumerate(k_heads):
            knaf_ref[:, hd * HEAD:(hd + 1) * HEAD] = kh


def _even_in(h, w, tm=256):
    n_ctx = T_CTX // tm
    resident = lambda a: pl.BlockSpec(a.shape, lambda i: (0, 0), pipeline_mode=pl.Buffered(1))
    small = lambda a: pl.BlockSpec(a.shape, lambda i: (0, 0))
    rows = lambda n: pl.BlockSpec((tm, n), lambda i: (i, 0))
    ctx_rows = lambda n: pl.BlockSpec((tm, n), lambda i: (jnp.minimum(i, n_ctx - 1), 0))
    table = pl.BlockSpec((tm, 128), lambda i: (_rope_block_of_tile(i, tm), 0))
    tok = lambda n, dt: jax.ShapeDtypeStruct((T, n), dt)
    ctx = lambda n: jax.ShapeDtypeStruct((T_CTX, n), F32)
    cos, sin = _rope_tables(ROPE)
    norms = [w[k] for k in ("q_norm", "kv_norm", "gq", "gkn", "gkr", "gnaq", "gnak")]
    return pl.pallas_call(
        functools.partial(_even_in_kernel, n_ctx_tiles=n_ctx),
        grid=(T // tm,),
        in_specs=[rows(D), resident(w["w_in"]), resident(w["wq"]), resident(w["wk"]), resident(w["wv"])]
                 + [small(a) for a in norms] + [table, table],
        out_specs=[rows(HA * QK_PAD), rows(HA * QK_PAD), rows(HA * HEAD), rows(HB * HEAD), rows(HB * HEAD),
                   rows(HB * HEAD), ctx_rows(KV_LORA), ctx_rows(ROPE), ctx_rows(HB * HEAD), ctx_rows(HB * HEAD)],
        out_shape=[tok(HA * QK_PAD, BF16), tok(HA * QK_PAD, BF16), tok(HA * HEAD, BF16), tok(HB * HEAD, BF16),
                   tok(HB * HEAD, BF16), tok(HB * HEAD, BF16), ctx(KV_LORA), ctx(ROPE), ctx(HB * HEAD),
                   ctx(HB * HEAD)],
        compiler_params=_cparams(1),
        name="even_in",
    )(h, w["w_in"], w["wq"], w["wk"], w["wv"], *norms, cos, sin)


def _cache_kv_kernel(ckv_ref, kr_ref, wk_ref, wv_ref, gkn_ref, gkr_ref, km_ref, vm_ref):
    _mla_keys_values(ckv_ref[...], kr_ref[...], wk_ref, wv_ref, gkn_ref, gkr_ref, None, None, km_ref, vm_ref)


def _cache_kv(ckv, kr, wk, wv, gkn, gkr, tm=512):
    n = ckv.shape[0]
    full = lambda shape: pl.BlockSpec(shape, lambda i: (0, 0))
    rows = lambda w: pl.BlockSpec((tm, w), lambda i: (i, 0))
    return pl.pallas_call(
        _cache_kv_kernel,
        grid=(n // tm,),
        in_specs=[rows(KV_LORA), rows(128), full(wk.shape), full(wv.shape), full(gkn.shape), full(gkr.shape)],
        out_specs=[rows(HA * QK_PAD), rows(HA * HEAD)],
        out_shape=[jax.ShapeDtypeStruct((n, HA * QK_PAD), BF16), jax.ShapeDtypeStruct((n, HA * HEAD), BF16)],
        compiler_params=_cparams(1),
        name="mla_cache_kv",
    )(ckv, kr, wk, wv, gkn, gkr)


def _odd_in_kernel(h_ref, w_ref, gq_ref, gk_ref, cos_ref, sin_ref,
                   q_ref, k_ref, v_ref, kf_ref, vf_ref, *, n_ctx_tiles, tm):
    is_ctx = pl.program_id(0) < n_ctx_tiles
    k0 = HC * HEAD
    v0 = k0 + KVH_C * HEAD
    cache_rows = []
    for r in range(0, tm, M_CHUNK):
        rows = slice(r, r + M_CHUNK)
        cos = cos_ref[rows, :]
        sin = sin_ref[rows, :]
        h = h_ref[rows, :]
        q = jnp.dot(h, w_ref[:, :k0], preferred_element_type=F32)
        for hd in range(HC):
            y = _rms(q[:, hd * HEAD:(hd + 1) * HEAD], gq_ref[...], HEAD)
            q_ref[rows, hd * HEAD:(hd + 1) * HEAD] = _rope(y, cos, sin, HEAD // 4).astype(BF16)
        k = jnp.dot(h, w_ref[:, k0:v0], preferred_element_type=F32)
        k_heads = []
        for hd in range(KVH_C):
            y = _rms(k[:, hd * HEAD:(hd + 1) * HEAD], gk_ref[...], HEAD)
            k_ref[rows, hd * HEAD:(hd + 1) * HEAD] = _rope(y, cos, sin, HEAD // 4).astype(BF16)
            k_heads.append(y)
        v = jnp.dot(h, w_ref[:, v0:], preferred_element_type=F32)
        v_ref[rows, :] = v.astype(BF16)
        cache_rows.append((rows, k_heads, v))

    @pl.when(is_ctx)
    def _():
        for rows, k_heads, v in cache_rows:
            vf_ref[rows, :] = v
            for hd, y in enumerate(k_heads):
                kf_ref[rows, hd * HEAD:(hd + 1) * HEAD] = y


def _odd_in(h, w, tm=512):
    n_ctx = T_CTX // tm
    small = lambda a: pl.BlockSpec(a.shape, lambda i: (0, 0))
    rows = lambda n: pl.BlockSpec((tm, n), lambda i: (i, 0))
    ctx_rows = lambda n: pl.BlockSpec((tm, n), lambda i: (jnp.minimum(i, n_ctx - 1), 0))
    table = pl.BlockSpec((tm, 128), lambda i: (_rope_block_of_tile(i, tm), 0))
    tok = lambda n, dt: jax.ShapeDtypeStruct((T, n), dt)
    ctx = lambda n: jax.ShapeDtypeStruct((T_CTX, n), F32)
    cos, sin = _rope_tables(HEAD)
    return pl.pallas_call(
        functools.partial(_odd_in_kernel, n_ctx_tiles=n_ctx, tm=tm),
        grid=(T // tm,),
        in_specs=[rows(D), pl.BlockSpec(w["w_in"].shape, lambda i: (0, 0), pipeline_mode=pl.Buffered(1)),
                  small(w["gq"]), small(w["gk"]), table, table],
        out_specs=[rows(HC * HEAD), rows(KVH_C * HEAD), rows(KVH_C * HEAD), ctx_rows(KVH_C * HEAD),
                   ctx_rows(KVH_C * HEAD)],
        out_shape=[tok(HC * HEAD, BF16), tok(KVH_C * HEAD, BF16), tok(KVH_C * HEAD, BF16), ctx(KVH_C * HEAD),
                   ctx(KVH_C * HEAD)],
        compiler_params=_cparams(1),
        name="odd_in",
    )(h, w["w_in"], w["gq"], w["gk"], cos, sin)


def _build_na_bias(toep_ref, bias_scr, hps):
    neg = jnp.full((GRID_W, GRID_W), NEG_INF, F32)
    qc = lax.broadcasted_iota(jnp.int32, (GRID_W, GRID_W), 0)
    kc = lax.broadcasted_iota(jnp.int32, (GRID_W, GRID_W), 1)
    lo = jnp.clip(qc - NA_KW // 2, 0, GRID_W - NA_KW)
    col_valid = (kc >= lo) & (kc < lo + NA_KW)
    for hh in range(hps):
        tiles = []
        for off in range(2 * NA_KH - 1):
            v = jnp.broadcast_to(toep_ref[hh, off:off + 1, :], (GRID_W, 128))
            t = pltpu.roll(v, 128 - (NA_KW - 1), 1, stride=1, stride_axis=0)[:, :GRID_W]
            tiles.append(jnp.where(col_valid, t, NEG_INF))
        for r in range(GRID_H):
            rs = min(max(r - NA_KH // 2, 0), GRID_H - NA_KH)
            for kr in range(GRID_H):
                tile = tiles[kr - r + NA_KH - 1] if rs <= kr < rs + NA_KH else neg
                bias_scr[hh, r * GRID_W:(r + 1) * GRID_W, kr * GRID_W:(kr + 1) * GRID_W] = tile


def _attn_kernel(*refs, plan, n_kv, bias_mode, has_sink, hps, groups, dk, dv, qb, c):
    it = iter(refs)
    q_ref = next(it)
    kv_refs = [(next(it), next(it)) for _ in range(n_kv)]
    bias_ref = next(it) if bias_mode is not None else None
    sink_ref = next(it) if has_sink else None
    o_ref = next(it)
    bias_scr = next(it) if bias_mode == "na" else None
    hblk = pl.program_id(0)

    if bias_mode == "na":
        @pl.when(pl.program_id(1) == 0)
        def _():
            _build_na_bias(bias_ref, bias_scr, hps)

    def one_head(hh, head, q_cols, o_cols):
        sink = sink_ref[hblk * hps * groups + head] if has_sink else None
        for qi, segs in enumerate(plan):
            q0 = qi * qb
            q = q_ref[q0:q0 + qb, q_cols]
            scores = []
            for (si, start, length, biased) in segs:
                k = kv_refs[si][0][start:start + length, hh * dk:(hh + 1) * dk].astype(BF16)
                t = lax.dot_general(q, k, _NT, preferred_element_type=F32) * c
                if biased and bias_mode == "na":
                    t = t + bias_scr[hh, q0:q0 + qb, start:start + length]
                elif biased:
                    t = t + bias_ref[qi, :, :length]
                scores.append(t)
            m = jnp.max(scores[0], axis=-1, keepdims=True)
            for t in scores[1:]:
                m = jnp.maximum(m, jnp.max(t, axis=-1, keepdims=True))
            if has_sink:
                m = jnp.maximum(m, sink)
            acc = None
            for t, (si, start, length, _) in zip(scores, segs):
                p = jnp.exp2(t - m).astype(BF16)
                v = kv_refs[si][1][start:start + length, hh * dv:(hh + 1) * dv].astype(BF16)
                v1 = jnp.concatenate([v, jnp.ones((length, dv), BF16)], axis=-1)
                pv = jnp.dot(p, v1, preferred_element_type=F32)
                acc = pv if acc is None else acc + pv
            denom = acc[:, dv:]
            if has_sink:
                denom = denom + jnp.exp2(sink - m)
            o_ref[q0:q0 + qb, o_cols] = (acc[:, :dv] / denom).astype(o_ref.dtype)

    for hh in range(hps):
        if groups > 1 and len(plan) > 1:
            def group_body(g, carry, hh=hh):
                head = hh * groups + g
                one_head(hh, head, pl.ds(pl.multiple_of(head * dk, dk), dk), pl.ds(pl.multiple_of(head * dv, dv), dv))
                return carry

            lax.fori_loop(0, groups, group_body, 0, unroll=2)
        else:
            for g in range(groups):
                head = hh * groups + g
                one_head(hh, head, slice(head * dk, (head + 1) * dk), slice(head * dv, (head + 1) * dv))


def _attention(q, kvs, plan, *, nb, sq, q_rowblk0, n_kv_heads, hps, groups, dk, dv, scale,
               bias=None, bias_mode=None, sink=None, name="attention"):
    grid = (n_kv_heads // hps, nb)
    in_specs = [pl.BlockSpec((sq, hps * groups * dk), lambda h, b: (q_rowblk0 + b, h))]
    args = [q]
    for (k, v, sk, r0) in kvs:
        in_specs.append(pl.BlockSpec((sk, hps * dk), lambda h, b, r0=r0: (r0 + b, h)))
        in_specs.append(pl.BlockSpec((sk, hps * dv), lambda h, b, r0=r0: (r0 + b, h)))
        args += [k, v]
    scratch = []
    if bias_mode == "na":
        in_specs.append(pl.BlockSpec((hps,) + bias.shape[1:], lambda h, b: (h, 0, 0)))
        args.append(bias)
        scratch.append(pltpu.VMEM((hps, LAT_S, LAT_S), F32))
    elif bias_mode == "table":
        in_specs.append(pl.BlockSpec(bias.shape, lambda h, b: (0, 0, 0)))
        args.append(bias)
    if sink is not None:
        in_specs.append(pl.BlockSpec(memory_space=pltpu.SMEM))
        args.append(sink)
    body = functools.partial(
        _attn_kernel, plan=plan, n_kv=len(kvs), bias_mode=bias_mode, has_sink=sink is not None,
        hps=hps, groups=groups, dk=dk, dv=dv, qb=min(Q_BLOCK, sq), c=scale * LOG2E)
    return pl.pallas_call(
        body,
        grid=grid,
        in_specs=in_specs,
        out_specs=pl.BlockSpec((sq, hps * groups * dv), lambda h, b: (b, h)),
        out_shape=jax.ShapeDtypeStruct((nb * sq, n_kv_heads * groups * dv), BF16),
        scratch_shapes=scratch,
        compiler_params=_cparams(2),
        name=name,
    )(*args)


CTX_PER_STEP = 4
_CTX_ROWS = CTX_PER_STEP * CTX_S
_CTX_PLAN = [[(0, i * CTX_S, CTX_S, False)] for i in range(CTX_PER_STEP)]
_N_QB = LAT_S // Q_BLOCK
_MLA_PLAN = [[(0, 0, LAT_S, False), (1, 0, PAST, False)] for _ in range(_N_QB)]
_NA_ROWS = [(0, 8), (0, 12), (4, 16), (8, 16)]
_NA_PLAN = [[(0, lo * GRID_W, (hi - lo) * GRID_W, True), (1, 0, PAST, False)] for lo, hi in _NA_ROWS]
_WIN_SPANS = [(max(i * Q_BLOCK - WINDOW, 0), min((i + 1) * Q_BLOCK + WINDOW, LAT_S)) for i in range(_N_QB)]
_WIN_PLAN = [[(0, lo, hi - lo, True), (1, 0, PAST, False)] for lo, hi in _WIN_SPANS]


def _window_bias():
    width = max(hi - lo for lo, hi in _WIN_SPANS)
    out = np.full((_N_QB, Q_BLOCK, width), NEG_INF, np.float32)
    for i, (lo, hi) in enumerate(_WIN_SPANS):
        q_abs = i * Q_BLOCK + np.arange(Q_BLOCK)[:, None]
        k_abs = lo + np.arange(hi - lo)[None, :]
        out[i, :, :hi - lo] = np.where(np.abs(q_abs - k_abs) <= WINDOW, 0.0, NEG_INF)
    return jnp.asarray(out)


def _rope_tables(rot_dim):
    half = rot_dim // 2
    nf = half // 2
    t = np.arange(LAT_S)
    inv_freq = ROPE_THETA ** (-np.arange(nf, dtype=np.float64) / nf)
    cos = np.zeros((2 * LAT_S, 128), np.float64)
    sin = np.zeros((2 * LAT_S, 128), np.float64)
    cos[:LAT_S, :rot_dim] = 1.0
    for part, pos in enumerate((t // GRID_W, t % GRID_W)):
        ang = pos[:, None].astype(np.float64) * inv_freq[None, :]
        lo = part * half
        cos[LAT_S:, lo:lo + nf] = np.cos(ang)
        cos[LAT_S:, lo + nf:lo + half] = np.cos(ang)
        sin[LAT_S:, lo:lo + nf] = -np.sin(ang)
        sin[LAT_S:, lo + nf:lo + half] = np.sin(ang)
    return jnp.asarray(cos, F32), jnp.asarray(sin, F32)


def _na_bias_rows(rpb):
    n_off, n_rel = rpb.shape[1], rpb.shape[2]
    return jnp.pad(rpb * LOG2E, ((0, 0), (0, 16 - n_off), (0, 128 - n_rel)))


def _even_mixer(x, h, mods, g_next, w, caches):
    cache_ckv, cache_krope, cache_nak, cache_nav = caches
    qm, km, vm, qna, kna, vna, new_ckv, new_kr, new_nak, new_nav = _even_in(h, w)
    km_ctx, vm_ctx = _cache_kv(cache_ckv, cache_krope, w["wk"], w["wv"], w["gkn"], w["gkr"])
    s_mla = 1.0 / math.sqrt(QK)
    s_na = 1.0 / math.sqrt(HEAD)
    lat0 = T_CTX // LAT_S
    n_ctx_steps = CTX_B // CTX_PER_STEP
    o_mla_ctx = _attention(qm, [(km, vm, _CTX_ROWS, 0)], _CTX_PLAN, nb=n_ctx_steps, sq=_CTX_ROWS, q_rowblk0=0,
                           n_kv_heads=HA, hps=HA, groups=1, dk=QK_PAD, dv=HEAD, scale=s_mla, name="mla_ctx")
    o_na_ctx = _attention(qna, [(kna, vna, _CTX_ROWS, 0)], _CTX_PLAN, nb=n_ctx_steps, sq=_CTX_ROWS, q_rowblk0=0,
                          n_kv_heads=HB, hps=HB, groups=1, dk=HEAD, dv=HEAD, scale=s_na, name="na_ctx")
    o_mla_lat = _attention(qm, [(km, vm, LAT_S, lat0), (km_ctx, vm_ctx, PAST, 0)], _MLA_PLAN, nb=LAT_B, sq=LAT_S,
                           q_rowblk0=lat0, n_kv_heads=HA, hps=4, groups=1, dk=QK_PAD, dv=HEAD, scale=s_mla,
                           name="mla_lat")
    o_na_lat = _attention(qna, [(kna, vna, LAT_S, lat0), (cache_nak, cache_nav, PAST, 0)], _NA_PLAN, nb=LAT_B,
                          sq=LAT_S, q_rowblk0=lat0, n_kv_heads=HB, hps=4, groups=1, dk=HEAD, dv=HEAD, scale=s_na,
                          bias=w["na_toep"], bias_mode="na", name="na_lat")
    x, h = _out_proj([o_mla_ctx, o_na_ctx], [o_mla_lat, o_na_lat], w["w_out"], w["w_out_lead"], x, mods, g_next)
    return x, h, (new_ckv.reshape(CTX_B, CTX_S, KV_LORA), new_kr.reshape(CTX_B, CTX_S, ROPE),
                  new_nak.reshape(CTX_B, CTX_S, HB, HEAD), new_nav.reshape(CTX_B, CTX_S, HB, HEAD))


def _odd_mixer(x, h, mods, g_next, w, caches):
    cache_k, cache_v = caches
    q, k, v, new_k, new_v = _odd_in(h, w)
    scale = 1.0 / math.sqrt(HEAD)
    lat0 = T_CTX // LAT_S
    o_ctx = _attention(q, [(k, v, CTX_S, 0)], _CTX_PLAN[:1], nb=CTX_B, sq=CTX_S, q_rowblk0=0, n_kv_heads=KVH_C,
                       hps=KVH_C, groups=GROUPS_C, dk=HEAD, dv=HEAD, scale=scale, sink=w["sink"], name="gqa_ctx")
    o_lat = _attention(q, [(k, v, LAT_S, lat0), (cache_k, cache_v, PAST, 0)], _WIN_PLAN, nb=LAT_B, sq=LAT_S,
                       q_rowblk0=lat0, n_kv_heads=KVH_C, hps=1, groups=GROUPS_C, dk=HEAD, dv=HEAD, scale=scale,
                       bias=_window_bias(), bias_mode="table", sink=w["sink"], name="gqa_lat")
    x, h = _out_proj([o_ctx], [o_lat], w["w_out"], w["w_out_lead"], x, mods, g_next)
    return x, h, (new_k.reshape(CTX_B, CTX_S, KVH_C, HEAD), new_v.reshape(CTX_B, CTX_S, KVH_C, HEAD))


def _even_weights(e, even_w_in, w_out_bf16, mla_q_norm, mla_w_q_up, mla_kv_norm, mla_w_kv_up, mla_qk_norm,
                  na_qk_norm, na_rpb):
    w_in = even_w_in[e]
    i0, i1, i2 = Q_LORA, Q_LORA + KV_LORA, Q_LORA + KV_LORA + ROPE
    w_in = jnp.concatenate([w_in[:, :i1], w_in[:, i2:], w_in[:, i1:i2], jnp.zeros((D, 128 - ROPE), F32)], axis=1)
    wq = jnp.pad(mla_w_q_up[e].reshape(Q_LORA, HA, QK), ((0, 0), (0, 0), (0, QK_PAD - QK)))
    wkv = mla_w_kv_up[e].reshape(KV_LORA, HA, NOPE + HEAD)
    qk = mla_qk_norm[e]
    return {
        "w_in": w_in.astype(BF16),
        "w_out": w_out_bf16,
        "w_out_lead": (e,),
        "wq": wq.reshape(Q_LORA, HA * QK_PAD).astype(BF16),
        "wk": wkv[:, :, :NOPE].reshape(KV_LORA, HA * NOPE).astype(BF16),
        "wv": wkv[:, :, NOPE:].reshape(KV_LORA, HA * HEAD).astype(BF16),
        "q_norm": mla_q_norm[e][None, :],
        "kv_norm": mla_kv_norm[e][None, :],
        "gq": jnp.pad(qk[0], (0, QK_PAD - QK))[None, :],
        "gkn": qk[1, :NOPE][None, :],
        "gkr": jnp.pad(qk[1, NOPE:], (0, 128 - ROPE))[None, :],
        "gnaq": na_qk_norm[e, 0][None, :],
        "gnak": na_qk_norm[e, 1][None, :],
        "na_toep": _na_bias_rows(na_rpb[e]),
    }


def kernel(x_prompt, x_sample, cache_mla_ckv, cache_mla_krope, cache_na_k, cache_na_v, cache_gqa_k, cache_gqa_v, c, c_ctx, ada_w, ada_b, norm_g, ffn_w_in, ffn_w_out, even_w_in, even_w_out, mla_q_norm, mla_w_q_up, mla_kv_norm, mla_w_kv_up, mla_qk_norm, na_qk_norm, na_rpb, odd_w_in, odd_w_out, gqa_qk_norm, gqa_sink):
    depth = ada_w.shape[0]
    cond = jnp.concatenate([c_ctx[None, :], c, jnp.zeros((16 - N_GROUPS, D), F32)], axis=0)
    mods_all = _ada_modulation(cond, ada_w, ada_b)
    layer_mods = [mods_all[layer, :N_GROUPS].reshape(N_GROUPS, N_MOD, D) for layer in range(depth)]
    even_w_out16 = even_w_out.astype(BF16)
    odd_w_out16 = odd_w_out.astype(BF16)
    n_ctx_tiles = T_CTX // FFN_OUT_TM

    ckv_l, kr_l, nak_l, nav_l, gk_l, gv_l = [], [], [], [], [], []
    y_prompt = y_sample = None
    x, h = _embed(x_prompt, x_sample, layer_mods[0], norm_g[0, 0:1])
    for layer in range(depth):
        mods = layer_mods[layer]
        g = norm_g[layer]
        e = layer // 2
        act, w_out16 = _ffn_in(h, ffn_w_in, ffn_w_out, (layer, 0))
        x, h = _ffn_out(act, w_out16, x, mods, 2, nxt=(mods, g[1:2], 1))
        if layer % 2 == 0:
            w = _even_weights(e, even_w_in, even_w_out16, mla_q_norm, mla_w_q_up, mla_kv_norm, mla_w_kv_up,
                              mla_qk_norm, na_qk_norm, na_rpb)
            caches = (cache_mla_ckv[:, e].reshape(LAT_B * PAST, KV_LORA),
                      jnp.pad(cache_mla_krope[:, e].reshape(LAT_B * PAST, ROPE), ((0, 0), (0, 128 - ROPE))),
                      cache_na_k[:, e].reshape(LAT_B * PAST, HB * HEAD),
                      cache_na_v[:, e].reshape(LAT_B * PAST, HB * HEAD))
            x, h, (ckv, kr, nak, nav) = _even_mixer(x, h, mods, g[2:3], w, caches)
            ckv_l.append(ckv)
            kr_l.append(kr)
            nak_l.append(nak)
            nav_l.append(nav)
        else:
            w = {"w_in": odd_w_in[e].astype(BF16), "w_out": odd_w_out16, "w_out_lead": (e,),
                 "gq": gqa_qk_norm[e, 0][None, :], "gk": gqa_qk_norm[e, 1][None, :],
                 "sink": gqa_sink[e] * LOG2E}
            caches = (cache_gqa_k[:, e].reshape(LAT_B * PAST, KVH_C * HEAD),
                      cache_gqa_v[:, e].reshape(LAT_B * PAST, KVH_C * HEAD))
            x, h, (gk, gv) = _odd_mixer(x, h, mods, g[2:3], w, caches)
            gk_l.append(gk)
            gv_l.append(gv)
        act, w_out16 = _ffn_in(h, ffn_w_in, ffn_w_out, (layer, 1))
        if layer + 1 < depth:
            x, h = _ffn_out(act, w_out16, x, mods, 8, nxt=(layer_mods[layer + 1], norm_g[layer + 1, 0:1], 0))
        else:
            y_prompt = _ffn_out(act, w_out16, x, mods, 8, tile0=0, n_tiles=n_ctx_tiles).reshape(CTX_B, CTX_S, D)
            y_sample = _ffn_out(act, w_out16, x, mods, 8, tile0=n_ctx_tiles,
                                n_tiles=T // FFN_OUT_TM - n_ctx_tiles).reshape(LAT_B, LAT_S, D)

    return (y_prompt, y_sample, jnp.stack(ckv_l, axis=1), jnp.stack(kr_l, axis=1), jnp.stack(nak_l, axis=1),
            jnp.stack(nav_l, axis=1), jnp.stack(gk_l, axis=1), jnp.stack(gv_l, axis=1))
```

```python
import functools
import math

import numpy as np
import jax
import jax.numpy as jnp
from jax import lax
from jax.experimental import pallas as pl
from jax.experimental.pallas import tpu as pltpu

F32 = jnp.float32
BF16 = jnp.bfloat16

D = 2048
D_FF = 5632
N_MOD = 9
CTX_B, CTX_S = 16, 256
LAT_B, LAT_S = 8, 1024
PAST = 512
T_CTX = CTX_B * CTX_S
T_LAT = LAT_B * LAT_S
T = T_CTX + T_LAT
GRID_W = 64
GRID_H = LAT_S // GRID_W
N_GROUPS = 1 + LAT_B
HEAD = 128
HA = 8
NOPE, ROPE = 128, 64
QK = NOPE + ROPE
QK_PAD = 256
Q_LORA = 512
KV_LORA = 512
HB = 8
NA_KH, NA_KW = 8, 16
HC, KVH_C = 16, 4
GROUPS_C = HC // KVH_C
WINDOW = 128
IN_EVEN_PAD = 4224
IN_ODD = 3072
ROPE_THETA = 10000.0
EPS = 1e-6
NEG_INF = -1e30
LOG2E = math.log2(math.e)

VMEM_LIMIT = 56 * 1024 * 1024
Q_BLOCK = 256
M_CHUNK = 256
EMBED_CHUNK = 32
WO_COL_TILES = 4
FFN_OUT_TM = 512

_NT = (((1,), (1,)), ((), ()))


def _cparams(n_axes):
    return pltpu.CompilerParams(dimension_semantics=("arbitrary",) * n_axes,
                                vmem_limit_bytes=VMEM_LIMIT)


def _group_of_tile(i, tm):
    n_ctx = T_CTX // tm
    per = LAT_S // tm
    return jnp.where(i < n_ctx, 0, (i - n_ctx) // per + 1)


def _rope_block_of_tile(i, tm):
    n_ctx = T_CTX // tm
    per = LAT_S // tm
    return jnp.where(i < n_ctx, 0, per + (i - n_ctx) % per)


def _ada_kernel(c_ref, w_ref, b_ref, o_ref):
    c = c_ref[...]
    a = (c * jax.nn.sigmoid(c)).astype(BF16)
    o_ref[0] = jnp.dot(a, w_ref[0].astype(BF16), preferred_element_type=F32) + b_ref[0]


def _ada_modulation(cond, ada_w, ada_b, tn=2048):
    depth = ada_w.shape[0]
    n = N_MOD * D
    rows = cond.shape[0]
    return pl.pallas_call(
        _ada_kernel,
        grid=(depth, n // tn),
        in_specs=[
            pl.BlockSpec((rows, D), lambda l, j: (0, 0)),
            pl.BlockSpec((1, D, tn), lambda l, j: (l, 0, j)),
            pl.BlockSpec((1, 1, tn), lambda l, j: (l, 0, j)),
        ],
        out_specs=pl.BlockSpec((1, rows, tn), lambda l, j: (l, 0, j)),
        out_shape=jax.ShapeDtypeStruct((depth, rows, n), F32),
        compiler_params=_cparams(2),
        name="ada_modulation",
    )(cond, ada_w, ada_b.reshape(depth, 1, n))


def _mod_rows(mod_ref, g_ref, which):
    shift = mod_ref[0, 3 * which:3 * which + 1, :]
    scale = mod_ref[0, 3 * which + 1:3 * which + 2, :]
    return g_ref[...] * (1.0 + scale), shift


def _modulate(x, gain, shift):
    inv = lax.rsqrt(jnp.mean(x * x, axis=-1, keepdims=True) + EPS)
    return ((x * inv) * gain + shift).astype(BF16)


def _embed_kernel(xp_ref, xs_ref, mod_ref, g_ref, x_ref, h_ref, *, n_ctx_tiles, tm):
    gain, shift = _mod_rows(mod_ref, g_ref, 0)

    def run(src_ref):
        for r in range(0, tm, EMBED_CHUNK):
            x = src_ref[r:r + EMBED_CHUNK, :]
            x_ref[r:r + EMBED_CHUNK, :] = x
            h_ref[r:r + EMBED_CHUNK, :] = _modulate(x, gain, shift)

    @pl.when(pl.program_id(0) < n_ctx_tiles)
    def _():
        run(xp_ref)

    @pl.when(pl.program_id(0) >= n_ctx_tiles)
    def _():
        run(xs_ref)


def _embed(x_prompt, x_sample, mods, g, tm=512):
    n_ctx = T_CTX // tm
    return pl.pallas_call(
        functools.partial(_embed_kernel, n_ctx_tiles=n_ctx, tm=tm),
        grid=(T // tm,),
        in_specs=[
            pl.BlockSpec((tm, D), lambda i: (jnp.minimum(i, n_ctx - 1), 0)),
            pl.BlockSpec((tm, D), lambda i: (jnp.maximum(i - n_ctx, 0), 0)),
            pl.BlockSpec((1, N_MOD, D), lambda i: (_group_of_tile(i, tm), 0, 0)),
            pl.BlockSpec((1, D), lambda i: (0, 0)),
        ],
        out_specs=[pl.BlockSpec((tm, D), lambda i: (i, 0)), pl.BlockSpec((tm, D), lambda i: (i, 0))],
        out_shape=[jax.ShapeDtypeStruct((T, D), F32), jax.ShapeDtypeStruct((T, D), BF16)],
        compiler_params=_cparams(1),
        name="embed",
    )(x_prompt.reshape(T_CTX, D), x_sample.reshape(T_LAT, D), mods, g)


def _ffn_in_kernel(h_ref, wg_ref, wu_ref, wo_ref, o_ref, wo16_ref, *, tm):
    wo16_ref[...] = wo_ref[...].astype(BF16)

    wg = wg_ref[...].astype(BF16)
    wu = wu_ref[...].astype(BF16)
    for r in range(0, tm, M_CHUNK):
        h = h_ref[r:r + M_CHUNK, :]
        gate = jnp.dot(h, wg, preferred_element_type=F32)
        up = jnp.dot(h, wu, preferred_element_type=F32)
        o_ref[r:r + M_CHUNK, :] = (gate * jax.nn.sigmoid(gate) * up).astype(BF16)


def _ffn_in(h, w_in, w_out, lead, tm=2048, tn=512):
    nj = D_FF // tn
    n_lead = len(lead)
    wo_cols = D // WO_COL_TILES
    assert nj * tn == D_FF and T // tm >= WO_COL_TILES

    def wo_tile(i, j):
        parked = i >= WO_COL_TILES
        return jnp.where(parked, nj - 1, j), jnp.minimum(i, WO_COL_TILES - 1)

    return pl.pallas_call(
        functools.partial(_ffn_in_kernel, tm=tm),
        grid=(T // tm, nj),
        in_specs=[
            pl.BlockSpec((tm, D), lambda i, j: (i, 0)),
            pl.BlockSpec((None,) * n_lead + (D, tn), lambda i, j: lead + (0, j)),
            pl.BlockSpec((None,) * n_lead + (D, tn), lambda i, j: lead + (0, j + nj)),
            pl.BlockSpec((None,) * n_lead + (tn, wo_cols), lambda i, j: lead + wo_tile(i, j)),
        ],
        out_specs=[pl.BlockSpec((tm, tn), lambda i, j: (i, j)),
                   pl.BlockSpec((tn, wo_cols), wo_tile)],
        out_shape=[jax.ShapeDtypeStruct((T, D_FF), BF16), jax.ShapeDtypeStruct((D_FF, D), BF16)],
        compiler_params=_cparams(2),
        name="ffn_in",
    )(h, w_in, w_in, w_out)


def _ffn_out_kernel(*refs, gate_row, next_which, tm):
    if next_which is None:
        a_ref, w_ref, x_ref, mod_ref, o_ref = refs
    else:
        a_ref, w_ref, x_ref, mod_ref, modn_ref, gn_ref, o_ref, h_ref = refs
        gain, shift = _mod_rows(modn_ref, gn_ref, next_which)
    gate = 0.5 * mod_ref[0, gate_row:gate_row + 1, :]
    for r in range(0, tm, M_CHUNK):
        acc = jnp.dot(a_ref[r:r + M_CHUNK, :], w_ref[...], preferred_element_type=F32)
        x = x_ref[r:r + M_CHUNK, :] + gate * acc
        o_ref[r:r + M_CHUNK, :] = x
        if next_which is not None:
            h_ref[r:r + M_CHUNK, :] = _modulate(x, gain, shift)


def _ffn_out(a, w, x, mods, gate_row, nxt=None, tile0=0, n_tiles=T // FFN_OUT_TM, tm=FFN_OUT_TM):
    k = a.shape[1]
    rows = lambda n: pl.BlockSpec((tm, n), lambda i: (tile0 + i, 0))
    mod_spec = pl.BlockSpec((1, N_MOD, D), lambda i: (_group_of_tile(tile0 + i, tm), 0, 0))
    in_specs = [rows(k), pl.BlockSpec((k, D), lambda i: (0, 0), pipeline_mode=pl.Buffered(1)), rows(D), mod_spec]
    args = [a, w, x, mods]
    out_specs = [pl.BlockSpec((tm, D), lambda i: (i, 0))]
    out_shape = [jax.ShapeDtypeStruct((n_tiles * tm, D), F32)]
    if nxt is not None:
        in_specs += [mod_spec, pl.BlockSpec((1, D), lambda i: (0, 0))]
        args += [nxt[0], nxt[1]]
        out_specs.append(pl.BlockSpec((tm, D), lambda i: (i, 0)))
        out_shape.append(jax.ShapeDtypeStruct((n_tiles * tm, D), BF16))
    out = pl.pallas_call(
        functools.partial(_ffn_out_kernel, gate_row=gate_row, next_which=None if nxt is None else nxt[2], tm=tm),
        grid=(n_tiles,),
        in_specs=in_specs,
        out_specs=out_specs,
        out_shape=out_shape,
        compiler_params=_cparams(1),
        name="ffn_out",
    )(*args)
    return out if nxt is not None else out[0]


def _out_proj_kernel(*refs, widths, n_ctx_tiles, tm):
    n_a = len(widths)
    ctx_refs, lat_refs = refs[:n_a], refs[n_a:2 * n_a]
    w_ref, x_ref, mod_ref, g_ref, o_ref, h_ref = refs[2 * n_a:]
    gate = mod_ref[0, 5:6, :]
    gain, shift = _mod_rows(mod_ref, g_ref, 2)

    def run(a_refs):
        for r in range(0, tm, M_CHUNK):
            acc = None
            k0 = 0
            for a_ref, kw in zip(a_refs, widths):
                part = jnp.dot(a_ref[r:r + M_CHUNK, :], w_ref[k0:k0 + kw, :], preferred_element_type=F32)
                acc = part if acc is None else acc + part
                k0 += kw
            x = x_ref[r:r + M_CHUNK, :] + gate * acc
            o_ref[r:r + M_CHUNK, :] = x
            h_ref[r:r + M_CHUNK, :] = _modulate(x, gain, shift)

    @pl.when(pl.program_id(0) < n_ctx_tiles)
    def _():
        run(ctx_refs)

    @pl.when(pl.program_id(0) >= n_ctx_tiles)
    def _():
        run(lat_refs)


def _out_proj(a_ctx, a_lat, w, lead, x, mods, g_next, tm=512):
    widths = tuple(a.shape[1] for a in a_ctx)
    n_ctx = T_CTX // tm
    specs = [pl.BlockSpec((tm, kw), lambda i: (jnp.minimum(i, n_ctx - 1), 0)) for kw in widths]
    specs += [pl.BlockSpec((tm, kw), lambda i: (jnp.maximum(i - n_ctx, 0), 0)) for kw in widths]
    return pl.pallas_call(
        functools.partial(_out_proj_kernel, widths=widths, n_ctx_tiles=n_ctx, tm=tm),
        grid=(T // tm,),
        in_specs=specs + [
            pl.BlockSpec((None,) * len(lead) + (D, D), lambda i: lead + (0, 0), pipeline_mode=pl.Buffered(1)),
            pl.BlockSpec((tm, D), lambda i: (i, 0)),
            pl.BlockSpec((1, N_MOD, D), lambda i: (_group_of_tile(i, tm), 0, 0)),
            pl.BlockSpec((1, D), lambda i: (0, 0)),
        ],
        out_specs=[pl.BlockSpec((tm, D), lambda i: (i, 0)), pl.BlockSpec((tm, D), lambda i: (i, 0))],
        out_shape=[jax.ShapeDtypeStruct((T, D), F32), jax.ShapeDtypeStruct((T, D), BF16)],
        compiler_params=_cparams(1),
        name="mixer_out_proj",
    )(*a_ctx, *a_lat, w, x, mods, g_next)


def _rope(x, cos, sin, nf):
    lane = lax.broadcasted_iota(jnp.int32, x.shape, 1)
    first = (lane & (2 * nf - 1)) < nf
    partner = jnp.where(first, pltpu.roll(x, 128 - nf, 1), pltpu.roll(x, nf, 1))
    return x * cos + partner * sin


def _rms(x, g, n):
    inv = lax.rsqrt(jnp.sum(x * x, axis=-1, keepdims=True) * (1.0 / n) + EPS)
    return x * inv * g


def _mla_keys_values(ckvn, kr, wk_ref, wv_ref, gkn_ref, gkr_ref, cos, sin, km_ref, vm_ref):
    c16 = ckvn.astype(BF16)
    kn = jnp.dot(c16, wk_ref[...], preferred_element_type=F32)
    vm_ref[...] = jnp.dot(c16, wv_ref[...], preferred_element_type=F32).astype(BF16)
    kr_ss = jnp.sum(kr * kr, axis=-1, keepdims=True)
    for h in range(HA):
        x = kn[:, h * NOPE:(h + 1) * NOPE]
        inv = lax.rsqrt((jnp.sum(x * x, axis=-1, keepdims=True) + kr_ss) * (1.0 / QK) + EPS)
        km_ref[:, h * QK_PAD:h * QK_PAD + NOPE] = (x * inv * gkn_ref[...]).astype(BF16)
        r = kr * inv * gkr_ref[...]
        if cos is not None:
            r = _rope(r, cos, sin, ROPE // 4)
        km_ref[:, h * QK_PAD + NOPE:(h + 1) * QK_PAD] = r.astype(BF16)


def _even_in_kernel(h_ref, w_ref, wq_ref, wk_ref, wv_ref, qn_ref, kvn_ref, gq_ref, gkn_ref,
                    gkr_ref, gnaq_ref, gnak_ref, cos_ref, sin_ref,
                    qm_ref, km_ref, vm_ref, qna_ref, kna_ref, vna_ref,
                    ckv_ref, kr_ref, knaf_ref, vnaf_ref, *, n_ctx_tiles):
    is_ctx = pl.program_id(0) < n_ctx_tiles
    cos = cos_ref[...]
    sin = sin_ref[...]
    h = h_ref[...]
    proj = lambda lo, hi: jnp.dot(h, w_ref[:, lo:hi], preferred_element_type=F32)
    base = Q_LORA + KV_LORA

    low = proj(0, base)
    kr = proj(IN_EVEN_PAD - 128, IN_EVEN_PAD)
    cq = _rms(low[:, :Q_LORA], qn_ref[...], Q_LORA)
    q = jnp.dot(cq.astype(BF16), wq_ref[...], preferred_element_type=F32)
    for hd in range(HA):
        y = _rms(q[:, hd * QK_PAD:(hd + 1) * QK_PAD], gq_ref[...], QK)
        qm_ref[:, hd * QK_PAD:hd * QK_PAD + NOPE] = y[:, :NOPE].astype(BF16)
        qm_ref[:, hd * QK_PAD + NOPE:(hd + 1) * QK_PAD] = _rope(y[:, NOPE:], cos, sin, ROPE // 4).astype(BF16)
    ckvn = _rms(low[:, Q_LORA:], kvn_ref[...], KV_LORA)
    _mla_keys_values(ckvn, kr, wk_ref, wv_ref, gkn_ref, gkr_ref, cos, sin, km_ref, vm_ref)

    qn = proj(base, base + HB * HEAD)
    for hd in range(HB):
        qna_ref[:, hd * HEAD:(hd + 1) * HEAD] = _rms(qn[:, hd * HEAD:(hd + 1) * HEAD], gnaq_ref[...],
                                                     HEAD).astype(BF16)
    kn = proj(base + HB * HEAD, base + 2 * HB * HEAD)
    k_heads = []
    for hd in range(HB):
        kh = _rms(kn[:, hd * HEAD:(hd + 1) * HEAD], gnak_ref[...], HEAD)
        kna_ref[:, hd * HEAD:(hd + 1) * HEAD] = kh.astype(BF16)
        k_heads.append(kh)
    vn = proj(base + 2 * HB * HEAD, base + 3 * HB * HEAD)
    vna_ref[...] = vn.astype(BF16)

    @pl.when(is_ctx)
    def _():
        ckv_ref[...] = ckvn
        kr_ref[...] = kr[:, :ROPE]
        vnaf_ref[...] = vn
        for hd, kh in enumerate(k_heads):
            knaf_ref[:, hd * HEAD:(hd + 1) * HEAD] = kh


def _even_in(h, w, tm=256):
    n_ctx = T_CTX // tm
    resident = lambda a: pl.BlockSpec(a.shape, lambda i: (0, 0), pipeline_mode=pl.Buffered(1))
    small = lambda a: pl.BlockSpec(a.shape, lambda i: (0, 0))
    rows = lambda n: pl.BlockSpec((tm, n), lambda i: (i, 0))
    ctx_rows = lambda n: pl.BlockSpec((tm, n), lambda i: (jnp.minimum(i, n_ctx - 1), 0))
    table = pl.BlockSpec((tm, 128), lambda i: (_rope_block_of_tile(i, tm), 0))
    tok = lambda n, dt: jax.ShapeDtypeStruct((T, n), dt)
    ctx = lambda n: jax.ShapeDtypeStruct((T_CTX, n), F32)
    cos, sin = _rope_tables(ROPE)
    norms = [w[k] for k in ("q_norm", "kv_norm", "gq", "gkn", "gkr", "gnaq", "gnak")]
    return pl.pallas_call(
        functools.partial(_even_in_kernel, n_ctx_tiles=n_ctx),
        grid=(T // tm,),
        in_specs=[rows(D), resident(w["w_in"]), resident(w["wq"]), resident(w["wk"]), resident(w["wv"])]
                 + [small(a) for a in norms] + [table, table],
        out_specs=[rows(HA * QK_PAD), rows(HA * QK_PAD), rows(HA * HEAD), rows(HB * HEAD), rows(HB * HEAD),
                   rows(HB * HEAD), ctx_rows(KV_LORA), ctx_rows(ROPE), ctx_rows(HB * HEAD), ctx_rows(HB * HEAD)],
        out_shape=[tok(HA * QK_PAD, BF16), tok(HA * QK_PAD, BF16), tok(HA * HEAD, BF16), tok(HB * HEAD, BF16),
                   tok(HB * HEAD, BF16), tok(HB * HEAD, BF16), ctx(KV_LORA), ctx(ROPE), ctx(HB * HEAD),
                   ctx(HB * HEAD)],
        compiler_params=_cparams(1),
        name="even_in",
    )(h, w["w_in"], w["wq"], w["wk"], w["wv"], *norms, cos, sin)


def _cache_kv_kernel(ckv_ref, kr_ref, wk_ref, wv_ref, gkn_ref, gkr_ref, km_ref, vm_ref):
    _mla_keys_values(ckv_ref[...], kr_ref[...], wk_ref, wv_ref, gkn_ref, gkr_ref, None, None, km_ref, vm_ref)


def _cache_kv(ckv, kr, wk, wv, gkn, gkr, tm=512):
    n = ckv.shape[0]
    full = lambda shape: pl.BlockSpec(shape, lambda i: (0, 0))
    rows = lambda w: pl.BlockSpec((tm, w), lambda i: (i, 0))
    return pl.pallas_call(
        _cache_kv_kernel,
        grid=(n // tm,),
        in_specs=[rows(KV_LORA), rows(128), full(wk.shape), full(wv.shape), full(gkn.shape), full(gkr.shape)],
        out_specs=[rows(HA * QK_PAD), rows(HA * HEAD)],
        out_shape=[jax.ShapeDtypeStruct((n, HA * QK_PAD), BF16), jax.ShapeDtypeStruct((n, HA * HEAD), BF16)],
        compiler_params=_cparams(1),
        name="mla_cache_kv",
    )(ckv, kr, wk, wv, gkn, gkr)


def _odd_in_kernel(h_ref, w_ref, gq_ref, gk_ref, cos_ref, sin_ref,
                   q_ref, k_ref, v_ref, kf_ref, vf_ref, *, n_ctx_tiles, tm):
    is_ctx = pl.program_id(0) < n_ctx_tiles
    k0 = HC * HEAD
    v0 = k0 + KVH_C * HEAD
    cache_rows = []
    for r in range(0, tm, M_CHUNK):
        rows = slice(r, r + M_CHUNK)
        cos = cos_ref[rows, :]
        sin = sin_ref[rows, :]
        h = h_ref[rows, :]
        q = jnp.dot(h, w_ref[:, :k0], preferred_element_type=F32)
        for hd in range(HC):
            y = _rms(q[:, hd * HEAD:(hd + 1) * HEAD], gq_ref[...], HEAD)
            q_ref[rows, hd * HEAD:(hd + 1) * HEAD] = _rope(y, cos, sin, HEAD // 4).astype(BF16)
        k = jnp.dot(h, w_ref[:, k0:v0], preferred_element_type=F32)
        k_heads = []
        for hd in range(KVH_C):
            y = _rms(k[:, hd * HEAD:(hd + 1) * HEAD], gk_ref[...], HEAD)
            k_ref[rows, hd * HEAD:(hd + 1) * HEAD] = _rope(y, cos, sin, HEAD // 4).astype(BF16)
            k_heads.append(y)
        v = jnp.dot(h, w_ref[:, v0:], preferred_element_type=F32)
        v_ref[rows, :] = v.astype(BF16)
        cache_rows.append((rows, k_heads, v))

    @pl.when(is_ctx)
    def _():
        for rows, k_heads, v in cache_rows:
            vf_ref[rows, :] = v
            for hd, y in enumerate(k_heads):
                kf_ref[rows, hd * HEAD:(hd + 1) * HEAD] = y


def _odd_in(h, w, tm=512):
    n_ctx = T_CTX // tm
    small = lambda a: pl.BlockSpec(a.shape, lambda i: (0, 0))
    rows = lambda n: pl.BlockSpec((tm, n), lambda i: (i, 0))
    ctx_rows = lambda n: pl.BlockSpec((tm, n), lambda i: (jnp.minimum(i, n_ctx - 1), 0))
    table = pl.BlockSpec((tm, 128), lambda i: (_rope_block_of_tile(i, tm), 0))
    tok = lambda n, dt: jax.ShapeDtypeStruct((T, n), dt)
    ctx = lambda n: jax.ShapeDtypeStruct((T_CTX, n), F32)
    cos, sin = _rope_tables(HEAD)
    return pl.pallas_call(
        functools.partial(_odd_in_kernel, n_ctx_tiles=n_ctx, tm=tm),
        grid=(T // tm,),
        in_specs=[rows(D), pl.BlockSpec(w["w_in"].shape, lambda i: (0, 0), pipeline_mode=pl.Buffered(1)),
                  small(w["gq"]), small(w["gk"]), table, table],
        out_specs=[rows(HC * HEAD), rows(KVH_C * HEAD), rows(KVH_C * HEAD), ctx_rows(KVH_C * HEAD),
                   ctx_rows(KVH_C * HEAD)],
        out_shape=[tok(HC * HEAD, BF16), tok(KVH_C * HEAD, BF16), tok(KVH_C * HEAD, BF16), ctx(KVH_C * HEAD),
                   ctx(KVH_C * HEAD)],
        compiler_params=_cparams(1),
        name="odd_in",
    )(h, w["w_in"], w["gq"], w["gk"], cos, sin)


def _build_na_bias(toep_ref, bias_scr, hps):
    neg = jnp.full((GRID_W, GRID_W), NEG_INF, F32)
    qc = lax.broadcasted_iota(jnp.int32, (GRID_W, GRID_W), 0)
    kc = lax.broadcasted_iota(jnp.int32, (GRID_W, GRID_W), 1)
    lo = jnp.clip(qc - NA_KW // 2, 0, GRID_W - NA_KW)
    col_valid = (kc >= lo) & (kc < lo + NA_KW)
    for hh in range(hps):
        tiles = []
        for off in range(2 * NA_KH - 1):
            v = jnp.broadcast_to(toep_ref[hh, off:off + 1, :], (GRID_W, 128))
            t = pltpu.roll(v, 128 - (NA_KW - 1), 1, stride=1, stride_axis=0)[:, :GRID_W]
            tiles.append(jnp.where(col_valid, t, NEG_INF))
        for r in range(GRID_H):
            rs = min(max(r - NA_KH // 2, 0), GRID_H - NA_KH)
            for kr in range(GRID_H):
                tile = tiles[kr - r + NA_KH - 1] if rs <= kr < rs + NA_KH else neg
                bias_scr[hh, r * GRID_W:(r + 1) * GRID_W, kr * GRID_W:(kr + 1) * GRID_W] = tile


def _attn_kernel(*refs, plan, n_kv, bias_mode, has_sink, hps, groups, dk, dv, qb, c):
    it = iter(refs)
    q_ref = next(it)
    kv_refs = [(next(it), next(it)) for _ in range(n_kv)]
    bias_ref = next(it) if bias_mode is not None else None
    sink_ref = next(it) if has_sink else None
    o_ref = next(it)
    bias_scr = next(it) if bias_mode == "na" else None
    hblk = pl.program_id(0)

    if bias_mode == "na":
        @pl.when(pl.program_id(1) == 0)
        def _():
            _build_na_bias(bias_ref, bias_scr, hps)

    def one_head(hh, head, q_cols, o_cols):
        sink = sink_ref[hblk * hps * groups + head] if has_sink else None
        for qi, segs in enumerate(plan):
            q0 = qi * qb
            q = q_ref[q0:q0 + qb, q_cols]
            scores = []
            for (si, start, length, biased) in segs:
                k = kv_refs[si][0][start:start + length, hh * dk:(hh + 1) * dk].astype(BF16)
                t = lax.dot_general(q, k, _NT, preferred_element_type=F32) * c
                if biased and bias_mode == "na":
                    t = t + bias_scr[hh, q0:q0 + qb, start:start + length]
                elif biased:
                    t = t + bias_ref[qi, :, :length]
                scores.append(t)
            m = jnp.max(scores[0], axis=-1, keepdims=True)
            for t in scores[1:]:
                m = jnp.maximum(m, jnp.max(t, axis=-1, keepdims=True))
            if has_sink:
                m = jnp.maximum(m, sink)
            acc = None
            for t, (si, start, length, _) in zip(scores, segs):
                p = jnp.exp2(t - m).astype(BF16)
                v = kv_refs[si][1][start:start + length, hh * dv:(hh + 1) * dv].astype(BF16)
                v1 = jnp.concatenate([v, jnp.ones((length, dv), BF16)], axis=-1)
                pv = jnp.dot(p, v1, preferred_element_type=F32)
                acc = pv if acc is None else acc + pv
            denom = acc[:, dv:]
            if has_sink:
                denom = denom + jnp.exp2(sink - m)
            o_ref[q0:q0 + qb, o_cols] = (acc[:, :dv] / denom).astype(o_ref.dtype)

    for hh in range(hps):
        if groups > 1 and len(plan) > 1:
            def group_body(g, carry, hh=hh):
                head = hh * groups + g
                one_head(hh, head, pl.ds(pl.multiple_of(head * dk, dk), dk), pl.ds(pl.multiple_of(head * dv, dv), dv))
                return carry

            lax.fori_loop(0, groups, group_body, 0, unroll=2)
        else:
            for g in range(groups):
                head = hh * groups + g
                one_head(hh, head, slice(head * dk, (head + 1) * dk), slice(head * dv, (head + 1) * dv))


def _attention(q, kvs, plan, *, nb, sq, q_rowblk0, n_kv_heads, hps, groups, dk, dv, scale,
               bias=None, bias_mode=None, sink=None, name="attention"):
    grid = (n_kv_heads // hps, nb)
    in_specs = [pl.BlockSpec((sq, hps * groups * dk), lambda h, b: (q_rowblk0 + b, h))]
    args = [q]
    for (k, v, sk, r0) in kvs:
        in_specs.append(pl.BlockSpec((sk, hps * dk), lambda h, b, r0=r0: (r0 + b, h)))
        in_specs.append(pl.BlockSpec((sk, hps * dv), lambda h, b, r0=r0: (r0 + b, h)))
        args += [k, v]
    scratch = []
    if bias_mode == "na":
        in_specs.append(pl.BlockSpec((hps,) + bias.shape[1:], lambda h, b: (h, 0, 0)))
        args.append(bias)
        scratch.append(pltpu.VMEM((hps, LAT_S, LAT_S), F32))
    elif bias_mode == "table":
        in_specs.append(pl.BlockSpec(bias.shape, lambda h, b: (0, 0, 0)))
        args.append(bias)
    if sink is not None:
        in_specs.append(pl.BlockSpec(memory_space=pltpu.SMEM))
        args.append(sink)
    body = functools.partial(
        _attn_kernel, plan=plan, n_kv=len(kvs), bias_mode=bias_mode, has_sink=sink is not None,
        hps=hps, groups=groups, dk=dk, dv=dv, qb=min(Q_BLOCK, sq), c=scale * LOG2E)
    return pl.pallas_call(
        body,
        grid=grid,
        in_specs=in_specs,
        out_specs=pl.BlockSpec((sq, hps * groups * dv), lambda h, b: (b, h)),
        out_shape=jax.ShapeDtypeStruct((nb * sq, n_kv_heads * groups * dv), BF16),
        scratch_shapes=scratch,
        compiler_params=_cparams(2),
        name=name,
    )(*args)


CTX_PER_STEP = 4
_CTX_ROWS = CTX_PER_STEP * CTX_S
_CTX_PLAN = [[(0, i * CTX_S, CTX_S, False)] for i in range(CTX_PER_STEP)]
_N_QB = LAT_S // Q_BLOCK
_MLA_PLAN = [[(0, 0, LAT_S, False), (1, 0, PAST, False)] for _ in range(_N_QB)]
_NA_ROWS = [(0, 8), (0, 12), (4, 16), (8, 16)]
_NA_PLAN = [[(0, lo * GRID_W, (hi - lo) * GRID_W, True), (1, 0, PAST, False)] for lo, hi in _NA_ROWS]
_WIN_SPANS = [(max(i * Q_BLOCK - WINDOW, 0), min((i + 1) * Q_BLOCK + WINDOW, LAT_S)) for i in range(_N_QB)]
_WIN_PLAN = [[(0, lo, hi - lo, True), (1, 0, PAST, False)] for lo, hi in _WIN_SPANS]


def _window_bias():
    width = max(hi - lo for lo, hi in _WIN_SPANS)
    out = np.full((_N_QB, Q_BLOCK, width), NEG_INF, np.float32)
    for i, (lo, hi) in enumerate(_WIN_SPANS):
        q_abs = i * Q_BLOCK + np.arange(Q_BLOCK)[:, None]
        k_abs = lo + np.arange(hi - lo)[None, :]
        out[i, :, :hi - lo] = np.where(np.abs(q_abs - k_abs) <= WINDOW, 0.0, NEG_INF)
    return jnp.asarray(out)


def _rope_tables(rot_dim):
    half = rot_dim // 2
    nf = half // 2
    t = np.arange(LAT_S)
    inv_freq = ROPE_THETA ** (-np.arange(nf, dtype=np.float64) / nf)
    cos = np.zeros((2 * LAT_S, 128), np.float64)
    sin = np.zeros((2 * LAT_S, 128), np.float64)
    cos[:LAT_S, :rot_dim] = 1.0
    for part, pos in enumerate((t // GRID_W, t % GRID_W)):
        ang = pos[:, None].astype(np.float64) * inv_freq[None, :]
        lo = part * half
        cos[LAT_S:, lo:lo + nf] = np.cos(ang)
        cos[LAT_S:, lo + nf:lo + half] = np.cos(ang)
        sin[LAT_S:, lo:lo + nf] = -np.sin(ang)
        sin[LAT_S:, lo + nf:lo + half] = np.sin(ang)
    return jnp.asarray(cos, F32), jnp.asarray(sin, F32)


def _na_bias_rows(rpb):
    n_off, n_rel = rpb.shape[1], rpb.shape[2]
    return jnp.pad(rpb * LOG2E, ((0, 0), (0, 16 - n_off), (0, 128 - n_rel)))


def _even_mixer(x, h, mods, g_next, w, caches):
    cache_ckv, cache_krope, cache_nak, cache_nav = caches
    qm, km, vm, qna, kna, vna, new_ckv, new_kr, new_nak, new_nav = _even_in(h, w)
    km_ctx, vm_ctx = _cache_kv(cache_ckv, cache_krope, w["wk"], w["wv"], w["gkn"], w["gkr"])
    s_mla = 1.0 / math.sqrt(QK)
    s_na = 1.0 / math.sqrt(HEAD)
    lat0 = T_CTX // LAT_S
    n_ctx_steps = CTX_B // CTX_PER_STEP
    o_mla_ctx = _attention(qm, [(km, vm, _CTX_ROWS, 0)], _CTX_PLAN, nb=n_ctx_steps, sq=_CTX_ROWS, q_rowblk0=0,
                           n_kv_heads=HA, hps=HA, groups=1, dk=QK_PAD, dv=HEAD, scale=s_mla, name="mla_ctx")
    o_na_ctx = _attention(qna, [(kna, vna, _CTX_ROWS, 0)], _CTX_PLAN, nb=n_ctx_steps, sq=_CTX_ROWS, q_rowblk0=0,
                          n_kv_heads=HB, hps=HB, groups=1, dk=HEAD, dv=HEAD, scale=s_na, name="na_ctx")
    o_mla_lat = _attention(qm, [(km, vm, LAT_S, lat0), (km_ctx, vm_ctx, PAST, 0)], _MLA_PLAN, nb=LAT_B, sq=LAT_S,
                           q_rowblk0=lat0, n_kv_heads=HA, hps=8, groups=1, dk=QK_PAD, dv=HEAD, scale=s_mla,
                           name="mla_lat")
    o_na_lat = _attention(qna, [(kna, vna, LAT_S, lat0), (cache_nak, cache_nav, PAST, 0)], _NA_PLAN, nb=LAT_B,
                          sq=LAT_S, q_rowblk0=lat0, n_kv_heads=HB, hps=4, groups=1, dk=HEAD, dv=HEAD, scale=s_na,
                          bias=w["na_toep"], bias_mode="na", name="na_lat")
    x, h = _out_proj([o_mla_ctx, o_na_ctx], [o_mla_lat, o_na_lat], w["w_out"], w["w_out_lead"], x, mods, g_next)
    return x, h, (new_ckv.reshape(CTX_B, CTX_S, KV_LORA), new_kr.reshape(CTX_B, CTX_S, ROPE),
                  new_nak.reshape(CTX_B, CTX_S, HB, HEAD), new_nav.reshape(CTX_B, CTX_S, HB, HEAD))


def _odd_mixer(x, h, mods, g_next, w, caches):
    cache_k, cache_v = caches
    q, k, v, new_k, new_v = _odd_in(h, w)
    scale = 1.0 / math.sqrt(HEAD)
    lat0 = T_CTX // LAT_S
    o_ctx = _attention(q, [(k, v, CTX_S, 0)], _CTX_PLAN[:1], nb=CTX_B, sq=CTX_S, q_rowblk0=0, n_kv_heads=KVH_C,
                       hps=KVH_C, groups=GROUPS_C, dk=HEAD, dv=HEAD, scale=scale, sink=w["sink"], name="gqa_ctx")
    o_lat = _attention(q, [(k, v, LAT_S, lat0), (cache_k, cache_v, PAST, 0)], _WIN_PLAN, nb=LAT_B, sq=LAT_S,
                       q_rowblk0=lat0, n_kv_heads=KVH_C, hps=1, groups=GROUPS_C, dk=HEAD, dv=HEAD, scale=scale,
                       bias=_window_bias(), bias_mode="table", sink=w["sink"], name="gqa_lat")
    x, h = _out_proj([o_ctx], [o_lat], w["w_out"], w["w_out_lead"], x, mods, g_next)
    return x, h, (new_k.reshape(CTX_B, CTX_S, KVH_C, HEAD), new_v.reshape(CTX_B, CTX_S, KVH_C, HEAD))


def _even_weights(e, even_w_in, w_out_bf16, mla_q_norm, mla_w_q_up, mla_kv_norm, mla_w_kv_up, mla_qk_norm,
                  na_qk_norm, na_rpb):
    w_in = even_w_in[e]
    i0, i1, i2 = Q_LORA, Q_LORA + KV_LORA, Q_LORA + KV_LORA + ROPE
    w_in = jnp.concatenate([w_in[:, :i1], w_in[:, i2:], w_in[:, i1:i2], jnp.zeros((D, 128 - ROPE), F32)], axis=1)
    wq = jnp.pad(mla_w_q_up[e].reshape(Q_LORA, HA, QK), ((0, 0), (0, 0), (0, QK_PAD - QK)))
    wkv = mla_w_kv_up[e].reshape(KV_LORA, HA, NOPE + HEAD)
    qk = mla_qk_norm[e]
    return {
        "w_in": w_in.astype(BF16),
        "w_out": w_out_bf16,
        "w_out_lead": (e,),
        "wq": wq.reshape(Q_LORA, HA * QK_PAD).astype(BF16),
        "wk": wkv[:, :, :NOPE].reshape(KV_LORA, HA * NOPE).astype(BF16),
        "wv": wkv[:, :, NOPE:].reshape(KV_LORA, HA * HEAD).astype(BF16),
        "q_norm": mla_q_norm[e][None, :],
        "kv_norm": mla_kv_norm[e][None, :],
        "gq": jnp.pad(qk[0], (0, QK_PAD - QK))[None, :],
        "gkn": qk[1, :NOPE][None, :],
        "gkr": jnp.pad(qk[1, NOPE:], (0, 128 - ROPE))[None, :],
        "gnaq": na_qk_norm[e, 0][None, :],
        "gnak": na_qk_norm[e, 1][None, :],
        "na_toep": _na_bias_rows(na_rpb[e]),
    }


def kernel(x_prompt, x_sample, cache_mla_ckv, cache_mla_krope, cache_na_k, cache_na_v, cache_gqa_k, cache_gqa_v, c, c_ctx, ada_w, ada_b, norm_g, ffn_w_in, ffn_w_out, even_w_in, even_w_out, mla_q_norm, mla_w_q_up, mla_kv_norm, mla_w_kv_up, mla_qk_norm, na_qk_norm, na_rpb, odd_w_in, odd_w_out, gqa_qk_norm, gqa_sink):
    depth = ada_w.shape[0]
    cond = jnp.concatenate([c_ctx[None, :], c, jnp.zeros((16 - N_GROUPS, D), F32)], axis=0)
    mods_all = _ada_modulation(cond, ada_w, ada_b)
    layer_mods = [mods_all[layer, :N_GROUPS].reshape(N_GROUPS, N_MOD, D) for layer in range(depth)]
    even_w_out16 = even_w_out.astype(BF16)
    odd_w_out16 = odd_w_out.astype(BF16)
    n_ctx_tiles = T_CTX // FFN_OUT_TM

    ckv_l, kr_l, nak_l, nav_l, gk_l, gv_l = [], [], [], [], [], []
    y_prompt = y_sample = None
    x, h = _embed(x_prompt, x_sample, layer_mods[0], norm_g[0, 0:1])
    for layer in range(depth):
        mods = layer_mods[layer]
        g = norm_g[layer]
        e = layer // 2
        act, w_out16 = _ffn_in(h, ffn_w_in, ffn_w_out, (layer, 0))
        x, h = _ffn_out(act, w_out16, x, mods, 2, nxt=(mods, g[1:2], 1))
        if layer % 2 == 0:
            w = _even_weights(e, even_w_in, even_w_out16, mla_q_norm, mla_w_q_up, mla_kv_norm, mla_w_kv_up,
                              mla_qk_norm, na_qk_norm, na_rpb)
            caches = (cache_mla_ckv[:, e].reshape(LAT_B * PAST, KV_LORA),
                      jnp.pad(cache_mla_krope[:, e].reshape(LAT_B * PAST, ROPE), ((0, 0), (0, 128 - ROPE))),
                      cache_na_k[:, e].reshape(LAT_B * PAST, HB * HEAD),
                      cache_na_v[:, e].reshape(LAT_B * PAST, HB * HEAD))
            x, h, (ckv, kr, nak, nav) = _even_mixer(x, h, mods, g[2:3], w, caches)
            ckv_l.append(ckv)
            kr_l.append(kr)
            nak_l.append(nak)
            nav_l.append(nav)
        else:
            w = {"w_in": odd_w_in[e].astype(BF16), "w_out": odd_w_out16, "w_out_lead": (e,),
                 "gq": gqa_qk_norm[e, 0][None, :], "gk": gqa_qk_norm[e, 1][None, :],
                 "sink": gqa_sink[e] * LOG2E}
            caches = (cache_gqa_k[:, e].reshape(LAT_B * PAST, KVH_C * HEAD),
                      cache_gqa_v[:, e].reshape(LAT_B * PAST, KVH_C * HEAD))
            x, h, (gk, gv) = _odd_mixer(x, h, mods, g[2:3], w, caches)
            gk_l.append(gk)
            gv_l.append(gv)
        act, w_out16 = _ffn_in(h, ffn_w_in, ffn_w_out, (layer, 1))
        if layer + 1 < depth:
            x, h = _ffn_out(act, w_out16, x, mods, 8, nxt=(layer_mods[layer + 1], norm_g[layer + 1, 0:1], 0))
        else:
            y_prompt = _ffn_out(act, w_out16, x, mods, 8, tile0=0, n_tiles=n_ctx_tiles).reshape(CTX_B, CTX_S, D)
            y_sample = _ffn_out(act, w_out16, x, mods, 8, tile0=n_ctx_tiles,
                                n_tiles=T // FFN_OUT_TM - n_ctx_tiles).reshape(LAT_B, LAT_S, D)

    return (y_prompt, y_sample, jnp.stack(ckv_l, axis=1), jnp.stack(kr_l, axis=1), jnp.stack(nak_l, axis=1),
            jnp.stack(nav_l, axis=1), jnp.stack(gk_l, axis=1), jnp.stack(gv_l, axis=1))
```

```python
import functools
import math

import numpy as np
import jax
import jax.numpy as jnp
from jax import lax
from jax.experimental import pallas as pl
from jax.experimental.pallas import tpu as pltpu

F32 = jnp.float32
BF16 = jnp.bfloat16

D = 2048
D_FF = 5632
N_MOD = 9
CTX_B, CTX_S = 16, 256
LAT_B, LAT_S = 8, 1024
PAST = 512
T_CTX = CTX_B * CTX_S
T_LAT = LAT_B * LAT_S
T = T_CTX + T_LAT
GRID_W = 64
GRID_H = LAT_S // GRID_W
N_GROUPS = 1 + LAT_B
HEAD = 128
HA = 8
NOPE, ROPE = 128, 64
QK = NOPE + ROPE
QK_PAD = 256
Q_LORA = 512
KV_LORA = 512
HB = 8
NA_KH, NA_KW = 8, 16
HC, KVH_C = 16, 4
GROUPS_C = HC // KVH_C
WINDOW = 128
IN_EVEN_PAD = 4224
IN_ODD = 3072
ROPE_THETA = 10000.0
EPS = 1e-6
NEG_INF = -1e30
LOG2E = math.log2(math.e)

VMEM_LIMIT = 56 * 1024 * 1024
Q_BLOCK = 256
M_CHUNK = 256
EMBED_CHUNK = 32
WO_COL_TILES = 4
FFN_OUT_TM = 512

_NT = (((1,), (1,)), ((), ()))


def _cparams(n_axes):
    return pltpu.CompilerParams(dimension_semantics=("arbitrary",) * n_axes,
                                vmem_limit_bytes=VMEM_LIMIT)


def _group_of_tile(i, tm):
    n_ctx = T_CTX // tm
    per = LAT_S // tm
    return jnp.where(i < n_ctx, 0, (i - n_ctx) // per + 1)


def _rope_block_of_tile(i, tm):
    n_ctx = T_CTX // tm
    per = LAT_S // tm
    return jnp.where(i < n_ctx, 0, per + (i - n_ctx) % per)


def _ada_kernel(c_ref, w_ref, b_ref, o_ref):
    c = c_ref[...]
    a = (c * jax.nn.sigmoid(c)).astype(BF16)
    o_ref[0] = jnp.dot(a, w_ref[0].astype(BF16), preferred_element_type=F32) + b_ref[0]


def _ada_modulation(cond, ada_w, ada_b, tn=2048):
    depth = ada_w.shape[0]
    n = N_MOD * D
    rows = cond.shape[0]
    return pl.pallas_call(
        _ada_kernel,
        grid=(depth, n // tn),
        in_specs=[
            pl.BlockSpec((rows, D), lambda l, j: (0, 0)),
            pl.BlockSpec((1, D, tn), lambda l, j: (l, 0, j)),
            pl.BlockSpec((1, 1, tn), lambda l, j: (l, 0, j)),
        ],
        out_specs=pl.BlockSpec((1, rows, tn), lambda l, j: (l, 0, j)),
        out_shape=jax.ShapeDtypeStruct((depth, rows, n), F32),
        compiler_params=_cparams(2),
        name="ada_modulation",
    )(cond, ada_w, ada_b.reshape(depth, 1, n))


def _mod_rows(mod_ref, g_ref, which):
    shift = mod_ref[0, 3 * which:3 * which + 1, :]
    scale = mod_ref[0, 3 * which + 1:3 * which + 2, :]
    return g_ref[...] * (1.0 + scale), shift


def _modulate(x, gain, shift):
    inv = lax.rsqrt(jnp.mean(x * x, axis=-1, keepdims=True) + EPS)
    return ((x * inv) * gain + shift).astype(BF16)


def _embed_kernel(xp_ref, xs_ref, mod_ref, g_ref, x_ref, h_ref, *, n_ctx_tiles, tm):
    gain, shift = _mod_rows(mod_ref, g_ref, 0)

    def run(src_ref):
        for r in range(0, tm, EMBED_CHUNK):
            x = src_ref[r:r + EMBED_CHUNK, :]
            x_ref[r:r + EMBED_CHUNK, :] = x
            h_ref[r:r + EMBED_CHUNK, :] = _modulate(x, gain, shift)

    @pl.when(pl.program_id(0) < n_ctx_tiles)
    def _():
        run(xp_ref)

    @pl.when(pl.program_id(0) >= n_ctx_tiles)
    def _():
        run(xs_ref)


def _embed(x_prompt, x_sample, mods, g, tm=512):
    n_ctx = T_CTX // tm
    return pl.pallas_call(
        functools.partial(_embed_kernel, n_ctx_tiles=n_ctx, tm=tm),
        grid=(T // tm,),
        in_specs=[
            pl.BlockSpec((tm, D), lambda i: (jnp.minimum(i, n_ctx - 1), 0)),
            pl.BlockSpec((tm, D), lambda i: (jnp.maximum(i - n_ctx, 0), 0)),
            pl.BlockSpec((1, N_MOD, D), lambda i: (_group_of_tile(i, tm), 0, 0)),
            pl.BlockSpec((1, D), lambda i: (0, 0)),
        ],
        out_specs=[pl.BlockSpec((tm, D), lambda i: (i, 0)), pl.BlockSpec((tm, D), lambda i: (i, 0))],
        out_shape=[jax.ShapeDtypeStruct((T, D), F32), jax.ShapeDtypeStruct((T, D), BF16)],
        compiler_params=_cparams(1),
        name="embed",
    )(x_prompt.reshape(T_CTX, D), x_sample.reshape(T_LAT, D), mods, g)


def _ffn_in_kernel(h_ref, wg_ref, wu_ref, wo_ref, o_ref, wo16_ref, *, tm):
    wo16_ref[...] = wo_ref[...].astype(BF16)

    wg = wg_ref[...].astype(BF16)
    wu = wu_ref[...].astype(BF16)
    for r in range(0, tm, M_CHUNK):
        h = h_ref[r:r + M_CHUNK, :]
        gate = jnp.dot(h, wg, preferred_element_type=F32)
        up = jnp.dot(h, wu, preferred_element_type=F32)
        o_ref[r:r + M_CHUNK, :] = (gate * jax.nn.sigmoid(gate) * up).astype(BF16)


def _ffn_in(h, w_in, w_out, lead, tm=2048, tn=512):
    nj = D_FF // tn
    n_lead = len(lead)
    wo_cols = D // WO_COL_TILES
    assert nj * tn == D_FF and T // tm >= WO_COL_TILES

    def wo_tile(i, j):
        parked = i >= WO_COL_TILES
        return jnp.where(parked, nj - 1, j), jnp.minimum(i, WO_COL_TILES - 1)

    return pl.pallas_call(
        functools.partial(_ffn_in_kernel, tm=tm),
        grid=(T // tm, nj),
        in_specs=[
            pl.BlockSpec((tm, D), lambda i, j: (i, 0)),
            pl.BlockSpec((None,) * n_lead + (D, tn), lambda i, j: lead + (0, j)),
            pl.BlockSpec((None,) * n_lead + (D, tn), lambda i, j: lead + (0, j + nj)),
            pl.BlockSpec((None,) * n_lead + (tn, wo_cols), lambda i, j: lead + wo_tile(i, j)),
        ],
        out_specs=[pl.BlockSpec((tm, tn), lambda i, j: (i, j)),
                   pl.BlockSpec((tn, wo_cols), wo_tile)],
        out_shape=[jax.ShapeDtypeStruct((T, D_FF), BF16), jax.ShapeDtypeStruct((D_FF, D), BF16)],
        compiler_params=_cparams(2),
        name="ffn_in",
    )(h, w_in, w_in, w_out)


def _ffn_out_kernel(*refs, gate_row, next_which, tm):
    if next_which is None:
        a_ref, w_ref, x_ref, mod_ref, o_ref = refs
    else:
        a_ref, w_ref, x_ref, mod_ref, modn_ref, gn_ref, o_ref, h_ref = refs
        gain, shift = _mod_rows(modn_ref, gn_ref, next_which)
    gate = 0.5 * mod_ref[0, gate_row:gate_row + 1, :]
    for r in range(0, tm, M_CHUNK):
        acc = jnp.dot(a_ref[r:r + M_CHUNK, :], w_ref[...], preferred_element_type=F32)
        x = x_ref[r:r + M_CHUNK, :] + gate * acc
        o_ref[r:r + M_CHUNK, :] = x
        if next_which is not None:
            h_ref[r:r + M_CHUNK, :] = _modulate(x, gain, shift)


def _ffn_out(a, w, x, mods, gate_row, nxt=None, tile0=0, n_tiles=T // FFN_OUT_TM, tm=FFN_OUT_TM):
    k = a.shape[1]
    rows = lambda n: pl.BlockSpec((tm, n), lambda i: (tile0 + i, 0))
    mod_spec = pl.BlockSpec((1, N_MOD, D), lambda i: (_group_of_tile(tile0 + i, tm), 0, 0))
    in_specs = [rows(k), pl.BlockSpec((k, D), lambda i: (0, 0), pipeline_mode=pl.Buffered(1)), rows(D), mod_spec]
    args = [a, w, x, mods]
    out_specs = [pl.BlockSpec((tm, D), lambda i: (i, 0))]
    out_shape = [jax.ShapeDtypeStruct((n_tiles * tm, D), F32)]
    if nxt is not None:
        in_specs += [mod_spec, pl.BlockSpec((1, D), lambda i: (0, 0))]
        args += [nxt[0], nxt[1]]
        out_specs.append(pl.BlockSpec((tm, D), lambda i: (i, 0)))
        out_shape.append(jax.ShapeDtypeStruct((n_tiles * tm, D), BF16))
    out = pl.pallas_call(
        functools.partial(_ffn_out_kernel, gate_row=gate_row, next_which=None if nxt is None else nxt[2], tm=tm),
        grid=(n_tiles,),
        in_specs=in_specs,
        out_specs=out_specs,
        out_shape=out_shape,
        compiler_params=_cparams(1),
        name="ffn_out",
    )(*args)
    return out if nxt is not None else out[0]


def _out_proj_kernel(*refs, widths, n_ctx_tiles, tm):
    n_a = len(widths)
    ctx_refs, lat_refs = refs[:n_a], refs[n_a:2 * n_a]
    w_ref, x_ref, mod_ref, g_ref, o_ref, h_ref = refs[2 * n_a:]
    gate = mod_ref[0, 5:6, :]
    gain, shift = _mod_rows(mod_ref, g_ref, 2)

    def run(a_refs):
        for r in range(0, tm, M_CHUNK):
            acc = None
            k0 = 0
            for a_ref, kw in zip(a_refs, widths):
                part = jnp.dot(a_ref[r:r + M_CHUNK, :], w_ref[k0:k0 + kw, :], preferred_element_type=F32)
                acc = part if acc is None else acc + part
                k0 += kw
            x = x_ref[r:r + M_CHUNK, :] + gate * acc
            o_ref[r:r + M_CHUNK, :] = x
            h_ref[r:r + M_CHUNK, :] = _modulate(x, gain, shift)

    @pl.when(pl.program_id(0) < n_ctx_tiles)
    def _():
        run(ctx_refs)

    @pl.when(pl.program_id(0) >= n_ctx_tiles)
    def _():
        run(lat_refs)


def _out_proj(a_ctx, a_lat, w, lead, x, mods, g_next, tm=512):
    widths = tuple(a.shape[1] for a in a_ctx)
    n_ctx = T_CTX // tm
    specs = [pl.BlockSpec((tm, kw), lambda i: (jnp.minimum(i, n_ctx - 1), 0)) for kw in widths]
    specs += [pl.BlockSpec((tm, kw), lambda i: (jnp.maximum(i - n_ctx, 0), 0)) for kw in widths]
    return pl.pallas_call(
        functools.partial(_out_proj_kernel, widths=widths, n_ctx_tiles=n_ctx, tm=tm),
        grid=(T // tm,),
        in_specs=specs + [
            pl.BlockSpec((None,) * len(lead) + (D, D), lambda i: lead + (0, 0), pipeline_mode=pl.Buffered(1)),
            pl.BlockSpec((tm, D), lambda i: (i, 0)),
            pl.BlockSpec((1, N_MOD, D), lambda i: (_group_of_tile(i, tm), 0, 0)),
            pl.BlockSpec((1, D), lambda i: (0, 0)),
        ],
        out_specs=[pl.BlockSpec((tm, D), lambda i: (i, 0)), pl.BlockSpec((tm, D), lambda i: (i, 0))],
        out_shape=[jax.ShapeDtypeStruct((T, D), F32), jax.ShapeDtypeStruct((T, D), BF16)],
        compiler_params=_cparams(1),
        name="mixer_out_proj",
    )(*a_ctx, *a_lat, w, x, mods, g_next)


def _rope(x, cos, sin, nf):
    lane = lax.broadcasted_iota(jnp.int32, x.shape, 1)
    first = (lane & (2 * nf - 1)) < nf
    partner = jnp.where(first, pltpu.roll(x, 128 - nf, 1), pltpu.roll(x, nf, 1))
    return x * cos + partner * sin


def _rms(x, g, n):
    inv = lax.rsqrt(jnp.sum(x * x, axis=-1, keepdims=True) * (1.0 / n) + EPS)
    return x * inv * g


def _mla_keys_values(ckvn, kr, wk_ref, wv_ref, gkn_ref, gkr_ref, cos, sin, km_ref, vm_ref):
    c16 = ckvn.astype(BF16)
    kn = jnp.dot(c16, wk_ref[...], preferred_element_type=F32)
    vm_ref[...] = jnp.dot(c16, wv_ref[...], preferred_element_type=F32).astype(BF16)
    kr_ss = jnp.sum(kr * kr, axis=-1, keepdims=True)
    for h in range(HA):
        x = kn[:, h * NOPE:(h + 1) * NOPE]
        inv = lax.rsqrt((jnp.sum(x * x, axis=-1, keepdims=True) + kr_ss) * (1.0 / QK) + EPS)
        km_ref[:, h * QK_PAD:h * QK_PAD + NOPE] = (x * inv * gkn_ref[...]).astype(BF16)
        r = kr * inv * gkr_ref[...]
        if cos is not None:
            r = _rope(r, cos, sin, ROPE // 4)
        km_ref[:, h * QK_PAD + NOPE:(h + 1) * QK_PAD] = r.astype(BF16)


def _even_in_kernel(h_ref, w_ref, wq_ref, wk_ref, wv_ref, qn_ref, kvn_ref, gq_ref, gkn_ref,
                    gkr_ref, gnaq_ref, gnak_ref, cos_ref, sin_ref,
                    qm_ref, km_ref, vm_ref, qna_ref, kna_ref, vna_ref,
                    ckv_ref, kr_ref, knaf_ref, vnaf_ref, *, n_ctx_tiles):
    is_ctx = pl.program_id(0) < n_ctx_tiles
    cos = cos_ref[...]
    sin = sin_ref[...]
    h = h_ref[...]
    proj = lambda lo, hi: jnp.dot(h, w_ref[:, lo:hi], preferred_element_type=F32)
    base = Q_LORA + KV_LORA

    low = proj(0, base)
    kr = proj(IN_EVEN_PAD - 128, IN_EVEN_PAD)
    cq = _rms(low[:, :Q_LORA], qn_ref[...], Q_LORA)
    q = jnp.dot(cq.astype(BF16), wq_ref[...], preferred_element_type=F32)
    for hd in range(HA):
        y = _rms(q[:, hd * QK_PAD:(hd + 1) * QK_PAD], gq_ref[...], QK)
        qm_ref[:, hd * QK_PAD:hd * QK_PAD + NOPE] = y[:, :NOPE].astype(BF16)
        qm_ref[:, hd * QK_PAD + NOPE:(hd + 1) * QK_PAD] = _rope(y[:, NOPE:], cos, sin, ROPE // 4).astype(BF16)
    ckvn = _rms(low[:, Q_LORA:], kvn_ref[...], KV_LORA)
    _mla_keys_values(ckvn, kr, wk_ref, wv_ref, gkn_ref, gkr_ref, cos, sin, km_ref, vm_ref)

    qn = proj(base, base + HB * HEAD)
    for hd in range(HB):
        qna_ref[:, hd * HEAD:(hd + 1) * HEAD] = _rms(qn[:, hd * HEAD:(hd + 1) * HEAD], gnaq_ref[...],
                                                     HEAD).astype(BF16)
    kn = proj(base + HB * HEAD, base + 2 * HB * HEAD)
    k_heads = []
    for hd in range(HB):
        kh = _rms(kn[:, hd * HEAD:(hd + 1) * HEAD], gnak_ref[...], HEAD)
        kna_ref[:, hd * HEAD:(hd + 1) * HEAD] = kh.astype(BF16)
        k_heads.append(kh)
    vn = proj(base + 2 * HB * HEAD, base + 3 * HB * HEAD)
    vna_ref[...] = vn.astype(BF16)

    @pl.when(is_ctx)
    def _():
        ckv_ref[...] = ckvn
        kr_ref[...] = kr[:, :ROPE]
        vnaf_ref[...] = vn
        for hd, kh in enumerate(k_heads):
            knaf_ref[:, hd * HEAD:(hd + 1) * HEAD] = kh


def _even_in(h, w, tm=256):
    n_ctx = T_CTX // tm
    resident = lambda a: pl.BlockSpec(a.shape, lambda i: (0, 0), pipeline_mode=pl.Buffered(1))
    small = lambda a: pl.BlockSpec(a.shape, lambda i: (0, 0))
    rows = lambda n: pl.BlockSpec((tm, n), lambda i: (i, 0))
    ctx_rows = lambda n: pl.BlockSpec((tm, n), lambda i: (jnp.minimum(i, n_ctx - 1), 0))
    table = pl.BlockSpec((tm, 128), lambda i: (_rope_block_of_tile(i, tm), 0))
    tok = lambda n, dt: jax.ShapeDtypeStruct((T, n), dt)
    ctx = lambda n: jax.ShapeDtypeStruct((T_CTX, n), F32)
    cos, sin = _rope_tables(ROPE)
    norms = [w[k] for k in ("q_norm", "kv_norm", "gq", "gkn", "gkr", "gnaq", "gnak")]
    return pl.pallas_call(
        functools.partial(_even_in_kernel, n_ctx_tiles=n_ctx),
        grid=(T // tm,),
        in_specs=[rows(D), resident(w["w_in"]), resident(w["wq"]), resident(w["wk"]), resident(w["wv"])]
                 + [small(a) for a in norms] + [table, table],
        out_specs=[rows(HA * QK_PAD), rows(HA * QK_PAD), rows(HA * HEAD), rows(HB * HEAD), rows(HB * HEAD),
                   rows(HB * HEAD), ctx_rows(KV_LORA), ctx_rows(ROPE), ctx_rows(HB * HEAD), ctx_rows(HB * HEAD)],
        out_shape=[tok(HA * QK_PAD, BF16), tok(HA * QK_PAD, BF16), tok(HA * HEAD, BF16), tok(HB * HEAD, BF16),
                   tok(HB * HEAD, BF16), tok(HB * HEAD, BF16), ctx(KV_LORA), ctx(ROPE), ctx(HB * HEAD),
                   ctx(HB * HEAD)],
        compiler_params=_cparams(1),
        name="even_in",
    )(h, w["w_in"], w["wq"], w["wk"], w["wv"], *norms, cos, sin)


def _cache_kv_kernel(ckv_ref, kr_ref, wk_ref, wv_ref, gkn_ref, gkr_ref, km_ref, vm_ref):
    _mla_keys_values(ckv_ref[...], kr_ref[...], wk_ref, wv_ref, gkn_ref, gkr_ref, None, None, km_ref, vm_ref)


def _cache_kv(ckv, kr, wk, wv, gkn, gkr, tm=512):
    n = ckv.shape[0]
    full = lambda shape: pl.BlockSpec(shape, lambda i: (0, 0))
    rows = lambda w: pl.BlockSpec((tm, w), lambda i: (i, 0))
    return pl.pallas_call(
        _cache_kv_kernel,
        grid=(n // tm,),
        in_specs=[rows(KV_LORA), rows(128), full(wk.shape), full(wv.shape), full(gkn.shape), full(gkr.shape)],
        out_specs=[rows(HA * QK_PAD), rows(HA * HEAD)],
        out_shape=[jax.ShapeDtypeStruct((n, HA * QK_PAD), BF16), jax.ShapeDtypeStruct((n, HA * HEAD), BF16)],
        compiler_params=_cparams(1),
        name="mla_cache_kv",
    )(ckv, kr, wk, wv, gkn, gkr)


def _odd_in_kernel(h_ref, w_ref, gq_ref, gk_ref, cos_ref, sin_ref,
                   q_ref, k_ref, v_ref, kf_ref, vf_ref, *, n_ctx_tiles, tm):
    is_ctx = pl.program_id(0) < n_ctx_tiles
    k0 = HC * HEAD
    v0 = k0 + KVH_C * HEAD
    cache_rows = []
    for r in range(0, tm, M_CHUNK):
        rows = slice(r, r + M_CHUNK)
        cos = cos_ref[rows, :]
        sin = sin_ref[rows, :]
        h = h_ref[rows, :]
        q = jnp.dot(h, w_ref[:, :k0], preferred_element_type=F32)
        for hd in range(HC):
            y = _rms(q[:, hd * HEAD:(hd + 1) * HEAD], gq_ref[...], HEAD)
            q_ref[rows, hd * HEAD:(hd + 1) * HEAD] = _rope(y, cos, sin, HEAD // 4).astype(BF16)
        k = jnp.dot(h, w_ref[:, k0:v0], preferred_element_type=F32)
        k_heads = []
        for hd in range(KVH_C):
            y = _rms(k[:, hd * HEAD:(hd + 1) * HEAD], gk_ref[...], HEAD)
            k_ref[rows, hd * HEAD:(hd + 1) * HEAD] = _rope(y, cos, sin, HEAD // 4).astype(BF16)
            k_heads.append(y)
        v = jnp.dot(h, w_ref[:, v0:], preferred_element_type=F32)
        v_ref[rows, :] = v.astype(BF16)
        cache_rows.append((rows, k_heads, v))

    @pl.when(is_ctx)
    def _():
        for rows, k_heads, v in cache_rows:
            vf_ref[rows, :] = v
            for hd, y in enumerate(k_heads):
                kf_ref[rows, hd * HEAD:(hd + 1) * HEAD] = y


def _odd_in(h, w, tm=512):
    n_ctx = T_CTX // tm
    small = lambda a: pl.BlockSpec(a.shape, lambda i: (0, 0))
    rows = lambda n: pl.BlockSpec((tm, n), lambda i: (i, 0))
    ctx_rows = lambda n: pl.BlockSpec((tm, n), lambda i: (jnp.minimum(i, n_ctx - 1), 0))
    table = pl.BlockSpec((tm, 128), lambda i: (_rope_block_of_tile(i, tm), 0))
    tok = lambda n, dt: jax.ShapeDtypeStruct((T, n), dt)
    ctx = lambda n: jax.ShapeDtypeStruct((T_CTX, n), F32)
    cos, sin = _rope_tables(HEAD)
    return pl.pallas_call(
        functools.partial(_odd_in_kernel, n_ctx_tiles=n_ctx, tm=tm),
        grid=(T // tm,),
        in_specs=[rows(D), pl.BlockSpec(w["w_in"].shape, lambda i: (0, 0), pipeline_mode=pl.Buffered(1)),
                  small(w["gq"]), small(w["gk"]), table, table],
        out_specs=[rows(HC * HEAD), rows(KVH_C * HEAD), rows(KVH_C * HEAD), ctx_rows(KVH_C * HEAD),
                   ctx_rows(KVH_C * HEAD)],
        out_shape=[tok(HC * HEAD, BF16), tok(KVH_C * HEAD, BF16), tok(KVH_C * HEAD, BF16), ctx(KVH_C * HEAD),
                   ctx(KVH_C * HEAD)],
        compiler_params=_cparams(1),
        name="odd_in",
    )(h, w["w_in"], w["gq"], w["gk"], cos, sin)


def _build_na_bias(toep_ref, bias_scr, hps):
    neg = jnp.full((GRID_W, GRID_W), NEG_INF, F32)
    qc = lax.broadcasted_iota(jnp.int32, (GRID_W, GRID_W), 0)
    kc = lax.broadcasted_iota(jnp.int32, (GRID_W, GRID_W), 1)
    lo = jnp.clip(qc - NA_KW // 2, 0, GRID_W - NA_KW)
    col_valid = (kc >= lo) & (kc < lo + NA_KW)
    for hh in range(hps):
        tiles = []
        for off in range(2 * NA_KH - 1):
            v = jnp.broadcast_to(toep_ref[hh, off:off + 1, :], (GRID_W, 128))
            t = pltpu.roll(v, 128 - (NA_KW - 1), 1, stride=1, stride_axis=0)[:, :GRID_W]
            tiles.append(jnp.where(col_valid, t, NEG_INF))
        for r in range(GRID_H):
            rs = min(max(r - NA_KH // 2, 0), GRID_H - NA_KH)
            for kr in range(GRID_H):
                tile = tiles[kr - r + NA_KH - 1] if rs <= kr < rs + NA_KH else neg
                bias_scr[hh, r * GRID_W:(r + 1) * GRID_W, kr * GRID_W:(kr + 1) * GRID_W] = tile


def _attn_kernel(*refs, plan, n_kv, bias_mode, has_sink, hps, groups, dk, dv, qb, c):
    it = iter(refs)
    q_ref = next(it)
    kv_refs = [(next(it), next(it)) for _ in range(n_kv)]
    bias_ref = next(it) if bias_mode is not None else None
    sink_ref = next(it) if has_sink else None
    o_ref = next(it)
    bias_scr = next(it) if bias_mode == "na" else None
    hblk = pl.program_id(0)

    if bias_mode == "na":
        @pl.when(pl.program_id(1) == 0)
        def _():
            _build_na_bias(bias_ref, bias_scr, hps)

    def one_head(hh, head, q_cols, o_cols):
        sink = sink_ref[hblk * hps * groups + head] if has_sink else None
        for qi, segs in enumerate(plan):
            q0 = qi * qb
            q = q_ref[q0:q0 + qb, q_cols]
            scores = []
            for (si, start, length, biased) in segs:
                k = kv_refs[si][0][start:start + length, hh * dk:(hh + 1) * dk].astype(BF16)
                t = lax.dot_general(q, k, _NT, preferred_element_type=F32) * c
                if biased and bias_mode == "na":
                    t = t + bias_scr[hh, q0:q0 + qb, start:start + length]
                elif biased:
                    t = t + bias_ref[qi, :, :length]
                scores.append(t)
            m = jnp.max(scores[0], axis=-1, keepdims=True)
            for t in scores[1:]:
                m = jnp.maximum(m, jnp.max(t, axis=-1, keepdims=True))
            if has_sink:
                m = jnp.maximum(m, sink)
            acc = None
            for t, (si, start, length, _) in zip(scores, segs):
                p = jnp.exp2(t - m).astype(BF16)
                v = kv_refs[si][1][start:start + length, hh * dv:(hh + 1) * dv].astype(BF16)
                v1 = jnp.concatenate([v, jnp.ones((length, dv), BF16)], axis=-1)
                pv = jnp.dot(p, v1, preferred_element_type=F32)
                acc = pv if acc is None else acc + pv
            denom = acc[:, dv:]
            if has_sink:
                denom = denom + jnp.exp2(sink - m)
            o_ref[q0:q0 + qb, o_cols] = (acc[:, :dv] / denom).astype(o_ref.dtype)

    for hh in range(hps):
        if groups > 1 and len(plan) > 1:
            def group_body(g, carry, hh=hh):
                head = hh * groups + g
                one_head(hh, head, pl.ds(pl.multiple_of(head * dk, dk), dk), pl.ds(pl.multiple_of(head * dv, dv), dv))
                return carry

            lax.fori_loop(0, groups, group_body, 0, unroll=2)
        else:
            for g in range(groups):
                head = hh * groups + g
                one_head(hh, head, slice(head * dk, (head + 1) * dk), slice(head * dv, (head + 1) * dv))


def _attention(q, kvs, plan, *, nb, sq, q_rowblk0, n_kv_heads, hps, groups, dk, dv, scale,
               bias=None, bias_mode=None, sink=None, name="attention"):
    grid = (n_kv_heads // hps, nb)
    in_specs = [pl.BlockSpec((sq, hps * groups * dk), lambda h, b: (q_rowblk0 + b, h))]
    args = [q]
    for (k, v, sk, r0) in kvs:
        in_specs.append(pl.BlockSpec((sk, hps * dk), lambda h, b, r0=r0: (r0 + b, h)))
        in_specs.append(pl.BlockSpec((sk, hps * dv), lambda h, b, r0=r0: (r0 + b, h)))
        args += [k, v]
    scratch = []
    if bias_mode == "na":
        in_specs.append(pl.BlockSpec((hps,) + bias.shape[1:], lambda h, b: (h, 0, 0)))
        args.append(bias)
        scratch.append(pltpu.VMEM((hps, LAT_S, LAT_S), F32))
    elif bias_mode == "table":
        in_specs.append(pl.BlockSpec(bias.shape, lambda h, b: (0, 0, 0)))
        args.append(bias)
    if sink is not None:
        in_specs.append(pl.BlockSpec(memory_space=pltpu.SMEM))
        args.append(sink)
    body = functools.partial(
        _attn_kernel, plan=plan, n_kv=len(kvs), bias_mode=bias_mode, has_sink=sink is not None,
        hps=hps, groups=groups, dk=dk, dv=dv, qb=min(Q_BLOCK, sq), c=scale * LOG2E)
    return pl.pallas_call(
        body,
        grid=grid,
        in_specs=in_specs,
        out_specs=pl.BlockSpec((sq, hps * groups * dv), lambda h, b: (b, h)),
        out_shape=jax.ShapeDtypeStruct((nb * sq, n_kv_heads * groups * dv), BF16),
        scratch_shapes=scratch,
        compiler_params=_cparams(2),
        name=name,
    )(*args)


CTX_PER_STEP = 4
_CTX_ROWS = CTX_PER_STEP * CTX_S
_CTX_PLAN = [[(0, i * CTX_S, CTX_S, False)] for i in range(CTX_PER_STEP)]
_N_QB = LAT_S // Q_BLOCK
_MLA_PLAN = [[(0, 0, LAT_S, False), (1, 0, PAST, False)] for _ in range(_N_QB)]
_NA_ROWS = [(0, 8), (0, 12), (4, 16), (8, 16)]
_NA_PLAN = [[(0, lo * GRID_W, (hi - lo) * GRID_W, True), (1, 0, PAST, False)] for lo, hi in _NA_ROWS]
_WIN_SPANS = [(max(i * Q_BLOCK - WINDOW, 0), min((i + 1) * Q_BLOCK + WINDOW, LAT_S)) for i in range(_N_QB)]
_WIN_PLAN = [[(0, lo, hi - lo, True), (1, 0, PAST, False)] for lo, hi in _WIN_SPANS]


def _window_bias():
    width = max(hi - lo for lo, hi in _WIN_SPANS)
    out = np.full((_N_QB, Q_BLOCK, width), NEG_INF, np.float32)
    for i, (lo, hi) in enumerate(_WIN_SPANS):
        q_abs = i * Q_BLOCK + np.arange(Q_BLOCK)[:, None]
        k_abs = lo + np.arange(hi - lo)[None, :]
        out[i, :, :hi - lo] = np.where(np.abs(q_abs - k_abs) <= WINDOW, 0.0, NEG_INF)
    return jnp.asarray(out)


def _rope_tables(rot_dim):
    half = rot_dim // 2
    nf = half // 2
    t = np.arange(LAT_S)
    inv_freq = ROPE_THETA ** (-np.arange(nf, dtype=np.float64) / nf)
    cos = np.zeros((2 * LAT_S, 128), np.float64)
    sin = np.zeros((2 * LAT_S, 128), np.float64)
    cos[:LAT_S, :rot_dim] = 1.0
    for part, pos in enumerate((t // GRID_W, t % GRID_W)):
        ang = pos[:, None].astype(np.float64) * inv_freq[None, :]
        lo = part * half
        cos[LAT_S:, lo:lo + nf] = np.cos(ang)
        cos[LAT_S:, lo + nf:lo + half] = np.cos(ang)
        sin[LAT_S:, lo:lo + nf] = -np.sin(ang)
        sin[LAT_S:, lo + nf:lo + half] = np.sin(ang)
    return jnp.asarray(cos, F32), jnp.asarray(sin, F32)


def _na_bias_rows(rpb):
    n_off, n_rel = rpb.shape[1], rpb.shape[2]
    return jnp.pad(rpb * LOG2E, ((0, 0), (0, 16 - n_off), (0, 128 - n_rel)))


def _even_mixer(x, h, mods, g_next, w, caches):
    cache_ckv, cache_krope, cache_nak, cache_nav = caches
    qm, km, vm, qna, kna, vna, new_ckv, new_kr, new_nak, new_nav = _even_in(h, w)
    km_ctx, vm_ctx = _cache_kv(cache_ckv, cache_krope, w["wk"], w["wv"], w["gkn"], w["gkr"])
    s_mla = 1.0 / math.sqrt(QK)
    s_na = 1.0 / math.sqrt(HEAD)
    lat0 = T_CTX // LAT_S
    n_ctx_steps = CTX_B // CTX_PER_STEP
    o_mla_ctx = _attention(qm, [(km, vm, _CTX_ROWS, 0)], _CTX_PLAN, nb=n_ctx_steps, sq=_CTX_ROWS, q_rowblk0=0,
                           n_kv_heads=HA, hps=HA, groups=1, dk=QK_PAD, dv=HEAD, scale=s_mla, name="mla_ctx")
    o_na_ctx = _attention(qna, [(kna, vna, _CTX_ROWS, 0)], _CTX_PLAN, nb=n_ctx_steps, sq=_CTX_ROWS, q_rowblk0=0,
                          n_kv_heads=HB, hps=HB, groups=1, dk=HEAD, dv=HEAD, scale=s_na, name="na_ctx")
    o_mla_lat = _attention(qm, [(km, vm, LAT_S, lat0), (km_ctx, vm_ctx, PAST, 0)], _MLA_PLAN, nb=LAT_B, sq=LAT_S,
                           q_rowblk0=lat0, n_kv_heads=HA, hps=8, groups=1, dk=QK_PAD, dv=HEAD, scale=s_mla,
                           name="mla_lat")
    o_na_lat = _attention(qna, [(kna, vna, LAT_S, lat0), (cache_nak, cache_nav, PAST, 0)], _NA_PLAN, nb=LAT_B,
                          sq=LAT_S, q_rowblk0=lat0, n_kv_heads=HB, hps=4, groups=1, dk=HEAD, dv=HEAD, scale=s_na,
                          bias=w["na_toep"], bias_mode="na", name="na_lat")
    x, h = _out_proj([o_mla_ctx, o_na_ctx], [o_mla_lat, o_na_lat], w["w_out"], w["w_out_lead"], x, mods, g_next)
    return x, h, (new_ckv.reshape(CTX_B, CTX_S, KV_LORA), new_kr.reshape(CTX_B, CTX_S, ROPE),
                  new_nak.reshape(CTX_B, CTX_S, HB, HEAD), new_nav.reshape(CTX_B, CTX_S, HB, HEAD))


def _odd_mixer(x, h, mods, g_next, w, caches):
    cache_k, cache_v = caches
    q, k, v, new_k, new_v = _odd_in(h, w)
    scale = 1.0 / math.sqrt(HEAD)
    lat0 = T_CTX // LAT_S
    o_ctx = _attention(q, [(k, v, CTX_S, 0)], _CTX_PLAN[:1], nb=CTX_B, sq=CTX_S, q_rowblk0=0, n_kv_heads=KVH_C,
                       hps=KVH_C, groups=GROUPS_C, dk=HEAD, dv=HEAD, scale=scale, sink=w["sink"], name="gqa_ctx")
    o_lat = _attention(q, [(k, v, LAT_S, lat0), (cache_k, cache_v, PAST, 0)], _WIN_PLAN, nb=LAT_B, sq=LAT_S,
                       q_rowblk0=lat0, n_kv_heads=KVH_C, hps=2, groups=GROUPS_C, dk=HEAD, dv=HEAD, scale=scale,
                       bias=_window_bias(), bias_mode="table", sink=w["sink"], name="gqa_lat")
    x, h = _out_proj([o_ctx], [o_lat], w["w_out"], w["w_out_lead"], x, mods, g_next)
    return x, h, (new_k.reshape(CTX_B, CTX_S, KVH_C, HEAD), new_v.reshape(CTX_B, CTX_S, KVH_C, HEAD))


def _even_weights(e, even_w_in, w_out_bf16, mla_q_norm, mla_w_q_up, mla_kv_norm, mla_w_kv_up, mla_qk_norm,
                  na_qk_norm, na_rpb):
    w_in = even_w_in[e]
    i0, i1, i2 = Q_LORA, Q_LORA + KV_LORA, Q_LORA + KV_LORA + ROPE
    w_in = jnp.concatenate([w_in[:, :i1], w_in[:, i2:], w_in[:, i1:i2], jnp.zeros((D, 128 - ROPE), F32)], axis=1)
    wq = jnp.pad(mla_w_q_up[e].reshape(Q_LORA, HA, QK), ((0, 0), (0, 0), (0, QK_PAD - QK)))
    wkv = mla_w_kv_up[e].reshape(KV_LORA, HA, NOPE + HEAD)
    qk = mla_qk_norm[e]
    return {
        "w_in": w_in.astype(BF16),
        "w_out": w_out_bf16,
        "w_out_lead": (e,),
        "wq": wq.reshape(Q_LORA, HA * QK_PAD).astype(BF16),
        "wk": wkv[:, :, :NOPE].reshape(KV_LORA, HA * NOPE).astype(BF16),
        "wv": wkv[:, :, NOPE:].reshape(KV_LORA, HA * HEAD).astype(BF16),
        "q_norm": mla_q_norm[e][None, :],
        "kv_norm": mla_kv_norm[e][None, :],
        "gq": jnp.pad(qk[0], (0, QK_PAD - QK))[None, :],
        "gkn": qk[1, :NOPE][None, :],
        "gkr": jnp.pad(qk[1, NOPE:], (0, 128 - ROPE))[None, :],
        "gnaq": na_qk_norm[e, 0][None, :],
        "gnak": na_qk_norm[e, 1][None, :],
        "na_toep": _na_bias_rows(na_rpb[e]),
    }


def kernel(x_prompt, x_sample, cache_mla_ckv, cache_mla_krope, cache_na_k, cache_na_v, cache_gqa_k, cache_gqa_v, c, c_ctx, ada_w, ada_b, norm_g, ffn_w_in, ffn_w_out, even_w_in, even_w_out, mla_q_norm, mla_w_q_up, mla_kv_norm, mla_w_kv_up, mla_qk_norm, na_qk_norm, na_rpb, odd_w_in, odd_w_out, gqa_qk_norm, gqa_sink):
    depth = ada_w.shape[0]
    cond = jnp.concatenate([c_ctx[None, :], c, jnp.zeros((16 - N_GROUPS, D), F32)], axis=0)
    mods_all = _ada_modulation(cond, ada_w, ada_b)
    layer_mods = [mods_all[layer, :N_GROUPS].reshape(N_GROUPS, N_MOD, D) for layer in range(depth)]
    even_w_out16 = even_w_out.astype(BF16)
    odd_w_out16 = odd_w_out.astype(BF16)
    n_ctx_tiles = T_CTX // FFN_OUT_TM

    ckv_l, kr_l, nak_l, nav_l, gk_l, gv_l = [], [], [], [], [], []
    y_prompt = y_sample = None
    x, h = _embed(x_prompt, x_sample, layer_mods[0], norm_g[0, 0:1])
    for layer in range(depth):
        mods = layer_mods[layer]
        g = norm_g[layer]
        e = layer // 2
        act, w_out16 = _ffn_in(h, ffn_w_in, ffn_w_out, (layer, 0))
        x, h = _ffn_out(act, w_out16, x, mods, 2, nxt=(mods, g[1:2], 1))
        if layer % 2 == 0:
            w = _even_weights(e, even_w_in, even_w_out16, mla_q_norm, mla_w_q_up, mla_kv_norm, mla_w_kv_up,
                              mla_qk_norm, na_qk_norm, na_rpb)
            caches = (cache_mla_ckv[:, e].reshape(LAT_B * PAST, KV_LORA),
                      jnp.pad(cache_mla_krope[:, e].reshape(LAT_B * PAST, ROPE), ((0, 0), (0, 128 - ROPE))),
                      cache_na_k[:, e].reshape(LAT_B * PAST, HB * HEAD),
                      cache_na_v[:, e].reshape(LAT_B * PAST, HB * HEAD))
            x, h, (ckv, kr, nak, nav) = _even_mixer(x, h, mods, g[2:3], w, caches)
            ckv_l.append(ckv)
            kr_l.append(kr)
            nak_l.append(nak)
            nav_l.append(nav)
        else:
            w = {"w_in": odd_w_in[e].astype(BF16), "w_out": odd_w_out16, "w_out_lead": (e,),
                 "gq": gqa_qk_norm[e, 0][None, :], "gk": gqa_qk_norm[e, 1][None, :],
                 "sink": gqa_sink[e] * LOG2E}
            caches = (cache_gqa_k[:, e].reshape(LAT_B * PAST, KVH_C * HEAD),
                      cache_gqa_v[:, e].reshape(LAT_B * PAST, KVH_C * HEAD))
            x, h, (gk, gv) = _odd_mixer(x, h, mods, g[2:3], w, caches)
            gk_l.append(gk)
            gv_l.append(gv)
        act, w_out16 = _ffn_in(h, ffn_w_in, ffn_w_out, (layer, 1))
        if layer + 1 < depth:
            x, h = _ffn_out(act, w_out16, x, mods, 8, nxt=(layer_mods[layer + 1], norm_g[layer + 1, 0:1], 0))
        else:
            y_prompt = _ffn_out(act, w_out16, x, mods, 8, tile0=0, n_tiles=n_ctx_tiles).reshape(CTX_B, CTX_S, D)
            y_sample = _ffn_out(act, w_out16, x, mods, 8, tile0=n_ctx_tiles,
                                n_tiles=T // FFN_OUT_TM - n_ctx_tiles).reshape(LAT_B, LAT_S, D)

    return (y_prompt, y_sample, jnp.stack(ckv_l, axis=1), jnp.stack(kr_l, axis=1), jnp.stack(nak_l, axis=1),
            jnp.stack(nav_l, axis=1), jnp.stack(gk_l, axis=1), jnp.stack(gv_l, axis=1))
```

```python
import functools
import math

import numpy as np
import jax
import jax.numpy as jnp
from jax import lax
from jax.experimental import pallas as pl
from jax.experimental.pallas import tpu as pltpu

F32 = jnp.float32
BF16 = jnp.bfloat16

D = 2048
D_FF = 5632
N_MOD = 9
CTX_B, CTX_S = 16, 256
LAT_B, LAT_S = 8, 1024
PAST = 512
T_CTX = CTX_B * CTX_S
T_LAT = LAT_B * LAT_S
T = T_CTX + T_LAT
GRID_W = 64
GRID_H = LAT_S // GRID_W
N_GROUPS = 1 + LAT_B
HEAD = 128
HA = 8
NOPE, ROPE = 128, 64
QK = NOPE + ROPE
QK_PAD = 256
Q_LORA = 512
KV_LORA = 512
HB = 8
NA_KH, NA_KW = 8, 16
HC, KVH_C = 16, 4
GROUPS_C = HC // KVH_C
WINDOW = 128
IN_EVEN_PAD = 4224
IN_ODD = 3072
ROPE_THETA = 10000.0
EPS = 1e-6
NEG_INF = -1e30
LOG2E = math.log2(math.e)

VMEM_LIMIT = 56 * 1024 * 1024
Q_BLOCK = 256
M_CHUNK = 256
EMBED_CHUNK = 32
WO_COL_TILES = 4
FFN_OUT_TM = 512

_NT = (((1,), (1,)), ((), ()))


def _cparams(n_axes):
    return pltpu.CompilerParams(dimension_semantics=("arbitrary",) * n_axes,
                                vmem_limit_bytes=VMEM_LIMIT)


def _group_of_tile(i, tm):
    n_ctx = T_CTX // tm
    per = LAT_S // tm
    return jnp.where(i < n_ctx, 0, (i - n_ctx) // per + 1)


def _rope_block_of_tile(i, tm):
    n_ctx = T_CTX // tm
    per = LAT_S // tm
    return jnp.where(i < n_ctx, 0, per + (i - n_ctx) % per)


ADA_SLOTS = 3


def _ada_kernel(c_ref, w_hbm, b_ref, o_ref, buf, sem, *, nj, tn, n_steps):
    s = pl.program_id(0)

    def tile_copy(step, slot):
        cols = pl.ds(pl.multiple_of((step % nj) * tn, tn), tn)
        return pltpu.make_async_copy(w_hbm.at[step // nj, :, cols], buf.at[slot], sem.at[slot])

    @pl.when(s == 0)
    def _():
        for k in range(ADA_SLOTS - 1):
            tile_copy(k, k).start()

    @pl.when(s + ADA_SLOTS - 1 < n_steps)
    def _():
        tile_copy(s + ADA_SLOTS - 1, (s + ADA_SLOTS - 1) % ADA_SLOTS).start()

    slot = s % ADA_SLOTS
    tile_copy(s, slot).wait()
    c = c_ref[...]
    a = (c * jax.nn.sigmoid(c)).astype(BF16)
    o_ref[0] = jnp.dot(a, buf[slot].astype(BF16), preferred_element_type=F32) + b_ref[0]


def _ada_modulation(cond, ada_w, ada_b, tn=1024):
    depth = ada_w.shape[0]
    n = N_MOD * D
    rows = cond.shape[0]
    nj = n // tn
    n_steps = depth * nj
    assert n_steps >= ADA_SLOTS
    return pl.pallas_call(
        functools.partial(_ada_kernel, nj=nj, tn=tn, n_steps=n_steps),
        grid=(n_steps,),
        in_specs=[
            pl.BlockSpec((rows, D), lambda s: (0, 0)),
            pl.BlockSpec(memory_space=pl.ANY),
            pl.BlockSpec((1, 1, tn), lambda s: (s // nj, 0, s % nj)),
        ],
        out_specs=pl.BlockSpec((1, rows, tn), lambda s: (s // nj, 0, s % nj)),
        out_shape=jax.ShapeDtypeStruct((depth, rows, n), F32),
        scratch_shapes=[pltpu.VMEM((ADA_SLOTS, D, tn), F32), pltpu.SemaphoreType.DMA((ADA_SLOTS,))],
        compiler_params=_cparams(1),
        name="ada_modulation",
    )(cond, ada_w, ada_b.reshape(depth, 1, n))


def _mod_rows(mod_ref, g_ref, which):
    shift = mod_ref[0, 3 * which:3 * which + 1, :]
    scale = mod_ref[0, 3 * which + 1:3 * which + 2, :]
    return g_ref[...] * (1.0 + scale), shift


def _modulate(x, gain, shift):
    inv = lax.rsqrt(jnp.mean(x * x, axis=-1, keepdims=True) + EPS)
    return ((x * inv) * gain + shift).astype(BF16)


def _embed_kernel(xp_ref, xs_ref, mod_ref, g_ref, x_ref, h_ref, *, n_ctx_tiles, tm):
    gain, shift = _mod_rows(mod_ref, g_ref, 0)

    def run(src_ref):
        for r in range(0, tm, EMBED_CHUNK):
            x = src_ref[r:r + EMBED_CHUNK, :]
            x_ref[r:r + EMBED_CHUNK, :] = x
            h_ref[r:r + EMBED_CHUNK, :] = _modulate(x, gain, shift)

    @pl.when(pl.program_id(0) < n_ctx_tiles)
    def _():
        run(xp_ref)

    @pl.when(pl.program_id(0) >= n_ctx_tiles)
    def _():
        run(xs_ref)


def _embed(x_prompt, x_sample, mods, g, tm=512):
    n_ctx = T_CTX // tm
    return pl.pallas_call(
        functools.partial(_embed_kernel, n_ctx_tiles=n_ctx, tm=tm),
        grid=(T // tm,),
        in_specs=[
            pl.BlockSpec((tm, D), lambda i: (jnp.minimum(i, n_ctx - 1), 0)),
            pl.BlockSpec((tm, D), lambda i: (jnp.maximum(i - n_ctx, 0), 0)),
            pl.BlockSpec((1, N_MOD, D), lambda i: (_group_of_tile(i, tm), 0, 0)),
            pl.BlockSpec((1, D), lambda i: (0, 0)),
        ],
        out_specs=[pl.BlockSpec((tm, D), lambda i: (i, 0)), pl.BlockSpec((tm, D), lambda i: (i, 0))],
        out_shape=[jax.ShapeDtypeStruct((T, D), F32), jax.ShapeDtypeStruct((T, D), BF16)],
        compiler_params=_cparams(1),
        name="embed",
    )(x_prompt.reshape(T_CTX, D), x_sample.reshape(T_LAT, D), mods, g)


def _ffn_in_kernel(h_ref, wg_ref, wu_ref, wo_ref, o_ref, wo16_ref, *, tm):
    wo16_ref[...] = wo_ref[...].astype(BF16)

    wg = wg_ref[...].astype(BF16)
    wu = wu_ref[...].astype(BF16)
    for r in range(0, tm, M_CHUNK):
        h = h_ref[r:r + M_CHUNK, :]
        gate = jnp.dot(h, wg, preferred_element_type=F32)
        up = jnp.dot(h, wu, preferred_element_type=F32)
        o_ref[r:r + M_CHUNK, :] = (gate * jax.nn.sigmoid(gate) * up).astype(BF16)


def _ffn_in(h, w_in, w_out, lead, tm=2048, tn=512):
    nj = D_FF // tn
    n_lead = len(lead)
    wo_cols = D // WO_COL_TILES
    assert nj * tn == D_FF and T // tm >= WO_COL_TILES

    def wo_tile(i, j):
        parked = i >= WO_COL_TILES
        return jnp.where(parked, nj - 1, j), jnp.minimum(i, WO_COL_TILES - 1)

    return pl.pallas_call(
        functools.partial(_ffn_in_kernel, tm=tm),
        grid=(T // tm, nj),
        in_specs=[
            pl.BlockSpec((tm, D), lambda i, j: (i, 0)),
            pl.BlockSpec((None,) * n_lead + (D, tn), lambda i, j: lead + (0, j)),
            pl.BlockSpec((None,) * n_lead + (D, tn), lambda i, j: lead + (0, j + nj)),
            pl.BlockSpec((None,) * n_lead + (tn, wo_cols), lambda i, j: lead + wo_tile(i, j)),
        ],
        out_specs=[pl.BlockSpec((tm, tn), lambda i, j: (i, j)),
                   pl.BlockSpec((tn, wo_cols), wo_tile)],
        out_shape=[jax.ShapeDtypeStruct((T, D_FF), BF16), jax.ShapeDtypeStruct((D_FF, D), BF16)],
        compiler_params=_cparams(2),
        name="ffn_in",
    )(h, w_in, w_in, w_out)


def _ffn_out_kernel(*refs, gate_row, next_which, tm):
    if next_which is None:
        a_ref, w_ref, x_ref, mod_ref, o_ref = refs
    else:
        a_ref, w_ref, x_ref, mod_ref, modn_ref, gn_ref, o_ref, h_ref = refs
        gain, shift = _mod_rows(modn_ref, gn_ref, next_which)
    gate = 0.5 * mod_ref[0, gate_row:gate_row + 1, :]
    for r in range(0, tm, M_CHUNK):
        acc = jnp.dot(a_ref[r:r + M_CHUNK, :], w_ref[...], preferred_element_type=F32)
        x = x_ref[r:r + M_CHUNK, :] + gate * acc
        o_ref[r:r + M_CHUNK, :] = x
        if next_which is not None:
            h_ref[r:r + M_CHUNK, :] = _modulate(x, gain, shift)


def _ffn_out(a, w, x, mods, gate_row, nxt=None, tile0=0, n_tiles=T // FFN_OUT_TM, tm=FFN_OUT_TM):
    k = a.shape[1]
    rows = lambda n: pl.BlockSpec((tm, n), lambda i: (tile0 + i, 0))
    mod_spec = pl.BlockSpec((1, N_MOD, D), lambda i: (_group_of_tile(tile0 + i, tm), 0, 0))
    in_specs = [rows(k), pl.BlockSpec((k, D), lambda i: (0, 0), pipeline_mode=pl.Buffered(1)), rows(D), mod_spec]
    args = [a, w, x, mods]
    out_specs = [pl.BlockSpec((tm, D), lambda i: (i, 0))]
    out_shape = [jax.ShapeDtypeStruct((n_tiles * tm, D), F32)]
    if nxt is not None:
        in_specs += [mod_spec, pl.BlockSpec((1, D), lambda i: (0, 0))]
        args += [nxt[0], nxt[1]]
        out_specs.append(pl.BlockSpec((tm, D), lambda i: (i, 0)))
        out_shape.append(jax.ShapeDtypeStruct((n_tiles * tm, D), BF16))
    out = pl.pallas_call(
        functools.partial(_ffn_out_kernel, gate_row=gate_row, next_which=None if nxt is None else nxt[2], tm=tm),
        grid=(n_tiles,),
        in_specs=in_specs,
        out_specs=out_specs,
        out_shape=out_shape,
        compiler_params=_cparams(1),
        name="ffn_out",
    )(*args)
    return out if nxt is not None else out[0]


def _out_proj_kernel(*refs, widths, n_ctx_tiles, tm):
    n_a = len(widths)
    ctx_refs, lat_refs = refs[:n_a], refs[n_a:2 * n_a]
    w_ref, x_ref, mod_ref, g_ref, o_ref, h_ref = refs[2 * n_a:]
    gate = mod_ref[0, 5:6, :]
    gain, shift = _mod_rows(mod_ref, g_ref, 2)

    def run(a_refs):
        for r in range(0, tm, M_CHUNK):
            acc = None
            k0 = 0
            for a_ref, kw in zip(a_refs, widths):
                part = jnp.dot(a_ref[r:r + M_CHUNK, :], w_ref[k0:k0 + kw, :], preferred_element_type=F32)
                acc = part if acc is None else acc + part
                k0 += kw
            x = x_ref[r:r + M_CHUNK, :] + gate * acc
            o_ref[r:r + M_CHUNK, :] = x
            h_ref[r:r + M_CHUNK, :] = _modulate(x, gain, shift)

    @pl.when(pl.program_id(0) < n_ctx_tiles)
    def _():
        run(ctx_refs)

    @pl.when(pl.program_id(0) >= n_ctx_tiles)
    def _():
        run(lat_refs)


def _out_proj(a_ctx, a_lat, w, lead, x, mods, g_next, tm=512):
    widths = tuple(a.shape[1] for a in a_ctx)
    n_ctx = T_CTX // tm
    specs = [pl.BlockSpec((tm, kw), lambda i: (jnp.minimum(i, n_ctx - 1), 0)) for kw in widths]
    specs += [pl.BlockSpec((tm, kw), lambda i: (jnp.maximum(i - n_ctx, 0), 0)) for kw in widths]
    return pl.pallas_call(
        functools.partial(_out_proj_kernel, widths=widths, n_ctx_tiles=n_ctx, tm=tm),
        grid=(T // tm,),
        in_specs=specs + [
            pl.BlockSpec((None,) * len(lead) + (D, D), lambda i: lead + (0, 0), pipeline_mode=pl.Buffered(1)),
            pl.BlockSpec((tm, D), lambda i: (i, 0)),
            pl.BlockSpec((1, N_MOD, D), lambda i: (_group_of_tile(i, tm), 0, 0)),
            pl.BlockSpec((1, D), lambda i: (0, 0)),
        ],
        out_specs=[pl.BlockSpec((tm, D), lambda i: (i, 0)), pl.BlockSpec((tm, D), lambda i: (i, 0))],
        out_shape=[jax.ShapeDtypeStruct((T, D), F32), jax.ShapeDtypeStruct((T, D), BF16)],
        compiler_params=_cparams(1),
        name="mixer_out_proj",
    )(*a_ctx, *a_lat, w, x, mods, g_next)


def _rope(x, cos, sin, nf):
    lane = lax.broadcasted_iota(jnp.int32, x.shape, 1)
    first = (lane & (2 * nf - 1)) < nf
    partner = jnp.where(first, pltpu.roll(x, 128 - nf, 1), pltpu.roll(x, nf, 1))
    return x * cos + partner * sin


def _rms(x, g, n):
    inv = lax.rsqrt(jnp.sum(x * x, axis=-1, keepdims=True) * (1.0 / n) + EPS)
    return x * inv * g


def _mla_keys_values(ckvn, kr, wk_ref, wv_ref, gkn_ref, gkr_ref, cos, sin, km_ref, vm_ref):
    c16 = ckvn.astype(BF16)
    kn = jnp.dot(c16, wk_ref[...], preferred_element_type=F32)
    vm_ref[...] = jnp.dot(c16, wv_ref[...], preferred_element_type=F32).astype(BF16)
    kr_ss = jnp.sum(kr * kr, axis=-1, keepdims=True)
    for h in range(HA):
        x = kn[:, h * NOPE:(h + 1) * NOPE]
        inv = lax.rsqrt((jnp.sum(x * x, axis=-1, keepdims=True) + kr_ss) * (1.0 / QK) + EPS)
        km_ref[:, h * QK_PAD:h * QK_PAD + NOPE] = (x * inv * gkn_ref[...]).astype(BF16)
        r = kr * inv * gkr_ref[...]
        if cos is not None:
            r = _rope(r, cos, sin, ROPE // 4)
        km_ref[:, h * QK_PAD + NOPE:(h + 1) * QK_PAD] = r.astype(BF16)


def _even_in_kernel(h_ref, w_ref, wq_ref, wk_ref, wv_ref, qn_ref, kvn_ref, gq_ref, gkn_ref,
                    gkr_ref, gnaq_ref, gnak_ref, cos_ref, sin_ref,
                    qm_ref, km_ref, vm_ref, qna_ref, kna_ref, vna_ref,
                    ckv_ref, kr_ref, knaf_ref, vnaf_ref, *, n_ctx_tiles):
    is_ctx = pl.program_id(0) < n_ctx_tiles
    cos = cos_ref[...]
    sin = sin_ref[...]
    h = h_ref[...]
    proj = lambda lo, hi: jnp.dot(h, w_ref[:, lo:hi], preferred_element_type=F32)
    base = Q_LORA + KV_LORA

    low = proj(0, base)
    kr = proj(IN_EVEN_PAD - 128, IN_EVEN_PAD)
    cq = _rms(low[:, :Q_LORA], qn_ref[...], Q_LORA)
    q = jnp.dot(cq.astype(BF16), wq_ref[...], preferred_element_type=F32)
    for hd in range(HA):
        y = _rms(q[:, hd * QK_PAD:(hd + 1) * QK_PAD], gq_ref[...], QK)
        qm_ref[:, hd * QK_PAD:hd * QK_PAD + NOPE] = y[:, :NOPE].astype(BF16)
        qm_ref[:, hd * QK_PAD + NOPE:(hd + 1) * QK_PAD] = _rope(y[:, NOPE:], cos, sin, ROPE // 4).astype(BF16)
    ckvn = _rms(low[:, Q_LORA:], kvn_ref[...], KV_LORA)
    _mla_keys_values(ckvn, kr, wk_ref, wv_ref, gkn_ref, gkr_ref, cos, sin, km_ref, vm_ref)

    qn = proj(base, base + HB * HEAD)
    for hd in range(HB):
        qna_ref[:, hd * HEAD:(hd + 1) * HEAD] = _rms(qn[:, hd * HEAD:(hd + 1) * HEAD], gnaq_ref[...],
                                                     HEAD).astype(BF16)
    kn = proj(base + HB * HEAD, base + 2 * HB * HEAD)
    k_heads = []
    for hd in range(HB):
        kh = _rms(kn[:, hd * HEAD:(hd + 1) * HEAD], gnak_ref[...], HEAD)
        kna_ref[:, hd * HEAD:(hd + 1) * HEAD] = kh.astype(BF16)
        k_heads.append(kh)
    vn = proj(base + 2 * HB * HEAD, base + 3 * HB * HEAD)
    vna_ref[...] = vn.astype(BF16)

    @pl.when(is_ctx)
    def _():
        ckv_ref[...] = ckvn
        kr_ref[...] = kr[:, :ROPE]
        vnaf_ref[...] = vn
        for hd, kh in enumerate(k_heads):
            knaf_ref[:, hd * HEAD:(hd + 1) * HEAD] = kh


def _even_in(h, w, tm=256):
    n_ctx = T_CTX // tm
    resident = lambda a: pl.BlockSpec(a.shape, lambda i: (0, 0), pipeline_mode=pl.Buffered(1))
    small = lambda a: pl.BlockSpec(a.shape, lambda i: (0, 0))
    rows = lambda n: pl.BlockSpec((tm, n), lambda i: (i, 0))
    ctx_rows = lambda n: pl.BlockSpec((tm, n), lambda i: (jnp.minimum(i, n_ctx - 1), 0))
    table = pl.BlockSpec((tm, 128), lambda i: (_rope_block_of_tile(i, tm), 0))
    tok = lambda n, dt: jax.ShapeDtypeStruct((T, n), dt)
    ctx = lambda n: jax.ShapeDtypeStruct((T_CTX, n), F32)
    cos, sin = _rope_tables(ROPE)
    norms = [w[k] for k in ("q_norm", "kv_norm", "gq", "gkn", "gkr", "gnaq", "gnak")]
    return pl.pallas_call(
        functools.partial(_even_in_kernel, n_ctx_tiles=n_ctx),
        grid=(T // tm,),
        in_specs=[rows(D), resident(w["w_in"]), resident(w["wq"]), resident(w["wk"]), resident(w["wv"])]
                 + [small(a) for a in norms] + [table, table],
        out_specs=[rows(HA * QK_PAD), rows(HA * QK_PAD), rows(HA * HEAD), rows(HB * HEAD), rows(HB * HEAD),
                   rows(HB * HEAD), ctx_rows(KV_LORA), ctx_rows(ROPE), ctx_rows(HB * HEAD), ctx_rows(HB * HEAD)],
        out_shape=[tok(HA * QK_PAD, BF16), tok(HA * QK_PAD, BF16), tok(HA * HEAD, BF16), tok(HB * HEAD, BF16),
                   tok(HB * HEAD, BF16), tok(HB * HEAD, BF16), ctx(KV_LORA), ctx(ROPE), ctx(HB * HEAD),
                   ctx(HB * HEAD)],
        compiler_params=_cparams(1),
        name="even_in",
    )(h, w["w_in"], w["wq"], w["wk"], w["wv"], *norms, cos, sin)


def _cache_kv_kernel(ckv_ref, kr_ref, wk_ref, wv_ref, gkn_ref, gkr_ref, km_ref, vm_ref):
    _mla_keys_values(ckv_ref[...], kr_ref[...], wk_ref, wv_ref, gkn_ref, gkr_ref, None, None, km_ref, vm_ref)


def _cache_kv(ckv, kr, wk, wv, gkn, gkr, tm=512):
    n = ckv.shape[0]
    full = lambda shape: pl.BlockSpec(shape, lambda i: (0, 0))
    rows = lambda w: pl.BlockSpec((tm, w), lambda i: (i, 0))
    return pl.pallas_call(
        _cache_kv_kernel,
        grid=(n // tm,),
        in_specs=[rows(KV_LORA), rows(128), full(wk.shape), full(wv.shape), full(gkn.shape), full(gkr.shape)],
        out_specs=[rows(HA * QK_PAD), rows(HA * HEAD)],
        out_shape=[jax.ShapeDtypeStruct((n, HA * QK_PAD), BF16), jax.ShapeDtypeStruct((n, HA * HEAD), BF16)],
        compiler_params=_cparams(1),
        name="mla_cache_kv",
    )(ckv, kr, wk, wv, gkn, gkr)


def _odd_in_kernel(h_ref, w_ref, gq_ref, gk_ref, cos_ref, sin_ref,
                   q_ref, k_ref, v_ref, kf_ref, vf_ref, *, n_ctx_tiles, tm):
    is_ctx = pl.program_id(0) < n_ctx_tiles
    k0 = HC * HEAD
    v0 = k0 + KVH_C * HEAD
    cache_rows = []
    for r in range(0, tm, M_CHUNK):
        rows = slice(r, r + M_CHUNK)
        cos = cos_ref[rows, :]
        sin = sin_ref[rows, :]
        h = h_ref[rows, :]
        q = jnp.dot(h, w_ref[:, :k0], preferred_element_type=F32)
        for hd in range(HC):
            y = _rms(q[:, hd * HEAD:(hd + 1) * HEAD], gq_ref[...], HEAD)
            q_ref[rows, hd * HEAD:(hd + 1) * HEAD] = _rope(y, cos, sin, HEAD // 4).astype(BF16)
        k = jnp.dot(h, w_ref[:, k0:v0], preferred_element_type=F32)
        k_heads = []
        for hd in range(KVH_C):
            y = _rms(k[:, hd * HEAD:(hd + 1) * HEAD], gk_ref[...], HEAD)
            k_ref[rows, hd * HEAD:(hd + 1) * HEAD] = _rope(y, cos, sin, HEAD // 4).astype(BF16)
            k_heads.append(y)
        v = jnp.dot(h, w_ref[:, v0:], preferred_element_type=F32)
        v_ref[rows, :] = v.astype(BF16)
        cache_rows.append((rows, k_heads, v))

    @pl.when(is_ctx)
    def _():
        for rows, k_heads, v in cache_rows:
            vf_ref[rows, :] = v
            for hd, y in enumerate(k_heads):
                kf_ref[rows, hd * HEAD:(hd + 1) * HEAD] = y


def _odd_in(h, w, tm=512):
    n_ctx = T_CTX // tm
    small = lambda a: pl.BlockSpec(a.shape, lambda i: (0, 0))
    rows = lambda n: pl.BlockSpec((tm, n), lambda i: (i, 0))
    ctx_rows = lambda n: pl.BlockSpec((tm, n), lambda i: (jnp.minimum(i, n_ctx - 1), 0))
    table = pl.BlockSpec((tm, 128), lambda i: (_rope_block_of_tile(i, tm), 0))
    tok = lambda n, dt: jax.ShapeDtypeStruct((T, n), dt)
    ctx = lambda n: jax.ShapeDtypeStruct((T_CTX, n), F32)
    cos, sin = _rope_tables(HEAD)
    return pl.pallas_call(
        functools.partial(_odd_in_kernel, n_ctx_tiles=n_ctx, tm=tm),
        grid=(T // tm,),
        in_specs=[rows(D), pl.BlockSpec(w["w_in"].shape, lambda i: (0, 0), pipeline_mode=pl.Buffered(1)),
                  small(w["gq"]), small(w["gk"]), table, table],
        out_specs=[rows(HC * HEAD), rows(KVH_C * HEAD), rows(KVH_C * HEAD), ctx_rows(KVH_C * HEAD),
                   ctx_rows(KVH_C * HEAD)],
        out_shape=[tok(HC * HEAD, BF16), tok(KVH_C * HEAD, BF16), tok(KVH_C * HEAD, BF16), ctx(KVH_C * HEAD),
                   ctx(KVH_C * HEAD)],
        compiler_params=_cparams(1),
        name="odd_in",
    )(h, w["w_in"], w["gq"], w["gk"], cos, sin)


def _build_na_bias(toep_ref, bias_scr, hps):
    neg = jnp.full((GRID_W, GRID_W), NEG_INF, F32)
    qc = lax.broadcasted_iota(jnp.int32, (GRID_W, GRID_W), 0)
    kc = lax.broadcasted_iota(jnp.int32, (GRID_W, GRID_W), 1)
    lo = jnp.clip(qc - NA_KW // 2, 0, GRID_W - NA_KW)
    col_valid = (kc >= lo) & (kc < lo + NA_KW)
    for hh in range(hps):
        tiles = []
        for off in range(2 * NA_KH - 1):
            v = jnp.broadcast_to(toep_ref[hh, off:off + 1, :], (GRID_W, 128))
            t = pltpu.roll(v, 128 - (NA_KW - 1), 1, stride=1, stride_axis=0)[:, :GRID_W]
            tiles.append(jnp.where(col_valid, t, NEG_INF))
        for r in range(GRID_H):
            rs = min(max(r - NA_KH // 2, 0), GRID_H - NA_KH)
            for kr in range(GRID_H):
                tile = tiles[kr - r + NA_KH - 1] if rs <= kr < rs + NA_KH else neg
                bias_scr[hh, r * GRID_W:(r + 1) * GRID_W, kr * GRID_W:(kr + 1) * GRID_W] = tile


def _attn_kernel(*refs, plan, n_kv, bias_mode, has_sink, hps, groups, dk, dv, qb, c):
    it = iter(refs)
    q_ref = next(it)
    kv_refs = [(next(it), next(it)) for _ in range(n_kv)]
    bias_ref = next(it) if bias_mode is not None else None
    sink_ref = next(it) if has_sink else None
    o_ref = next(it)
    bias_scr = next(it) if bias_mode == "na" else None
    hblk = pl.program_id(0)

    if bias_mode == "na":
        @pl.when(pl.program_id(1) == 0)
        def _():
            _build_na_bias(bias_ref, bias_scr, hps)

    def one_head(hh, head, q_cols, o_cols):
        sink = sink_ref[hblk * hps * groups + head] if has_sink else None
        for qi, segs in enumerate(plan):
            q0 = qi * qb
            q = q_ref[q0:q0 + qb, q_cols]
            scores = []
            for (si, start, length, biased) in segs:
                k = kv_refs[si][0][start:start + length, hh * dk:(hh + 1) * dk].astype(BF16)
                t = lax.dot_general(q, k, _NT, preferred_element_type=F32) * c
                if biased and bias_mode == "na":
                    t = t + bias_scr[hh, q0:q0 + qb, start:start + length]
                elif biased:
                    t = t + bias_ref[qi, :, :length]
                scores.append(t)
            m = jnp.max(scores[0], axis=-1, keepdims=True)
            for t in scores[1:]:
                m = jnp.maximum(m, jnp.max(t, axis=-1, keepdims=True))
            if has_sink:
                m = jnp.maximum(m, sink)
            acc = None
            for t, (si, start, length, _) in zip(scores, segs):
                p = jnp.exp2(t - m).astype(BF16)
                v = kv_refs[si][1][start:start + length, hh * dv:(hh + 1) * dv].astype(BF16)
                v1 = jnp.concatenate([v, jnp.ones((length, dv), BF16)], axis=-1)
                pv = jnp.dot(p, v1, preferred_element_type=F32)
                acc = pv if acc is None else acc + pv
            denom = acc[:, dv:]
            if has_sink:
                denom = denom + jnp.exp2(sink - m)
            o_ref[q0:q0 + qb, o_cols] = (acc[:, :dv] / denom).astype(o_ref.dtype)

    for hh in range(hps):
        if groups > 1 and len(plan) > 1:
            def group_body(g, carry, hh=hh):
                head = hh * groups + g
                one_head(hh, head, pl.ds(pl.multiple_of(head * dk, dk), dk), pl.ds(pl.multiple_of(head * dv, dv), dv))
                return carry

            lax.fori_loop(0, groups, group_body, 0, unroll=2)
        else:
            for g in range(groups):
                head = hh * groups + g
                one_head(hh, head, slice(head * dk, (head + 1) * dk), slice(head * dv, (head + 1) * dv))


def _attention(q, kvs, plan, *, nb, sq, q_rowblk0, n_kv_heads, hps, groups, dk, dv, scale,
               bias=None, bias_mode=None, sink=None, name="attention"):
    grid = (n_kv_heads // hps, nb)
    in_specs = [pl.BlockSpec((sq, hps * groups * dk), lambda h, b: (q_rowblk0 + b, h))]
    args = [q]
    for (k, v, sk, r0) in kvs:
        in_specs.append(pl.BlockSpec((sk, hps * dk), lambda h, b, r0=r0: (r0 + b, h)))
        in_specs.append(pl.BlockSpec((sk, hps * dv), lambda h, b, r0=r0: (r0 + b, h)))
        args += [k, v]
    scratch = []
    if bias_mode == "na":
        in_specs.append(pl.BlockSpec((hps,) + bias.shape[1:], lambda h, b: (h, 0, 0)))
        args.append(bias)
        scratch.append(pltpu.VMEM((hps, LAT_S, LAT_S), F32))
    elif bias_mode == "table":
        in_specs.append(pl.BlockSpec(bias.shape, lambda h, b: (0, 0, 0)))
        args.append(bias)
    if sink is not None:
        in_specs.append(pl.BlockSpec(memory_space=pltpu.SMEM))
        args.append(sink)
    body = functools.partial(
        _attn_kernel, plan=plan, n_kv=len(kvs), bias_mode=bias_mode, has_sink=sink is not None,
        hps=hps, groups=groups, dk=dk, dv=dv, qb=min(Q_BLOCK, sq), c=scale * LOG2E)
    return pl.pallas_call(
        body,
        grid=grid,
        in_specs=in_specs,
        out_specs=pl.BlockSpec((sq, hps * groups * dv), lambda h, b: (b, h)),
        out_shape=jax.ShapeDtypeStruct((nb * sq, n_kv_heads * groups * dv), BF16),
        scratch_shapes=scratch,
        compiler_params=_cparams(2),
        name=name,
    )(*args)


CTX_PER_STEP = 4
_CTX_ROWS = CTX_PER_STEP * CTX_S
_CTX_PLAN = [[(0, i * CTX_S, CTX_S, False)] for i in range(CTX_PER_STEP)]
_N_QB = LAT_S // Q_BLOCK
_MLA_PLAN = [[(0, 0, LAT_S, False), (1, 0, PAST, False)] for _ in range(_N_QB)]
_NA_ROWS = [(0, 8), (0, 12), (4, 16), (8, 16)]
_NA_PLAN = [[(0, lo * GRID_W, (hi - lo) * GRID_W, True), (1, 0, PAST, False)] for lo, hi in _NA_ROWS]
_WIN_SPANS = [(max(i * Q_BLOCK - WINDOW, 0), min((i + 1) * Q_BLOCK + WINDOW, LAT_S)) for i in range(_N_QB)]
_WIN_PLAN = [[(0, lo, hi - lo, True), (1, 0, PAST, False)] for lo, hi in _WIN_SPANS]


def _window_bias():
    width = max(hi - lo for lo, hi in _WIN_SPANS)
    out = np.full((_N_QB, Q_BLOCK, width), NEG_INF, np.float32)
    for i, (lo, hi) in enumerate(_WIN_SPANS):
        q_abs = i * Q_BLOCK + np.arange(Q_BLOCK)[:, None]
        k_abs = lo + np.arange(hi - lo)[None, :]
        out[i, :, :hi - lo] = np.where(np.abs(q_abs - k_abs) <= WINDOW, 0.0, NEG_INF)
    return jnp.asarray(out)


def _rope_tables(rot_dim):
    half = rot_dim // 2
    nf = half // 2
    t = np.arange(LAT_S)
    inv_freq = ROPE_THETA ** (-np.arange(nf, dtype=np.float64) / nf)
    cos = np.zeros((2 * LAT_S, 128), np.float64)
    sin = np.zeros((2 * LAT_S, 128), np.float64)
    cos[:LAT_S, :rot_dim] = 1.0
    for part, pos in enumerate((t // GRID_W, t % GRID_W)):
        ang = pos[:, None].astype(np.float64) * inv_freq[None, :]
        lo = part * half
        cos[LAT_S:, lo:lo + nf] = np.cos(ang)
        cos[LAT_S:, lo + nf:lo + half] = np.cos(ang)
        sin[LAT_S:, lo:lo + nf] = -np.sin(ang)
        sin[LAT_S:, lo + nf:lo + half] = np.sin(ang)
    return jnp.asarray(cos, F32), jnp.asarray(sin, F32)


def _na_bias_rows(rpb):
    n_off, n_rel = rpb.shape[1], rpb.shape[2]
    return jnp.pad(rpb * LOG2E, ((0, 0), (0, 16 - n_off), (0, 128 - n_rel)))


def _even_mixer(x, h, mods, g_next, w, caches):
    cache_ckv, cache_krope, cache_nak, cache_nav = caches
    qm, km, vm, qna, kna, vna, new_ckv, new_kr, new_nak, new_nav = _even_in(h, w)
    km_ctx, vm_ctx = _cache_kv(cache_ckv, cache_krope, w["wk"], w["wv"], w["gkn"], w["gkr"])
    s_mla = 1.0 / math.sqrt(QK)
    s_na = 1.0 / math.sqrt(HEAD)
    lat0 = T_CTX // LAT_S
    n_ctx_steps = CTX_B // CTX_PER_STEP
    o_mla_ctx = _attention(qm, [(km, vm, _CTX_ROWS, 0)], _CTX_PLAN, nb=n_ctx_steps, sq=_CTX_ROWS, q_rowblk0=0,
                           n_kv_heads=HA, hps=HA, groups=1, dk=QK_PAD, dv=HEAD, scale=s_mla, name="mla_ctx")
    o_na_ctx = _attention(qna, [(kna, vna, _CTX_ROWS, 0)], _CTX_PLAN, nb=n_ctx_steps, sq=_CTX_ROWS, q_rowblk0=0,
                          n_kv_heads=HB, hps=HB, groups=1, dk=HEAD, dv=HEAD, scale=s_na, name="na_ctx")
    o_mla_lat = _attention(qm, [(km, vm, LAT_S, lat0), (km_ctx, vm_ctx, PAST, 0)], _MLA_PLAN, nb=LAT_B, sq=LAT_S,
                           q_rowblk0=lat0, n_kv_heads=HA, hps=8, groups=1, dk=QK_PAD, dv=HEAD, scale=s_mla,
                           name="mla_lat")
    o_na_lat = _attention(qna, [(kna, vna, LAT_S, lat0), (cache_nak, cache_nav, PAST, 0)], _NA_PLAN, nb=LAT_B,
                          sq=LAT_S, q_rowblk0=lat0, n_kv_heads=HB, hps=4, groups=1, dk=HEAD, dv=HEAD, scale=s_na,
                          bias=w["na_toep"], bias_mode="na", name="na_lat")
    x, h = _out_proj([o_mla_ctx, o_na_ctx], [o_mla_lat, o_na_lat], w["w_out"], w["w_out_lead"], x, mods, g_next)
    return x, h, (new_ckv.reshape(CTX_B, CTX_S, KV_LORA), new_kr.reshape(CTX_B, CTX_S, ROPE),
                  new_nak.reshape(CTX_B, CTX_S, HB, HEAD), new_nav.reshape(CTX_B, CTX_S, HB, HEAD))


def _odd_mixer(x, h, mods, g_next, w, caches):
    cache_k, cache_v = caches
    q, k, v, new_k, new_v = _odd_in(h, w)
    scale = 1.0 / math.sqrt(HEAD)
    lat0 = T_CTX // LAT_S
    o_ctx = _attention(q, [(k, v, CTX_S, 0)], _CTX_PLAN[:1], nb=CTX_B, sq=CTX_S, q_rowblk0=0, n_kv_heads=KVH_C,
                       hps=KVH_C, groups=GROUPS_C, dk=HEAD, dv=HEAD, scale=scale, sink=w["sink"], name="gqa_ctx")
    o_lat = _attention(q, [(k, v, LAT_S, lat0), (cache_k, cache_v, PAST, 0)], _WIN_PLAN, nb=LAT_B, sq=LAT_S,
                       q_rowblk0=lat0, n_kv_heads=KVH_C, hps=1, groups=GROUPS_C, dk=HEAD, dv=HEAD, scale=scale,
                       bias=_window_bias(), bias_mode="table", sink=w["sink"], name="gqa_lat")
    x, h = _out_proj([o_ctx], [o_lat], w["w_out"], w["w_out_lead"], x, mods, g_next)
    return x, h, (new_k.reshape(CTX_B, CTX_S, KVH_C, HEAD), new_v.reshape(CTX_B, CTX_S, KVH_C, HEAD))


def _even_weights(e, even_w_in, w_out_bf16, mla_q_norm, mla_w_q_up, mla_kv_norm, mla_w_kv_up, mla_qk_norm,
                  na_qk_norm, na_rpb):
    w_in = even_w_in[e]
    i0, i1, i2 = Q_LORA, Q_LORA + KV_LORA, Q_LORA + KV_LORA + ROPE
    w_in = jnp.concatenate([w_in[:, :i1], w_in[:, i2:], w_in[:, i1:i2], jnp.zeros((D, 128 - ROPE), F32)], axis=1)
    wq = jnp.pad(mla_w_q_up[e].reshape(Q_LORA, HA, QK), ((0, 0), (0, 0), (0, QK_PAD - QK)))
    wkv = mla_w_kv_up[e].reshape(KV_LORA, HA, NOPE + HEAD)
    qk = mla_qk_norm[e]
    return {
        "w_in": w_in.astype(BF16),
        "w_out": w_out_bf16,
        "w_out_lead": (e,),
        "wq": wq.reshape(Q_LORA, HA * QK_PAD).astype(BF16),
        "wk": wkv[:, :, :NOPE].reshape(KV_LORA, HA * NOPE).astype(BF16),
        "wv": wkv[:, :, NOPE:].reshape(KV_LORA, HA * HEAD).astype(BF16),
        "q_norm": mla_q_norm[e][None, :],
        "kv_norm": mla_kv_norm[e][None, :],
        "gq": jnp.pad(qk[0], (0, QK_PAD - QK))[None, :],
        "gkn": qk[1, :NOPE][None, :],
        "gkr": jnp.pad(qk[1, NOPE:], (0, 128 - ROPE))[None, :],
        "gnaq": na_qk_norm[e, 0][None, :],
        "gnak": na_qk_norm[e, 1][None, :],
        "na_toep": _na_bias_rows(na_rpb[e]),
    }


def kernel(x_prompt, x_sample, cache_mla_ckv, cache_mla_krope, cache_na_k, cache_na_v, cache_gqa_k, cache_gqa_v, c, c_ctx, ada_w, ada_b, norm_g, ffn_w_in, ffn_w_out, even_w_in, even_w_out, mla_q_norm, mla_w_q_up, mla_kv_norm, mla_w_kv_up, mla_qk_norm, na_qk_norm, na_rpb, odd_w_in, odd_w_out, gqa_qk_norm, gqa_sink):
    depth = ada_w.shape[0]
    cond = jnp.concatenate([c_ctx[None, :], c, jnp.zeros((16 - N_GROUPS, D), F32)], axis=0)
    mods_all = _ada_modulation(cond, ada_w, ada_b)
    layer_mods = [mods_all[layer, :N_GROUPS].reshape(N_GROUPS, N_MOD, D) for layer in range(depth)]
    even_w_out16 = even_w_out.astype(BF16)
    odd_w_out16 = odd_w_out.astype(BF16)
    n_ctx_tiles = T_CTX // FFN_OUT_TM

    ckv_l, kr_l, nak_l, nav_l, gk_l, gv_l = [], [], [], [], [], []
    y_prompt = y_sample = None
    x, h = _embed(x_prompt, x_sample, layer_mods[0], norm_g[0, 0:1])
    for layer in range(depth):
        mods = layer_mods[layer]
        g = norm_g[layer]
        e = layer // 2
        act, w_out16 = _ffn_in(h, ffn_w_in, ffn_w_out, (layer, 0))
        x, h = _ffn_out(act, w_out16, x, mods, 2, nxt=(mods, g[1:2], 1))
        if layer % 2 == 0:
            w = _even_weights(e, even_w_in, even_w_out16, mla_q_norm, mla_w_q_up, mla_kv_norm, mla_w_kv_up,
                              mla_qk_norm, na_qk_norm, na_rpb)
            caches = (cache_mla_ckv[:, e].reshape(LAT_B * PAST, KV_LORA),
                      jnp.pad(cache_mla_krope[:, e].reshape(LAT_B * PAST, ROPE), ((0, 0), (0, 128 - ROPE))),
                      cache_na_k[:, e].reshape(LAT_B * PAST, HB * HEAD),
                      cache_na_v[:, e].reshape(LAT_B * PAST, HB * HEAD))
            x, h, (ckv, kr, nak, nav) = _even_mixer(x, h, mods, g[2:3], w, caches)
            ckv_l.append(ckv)
            kr_l.append(kr)
            nak_l.append(nak)
            nav_l.append(nav)
        else:
            w = {"w_in": odd_w_in[e].astype(BF16), "w_out": odd_w_out16, "w_out_lead": (e,),
                 "gq": gqa_qk_norm[e, 0][None, :], "gk": gqa_qk_norm[e, 1][None, :],
                 "sink": gqa_sink[e] * LOG2E}
            caches = (cache_gqa_k[:, e].reshape(LAT_B * PAST, KVH_C * HEAD),
                      cache_gqa_v[:, e].reshape(LAT_B * PAST, KVH_C * HEAD))
            x, h, (gk, gv) = _odd_mixer(x, h, mods, g[2:3], w, caches)
            gk_l.append(gk)
            gv_l.append(gv)
        act, w_out16 = _ffn_in(h, ffn_w_in, ffn_w_out, (layer, 1))
        if layer + 1 < depth:
            x, h = _ffn_out(act, w_out16, x, mods, 8, nxt=(layer_mods[layer + 1], norm_g[layer + 1, 0:1], 0))
        else:
            y_prompt = _ffn_out(act, w_out16, x, mods, 8, tile0=0, n_tiles=n_ctx_tiles).reshape(CTX_B, CTX_S, D)
            y_sample = _ffn_out(act, w_out16, x, mods, 8, tile0=n_ctx_tiles,
                                n_tiles=T // FFN_OUT_TM - n_ctx_tiles).reshape(LAT_B, LAT_S, D)

    return (y_prompt, y_sample, jnp.stack(ckv_l, axis=1), jnp.stack(kr_l, axis=1), jnp.stack(nak_l, axis=1),
            jnp.stack(nav_l, axis=1), jnp.stack(gk_l, axis=1), jnp.stack(gv_l, axis=1))
```
